```python
import math
import jax
import jax.numpy as jnp
from jax import lax
import numpy as np

D_MODEL = 1024
BATCH = 8
SEQ = 8192
DEPTH = 2

S5_GROUPS = 16
S5_GROUP_DIM = 16
S5_WIDTH = S5_GROUPS * S5_GROUP_DIM
S5_STATE = 64
S5_DT_MIN = 1e-3
S5_DT_MAX = 1e-1

MLSTM_HEADS = 4
MLSTM_HEAD_DIM = 96
MLSTM_WIDTH = MLSTM_HEADS * MLSTM_HEAD_DIM
MLSTM_CHUNK = 128
CONV_WIDTH = 4

FOX_HEADS = 6
FOX_HEAD_DIM = 64
FOX_WIDTH = FOX_HEADS * FOX_HEAD_DIM
FOX_BLOCK = 128

N_BRANCHES = 3
FFN_HIDDEN = -(-8 * D_MODEL // (3 * 256)) * 256
EPS = 1e-6

IN_SIZES = (S5_WIDTH,
            MLSTM_WIDTH,
            MLSTM_WIDTH,
            MLSTM_WIDTH,
            MLSTM_HEADS,
            MLSTM_HEADS,
            FOX_WIDTH,
            FOX_WIDTH,
            FOX_WIDTH,
            FOX_HEADS,
            N_BRANCHES * D_MODEL)
N_IN = S5_WIDTH + 3 * MLSTM_WIDTH + 2 * MLSTM_HEADS + 3 * FOX_WIDTH + FOX_HEADS + N_BRANCHES * D_MODEL

kernel_name = 'hybrid_s5_mlstm_fox_adaln'


def _in_offset(idx):
    return sum(IN_SIZES[:idx])


def _split_in(z):
    parts = []
    start = 0
    for n in IN_SIZES:
        parts.append(z[..., start:start + n])
        start += n
    return parts


def _rmsnorm(x, g):
    xf = x.astype(jnp.float32)
    y = xf * lax.rsqrt(jnp.mean(xf * xf, axis=-1, keepdims=True) + EPS)
    return (y * g.astype(jnp.float32)).astype(x.dtype)


def _complex_combine(e1, e2):
    a1r, a1i, b1r, b1i = e1
    a2r, a2i, b2r, b2i = e2
    ar = a2r * a1r - a2i * a1i
    ai = a2r * a1i + a2i * a1r
    br = a2r * b1r - a2i * b1i + b2r
    bi = a2r * b1i + a2i * b1r + b2i
    return (ar, ai, br, bi)


def _s5(u, a_re, a_im, log_dt, b_re, b_im, c_re, c_im, d, w_glu):
    bsz, seq, _ = u.shape
    f32 = jnp.float32
    uf = u.astype(f32).reshape(bsz, seq, S5_GROUPS, S5_GROUP_DIM)
    are, aim = a_re.astype(f32), a_im.astype(f32)
    dt = jnp.exp(log_dt.astype(f32))[:, None]
    mag = jnp.exp(dt * are)
    ang = dt * aim
    abar_r, abar_i = mag * jnp.cos(ang), mag * jnp.sin(ang)
    den = are * are + aim * aim
    p, q = abar_r - 1.0, abar_i
    coef_r = (p * are + q * aim) / den
    coef_i = (q * are - p * aim) / den
    br, bi = b_re.astype(f32), b_im.astype(f32)
    bbar_r = coef_r[..., None] * br - coef_i[..., None] * bi
    bbar_i = coef_r[..., None] * bi + coef_i[..., None] * br
    bu_r = jnp.einsum('blgp,gnp->blgn', uf, bbar_r)
    bu_i = jnp.einsum('blgp,gnp->blgn', uf, bbar_i)
    a_r = jnp.broadcast_to(abar_r, (1, seq) + abar_r.shape)
    a_i = jnp.broadcast_to(abar_i, (1, seq) + abar_i.shape)
    _, _, s_r, s_i = lax.associative_scan(_complex_combine, (a_r, a_i, bu_r, bu_i), axis=1)
    y = (jnp.einsum('blgn,gpn->blgp', s_r, c_re.astype(f32))
         - jnp.einsum('blgn,gpn->blgp', s_i, c_im.astype(f32)))
    y = y.reshape(bsz, seq, S5_WIDTH) + d.astype(f32) * uf.reshape(bsz, seq, S5_WIDTH)
    y = jax.nn.gelu(y)
    y = y * jax.nn.sigmoid(y @ w_glu.astype(f32))
    return y.astype(u.dtype)


def _causal_conv(x, w, b):
    seq = x.shape[1]
    xp = jnp.pad(x, ((0, 0), (CONV_WIDTH - 1, 0), (0, 0)))
    y = b
    for tap in range(CONV_WIDTH):
        y = y + w[tap] * xp[:, tap:tap + seq]
    return y


def _mlstm(qk_in, v_in, o_in, i_pre, f_pre, conv_w, conv_b, wq, wk, norm_g):
    bsz, seq, _ = qk_in.shape
    f32 = jnp.float32
    H, dh, C = MLSTM_HEADS, MLSTM_HEAD_DIM, MLSTM_CHUNK
    nc = seq // C
    cx = jax.nn.silu(_causal_conv(qk_in.astype(f32), conv_w.astype(f32), conv_b.astype(f32)))
    cx = cx.reshape(bsz, seq, H, dh)
    q = jnp.einsum('blhd,hde->bhle', cx, wq.astype(f32))
    k = jnp.einsum('blhd,hde->bhle', cx, wk.astype(f32)) * (dh ** -0.5)
    v = v_in.astype(f32).reshape(bsz, seq, H, dh).transpose(0, 2, 1, 3)
    li = i_pre.astype(f32).transpose(0, 2, 1)
    lf = jax.nn.log_sigmoid(f_pre.astype(f32)).transpose(0, 2, 1)

    def to_chunks(t):
        return jnp.moveaxis(t.reshape((bsz, H, nc, C) + t.shape[3:]), 2, 0)

    causal = jnp.tril(jnp.ones((C, C), dtype=bool))

    def step(carry, xs):
        c_st, n_st, m_st = carry
        qc, kc, vc, lic, lfc = xs
        b = jnp.cumsum(lfc, axis=-1)
        dmat = b[..., :, None] - b[..., None, :] + lic[..., None, :]
        dmat = jnp.where(causal, dmat, -jnp.inf)
        m_inter = b + m_st[..., None]
        m_t = jnp.maximum(m_inter, jnp.max(dmat, axis=-1))
        s = jnp.einsum('bhtd,bhsd->bhts', qc, kc) * jnp.exp(dmat - m_t[..., None])
        inter = jnp.exp(m_inter - m_t)
        num = (jnp.einsum('bhts,bhsd->bhtd', s, vc)
               + inter[..., None] * jnp.einsum('bhvk,bhtk->bhtv', c_st, qc))
        den = jnp.sum(s, axis=-1) + inter * jnp.einsum('bhk,bhtk->bht', n_st, qc)
        h = num / jnp.maximum(jnp.abs(den), jnp.exp(-m_t))[..., None]
        b_last = b[..., -1]
        w = b_last[..., None] - b + lic
        m_new = jnp.maximum(b_last + m_st, jnp.max(w, axis=-1))
        decay = jnp.exp(b_last + m_st - m_new)
        ws = jnp.exp(w - m_new[..., None])
        c_new = decay[..., None, None] * c_st + jnp.einsum('bhsv,bhsk->bhvk', vc * ws[..., None], kc)
        n_new = decay[..., None] * n_st + jnp.einsum('bhs,bhsk->bhk', ws, kc)
        return (c_new, n_new, m_new), h

    init = (jnp.zeros((bsz, H, dh, dh), f32), jnp.zeros((bsz, H, dh), f32), jnp.zeros((bsz, H), f32))
    _, hs = lax.scan(step, init, (to_chunks(q), to_chunks(k), to_chunks(v), to_chunks(li), to_chunks(lf)))
    h = jnp.moveaxis(hs, 0, 2).reshape(bsz, H, seq, dh).transpose(0, 2, 1, 3)
    h = h * lax.rsqrt(jnp.mean(h * h, axis=-1, keepdims=True) + EPS)
    h = h.reshape(bsz, seq, MLSTM_WIDTH) * norm_g.astype(f32)
    h = jax.nn.sigmoid(o_in.astype(f32)) * h
    return h.astype(qk_in.dtype)


def _fox(q_in, k_in, v_in, f_pre):
    bsz, seq, _ = q_in.shape
    f32 = jnp.float32
    H, dh, BLK = FOX_HEADS, FOX_HEAD_DIM, FOX_BLOCK
    nb = seq // BLK

    def heads(t):
        return t.astype(f32).reshape(bsz, seq, H, dh).transpose(0, 2, 1, 3)

    q = heads(q_in) * (dh ** -0.5)
    k, v = heads(k_in), heads(v_in)
    logf = jax.nn.log_sigmoid(f_pre.astype(f32)).transpose(0, 2, 1)
    F = jnp.cumsum(logf, axis=-1)
    pos = jnp.arange(seq, dtype=jnp.int32)
    q_blocks = jnp.moveaxis(q.reshape(bsz, H, nb, BLK, dh), 2, 0)
    f_blocks = jnp.moveaxis(F.reshape(bsz, H, nb, BLK), 2, 0)
    pos_blocks = pos.reshape(nb, BLK)

    def block(args):
        qb, fb, pb = args
        s = jnp.einsum('bhtd,bhsd->bhts', qb, k) + (fb[..., None] - F[:, :, None, :])
        s = jnp.where(pb[:, None] >= pos[None, :], s, -jnp.inf)
        p = jax.nn.softmax(s, axis=-1)
        return jnp.einsum('bhts,bhsd->bhtd', p, v)

    out = lax.map(block, (q_blocks, f_blocks, pos_blocks))
    out = jnp.moveaxis(out, 0, 2).reshape(bsz, H, seq, dh).transpose(0, 2, 1, 3)
    return out.reshape(bsz, seq, FOX_WIDTH).astype(q_in.dtype)


def _mixer(h, w_in, b_in, s5_a_re, s5_a_im, s5_log_dt, s5_b_re, s5_b_im, s5_c_re, s5_c_im,
           s5_d, s5_w_glu, mlstm_conv_w, mlstm_conv_b, mlstm_wq, mlstm_wk, mlstm_norm_g,
           w_up_s5, w_up_mlstm, w_up_fox, w_out):
    z = h @ w_in + b_in
    (s5_u, m_qk, m_v, m_o, m_i, m_f, f_q, f_k, f_v, f_f, gates) = _split_in(z)
    y_s5 = _s5(s5_u, s5_a_re, s5_a_im, s5_log_dt, s5_b_re, s5_b_im, s5_c_re, s5_c_im,
               s5_d, s5_w_glu) @ w_up_s5
    y_m = _mlstm(m_qk, m_v, m_o, m_i, m_f, mlstm_conv_w, mlstm_conv_b, mlstm_wq, mlstm_wk,
                 mlstm_norm_g) @ w_up_mlstm
    y_f = _fox(f_q, f_k, f_v, f_f) @ w_up_fox
    g_s5, g_m, g_f = jnp.split(jax.nn.sigmoid(gates), N_BRANCHES, axis=-1)
    merged = g_s5 * y_s5 + g_m * y_m + g_f * y_f
    return merged @ w_out


def _swiglu(h, w1, w3, w2):
    return (jax.nn.silu(h @ w1) * (h @ w3)) @ w2


def setup_inputs(seed: int = 0) -> dict:
    key = jax.random.key(seed)
    ks = iter(jax.random.split(key, 32))

    def nrm(shape, scale):
        return scale * jax.random.normal(next(ks), shape, jnp.float32)

    G, N, P = S5_GROUPS, S5_STATE, S5_GROUP_DIM
    H, dh = MLSTM_HEADS, MLSTM_HEAD_DIM
    x = nrm((BATCH, SEQ, D_MODEL), 1.0)
    c = nrm((BATCH, D_MODEL), 1.0)
    mod_w = nrm((DEPTH, D_MODEL, 6 * D_MODEL), 0.5 * D_MODEL ** -0.5)
    mod_b = nrm((DEPTH, 6 * D_MODEL), 0.02)
    norm1_g = 1.0 + nrm((DEPTH, D_MODEL), 0.02)
    norm2_g = 1.0 + nrm((DEPTH, D_MODEL), 0.02)
    w_in = nrm((DEPTH, D_MODEL, N_IN), D_MODEL ** -0.5)
    b_in = nrm((DEPTH, N_IN), 0.02)
    off_mf = _in_offset(5)
    off_ff = _in_offset(9)
    b_in = b_in.at[:, off_mf:off_mf + MLSTM_HEADS].add(jnp.linspace(3.0, 6.0, MLSTM_HEADS, dtype=jnp.float32))
    b_in = b_in.at[:, off_ff:off_ff + FOX_HEADS].add(jnp.linspace(2.0, 5.0, FOX_HEADS, dtype=jnp.float32))
    s5_a_re = -0.5 + nrm((DEPTH, G, N), 0.01)
    s5_a_im = math.pi * jnp.arange(N, dtype=jnp.float32) + nrm((DEPTH, G, N), 0.01)
    s5_log_dt = jax.random.uniform(next(ks), (DEPTH, G), jnp.float32,
                                   math.log(S5_DT_MIN), math.log(S5_DT_MAX))
    s5_b_re = nrm((DEPTH, G, N, P), (2.0 * P) ** -0.5)
    s5_b_im = nrm((DEPTH, G, N, P), (2.0 * P) ** -0.5)
    s5_c_re = nrm((DEPTH, G, P, N), (2.0 * N) ** -0.5)
    s5_c_im = nrm((DEPTH, G, P, N), (2.0 * N) ** -0.5)
    s5_d = nrm((DEPTH, S5_WIDTH), 1.0)
    s5_w_glu = nrm((DEPTH, S5_WIDTH, S5_WIDTH), S5_WIDTH ** -0.5)
    mlstm_conv_w = nrm((DEPTH, CONV_WIDTH, MLSTM_WIDTH), CONV_WIDTH ** -0.5)
    mlstm_conv_b = nrm((DEPTH, MLSTM_WIDTH), 0.02)
    mlstm_wq = nrm((DEPTH, H, dh, dh), dh ** -0.5)
    mlstm_wk = nrm((DEPTH, H, dh, dh), dh ** -0.5)
    mlstm_norm_g = 1.0 + nrm((DEPTH, MLSTM_WIDTH), 0.02)
    w_up_s5 = nrm((DEPTH, S5_WIDTH, D_MODEL), S5_WIDTH ** -0.5)
    w_up_mlstm = nrm((DEPTH, MLSTM_WIDTH, D_MODEL), MLSTM_WIDTH ** -0.5)
    w_up_fox = nrm((DEPTH, FOX_WIDTH, D_MODEL), FOX_WIDTH ** -0.5)
    w_out = nrm((DEPTH, D_MODEL, D_MODEL), D_MODEL ** -0.5)
    ffn_w1 = nrm((DEPTH, D_MODEL, FFN_HIDDEN), D_MODEL ** -0.5)
    ffn_w3 = nrm((DEPTH, D_MODEL, FFN_HIDDEN), D_MODEL ** -0.5)
    ffn_w2 = nrm((DEPTH, FFN_HIDDEN, D_MODEL), FFN_HIDDEN ** -0.5)
    final_g = 1.0 + nrm((D_MODEL,), 0.02)
    return {'x': x, 'c': c, 'mod_w': mod_w, 'mod_b': mod_b, 'norm1_g': norm1_g, 'norm2_g': norm2_g,
            'w_in': w_in, 'b_in': b_in, 's5_a_re': s5_a_re, 's5_a_im': s5_a_im, 's5_log_dt': s5_log_dt,
            's5_b_re': s5_b_re, 's5_b_im': s5_b_im, 's5_c_re': s5_c_re, 's5_c_im': s5_c_im,
            's5_d': s5_d, 's5_w_glu': s5_w_glu, 'mlstm_conv_w': mlstm_conv_w, 'mlstm_conv_b': mlstm_conv_b,
            'mlstm_wq': mlstm_wq, 'mlstm_wk': mlstm_wk, 'mlstm_norm_g': mlstm_norm_g,
            'w_up_s5': w_up_s5, 'w_up_mlstm': w_up_mlstm, 'w_up_fox': w_up_fox, 'w_out': w_out,
            'ffn_w1': ffn_w1, 'ffn_w3': ffn_w3, 'ffn_w2': ffn_w2, 'final_g': final_g}


def reference(x, c, mod_w, mod_b, norm1_g, norm2_g, w_in, b_in, s5_a_re, s5_a_im, s5_log_dt,
              s5_b_re, s5_b_im, s5_c_re, s5_c_im, s5_d, s5_w_glu, mlstm_conv_w, mlstm_conv_b,
              mlstm_wq, mlstm_wk, mlstm_norm_g, w_up_s5, w_up_mlstm, w_up_fox, w_out,
              ffn_w1, ffn_w3, ffn_w2, final_g):
    cf = jax.nn.silu(c)
    for l in range(DEPTH):
        mod = cf @ mod_w[l] + mod_b[l]
        sh1, sc1, g1, sh2, sc2, g2 = [m[:, None, :] for m in jnp.split(mod, 6, axis=-1)]
        h = _rmsnorm(x, norm1_g[l]) * (1.0 + sc1) + sh1
        x = x + g1 * _mixer(h, w_in[l], b_in[l], s5_a_re[l], s5_a_im[l], s5_log_dt[l],
                            s5_b_re[l], s5_b_im[l], s5_c_re[l], s5_c_im[l], s5_d[l], s5_w_glu[l],
                            mlstm_conv_w[l], mlstm_conv_b[l], mlstm_wq[l], mlstm_wk[l],
                            mlstm_norm_g[l], w_up_s5[l], w_up_mlstm[l], w_up_fox[l], w_out[l])
        h = _rmsnorm(x, norm2_g[l]) * (1.0 + sc2) + sh2
        x = x + g2 * _swiglu(h, ffn_w1[l], ffn_w3[l], ffn_w2[l])
    return _rmsnorm(x, final_g)
```

```python
import functools
import math

import jax
import jax.numpy as jnp
from jax import lax
from jax.experimental import pallas as pl
from jax.experimental.pallas import tpu as pltpu

F32 = jnp.float32
BF16 = jnp.bfloat16

LANES = 128
SUBLANES = 8
VMEM_LIMIT_BYTES = 56 * 1024 * 1024

NORM_EPS = 1e-6
S5_GROUPS = 16
S5_GROUP_DIM = 16
S5_STATE = 64
S5_WIDTH = S5_GROUPS * S5_GROUP_DIM
S5_CHANNELS = S5_GROUPS * S5_STATE
MLSTM_HEADS = 4
MLSTM_HEAD_DIM = 96
MLSTM_WIDTH = MLSTM_HEADS * MLSTM_HEAD_DIM
MLSTM_PAD_WIDTH = MLSTM_HEADS * LANES
MLSTM_CHUNK = 128
CONV_TAPS = 4
FOX_HEADS = 6
FOX_HEAD_DIM = 64
FOX_WIDTH = FOX_HEADS * FOX_HEAD_DIM
FOX_PAIRS = FOX_WIDTH // LANES
GATE_ROWS = 16
NEG_BIG = -1e30

NT_DIMS = (((1,), (1,)), ((), ()))


def _dot(a, b):
    return jnp.dot(a, b, preferred_element_type=F32)


def _dot_nt(a, b):
    return lax.dot_general(a, b, NT_DIMS, preferred_element_type=F32)


def _sigmoid(x):
    return 1.0 / (1.0 + jnp.exp(-x))


def _silu(x):
    return x * _sigmoid(x)


def _log_sigmoid(x):
    return jnp.minimum(x, 0.0) - jnp.log1p(jnp.exp(-jnp.abs(x)))


def _rms_modulate(x, gain, scale, shift):
    y = x * lax.rsqrt(jnp.mean(x * x, axis=-1, keepdims=True) + NORM_EPS)
    return (y * gain) * (1.0 + scale) + shift


def _params(semantics):
    return pltpu.CompilerParams(dimension_semantics=semantics, vmem_limit_bytes=VMEM_LIMIT_BYTES)


def _const_spec(shape):
    nd = len(shape)
    return pl.BlockSpec(shape, lambda *_: (0,) * nd, pipeline_mode=pl.Buffered(1))


def _row_tile(seq, want):
    t = min(want, seq)
    assert seq % t == 0 and t % SUBLANES == 0
    return t


def _mod_kernel(c_ref, w_ref, b_ref, o_ref):
    cf = _silu(c_ref[...]).astype(BF16)
    o_ref[0] = _dot(cf, w_ref[0].astype(BF16)) + b_ref[0]


def _modulation(c, mod_w, mod_b):
    depth, d, six_d = mod_w.shape
    bsz = c.shape[0]
    n_col = six_d // d
    return pl.pallas_call(
        _mod_kernel,
        grid=(depth, n_col),
        in_specs=[pl.BlockSpec((bsz, d), lambda l, j: (0, 0)),
                  pl.BlockSpec((1, d, d), lambda l, j: (l, 0, j)),
                  pl.BlockSpec((1, 1, d), lambda l, j: (l, 0, j))],
        out_specs=pl.BlockSpec((1, bsz, d), lambda l, j: (l, 0, j)),
        out_shape=jax.ShapeDtypeStruct((depth, bsz, six_d), F32),
        compiler_params=_params(("parallel", "parallel")),
        name="adaln_modulation",
    )(c, mod_w, mod_b.reshape(depth, 1, six_d))


IN_SEGMENTS = (S5_WIDTH, MLSTM_PAD_WIDTH, MLSTM_PAD_WIDTH, MLSTM_PAD_WIDTH,
               FOX_WIDTH, FOX_WIDTH, FOX_WIDTH)
DOT_COLS = 512


def _inproj_kernel(x_ref, sh_ref, sc_ref, g_ref, w_ref, b_ref, wt_ref, bt_ref, *out_refs):
    h = _rms_modulate(x_ref[0], g_ref[...], sc_ref[0], sh_ref[0]).astype(BF16)
    seg_refs, small_ref = out_refs[:-1], out_refs[-1]
    start = 0
    for o_ref in seg_refs:
        width = o_ref.shape[-1]
        for c0 in range(0, width, DOT_COLS):
            cw = min(DOT_COLS, width - c0)
            z = _dot(h, w_ref[:, start + c0:start + c0 + cw]) + b_ref[:, start + c0:start + c0 + cw]
            o_ref[0, :, c0:c0 + cw] = z.astype(o_ref.dtype)
        start += width
    small_ref[0] = _dot_nt(wt_ref[...], h) + bt_ref[...]


def _input_projection(x, shift, scale, gain, w_cat, b_cat, w_small_t, b_small_t, tm):
    bsz, seq, d = x.shape
    widths = IN_SEGMENTS + (w_cat.shape[1] - sum(IN_SEGMENTS),)
    tok = lambda b, i: (b, i, 0)
    vec = lambda b, i: (b, 0, 0)
    out_shape = [jax.ShapeDtypeStruct((bsz, seq, w), F32) for w in widths]
    out_shape.append(jax.ShapeDtypeStruct((bsz, GATE_ROWS, seq), F32))
    out_specs = [pl.BlockSpec((1, tm, w), tok) for w in widths]
    out_specs.append(pl.BlockSpec((1, GATE_ROWS, tm), lambda b, i: (b, 0, i)))
    return pl.pallas_call(
        _inproj_kernel,
        grid=(bsz, seq // tm),
        in_specs=[pl.BlockSpec((1, tm, d), tok),
                  pl.BlockSpec((1, 1, d), vec), pl.BlockSpec((1, 1, d), vec),
                  _const_spec(gain.shape), _const_spec(w_cat.shape), _const_spec(b_cat.shape),
                  _const_spec(w_small_t.shape), _const_spec(b_small_t.shape)],
        out_specs=out_specs,
        out_shape=out_shape,
        compiler_params=_params(("parallel", "parallel")),
        name="norm_input_projection",
    )(x, shift, scale, gain, w_cat, b_cat, w_small_t, b_small_t)


def _gate_scan_kernel(x_ref, o_ref):
    x = x_ref[0]
    seq = x.shape[1]
    seg = MLSTM_CHUNK
    lane = lax.broadcasted_iota(jnp.int32, x.shape, 1) & (seg - 1)
    v = _log_sigmoid(x)
    k = 1
    while k < seg:
        v = v + jnp.where(lane >= k, pltpu.roll(v, k, 1), 0.0)
        k *= 2
    row = lax.broadcasted_iota(jnp.int32, (GATE_ROWS, seg), 0)
    carry = jnp.zeros((GATE_ROWS, 1), F32)
    for j in range(seq // seg):
        sl = slice(j * seg, (j + 1) * seg)
        local = v[:, sl]
        o_ref[0, :, sl] = jnp.where(row < MLSTM_HEADS, x[:, sl],
                                    jnp.where(row < 2 * MLSTM_HEADS, local, local + carry))
        carry = carry + local[:, seg - 1:seg]


def _gate_scan(pre):
    bsz, rows, seq = pre.shape
    spec = pl.BlockSpec((1, rows, seq), lambda b: (b, 0, 0))
    return pl.pallas_call(
        _gate_scan_kernel, grid=(bsz,), in_specs=[spec], out_specs=spec,
        out_shape=jax.ShapeDtypeStruct(pre.shape, F32),
        compiler_params=_params(("parallel",)),
        name="gate_scan",
    )(pre)


SCAN_SHIFTS = (1, 2, 4)


def _s5_kernel(u_ref, bmat_ref, cmat_ref, astep_ref, apow_ref, d_ref, wglu_ref, o_ref,
               st_ref, carry_ref):
    nch = S5_CHANNELS

    @pl.when(pl.program_id(1) == 0)
    def _():
        carry_ref[...] = jnp.zeros_like(carry_ref)

    u = u_ref[0]
    st_ref[...] = _dot(u.astype(BF16), bmat_ref[...])
    n_blocks = u.shape[0] // SUBLANES

    def body(i, carry):
        cr, ci = carry
        r0 = pl.multiple_of(i * SUBLANES, SUBLANES)
        xr = st_ref[pl.ds(r0, SUBLANES), :nch]
        xi = st_ref[pl.ds(r0, SUBLANES), nch:]
        for k, shift in enumerate(SCAN_SHIFTS):
            ar, ai = astep_ref[k, :, :nch], astep_ref[k, :, nch:]
            sr, si = pltpu.roll(xr, shift, 0), pltpu.roll(xi, shift, 0)
            xr, xi = xr + (ar * sr - ai * si), xi + (ar * si + ai * sr)
        pr, pi = apow_ref[:, :nch], apow_ref[:, nch:]
        xr, xi = xr + (pr * cr - pi * ci), xi + (pr * ci + pi * cr)
        st_ref[pl.ds(r0, SUBLANES), :nch] = xr
        st_ref[pl.ds(r0, SUBLANES), nch:] = xi
        last = SUBLANES - 1
        return (jnp.broadcast_to(xr[last:, :], xr.shape), jnp.broadcast_to(xi[last:, :], xi.shape))

    cr, ci = lax.fori_loop(0, n_blocks, body, (carry_ref[:, :nch], carry_ref[:, nch:]))
    carry_ref[:, :nch] = cr
    carry_ref[:, nch:] = ci

    y = _dot(st_ref[...].astype(BF16), cmat_ref[...]) + d_ref[...] * u
    y = y * (0.5 * (1.0 + jnp.tanh(math.sqrt(2.0 / math.pi) * (y + 0.044715 * (y * y * y)))))
    o_ref[0] = y * _sigmoid(_dot(y.astype(BF16), wglu_ref[...]))


def _s5(u, bmat, cmat, astep, apow, d_skip, w_glu, tb):
    bsz, seq, width = u.shape
    tok = lambda b, i: (b, i, 0)
    return pl.pallas_call(
        _s5_kernel,
        grid=(bsz, seq // tb),
        in_specs=[pl.BlockSpec((1, tb, width), tok)] + [
            _const_spec(a.shape) for a in (bmat, cmat, astep, apow, d_skip, w_glu)],
        out_specs=pl.BlockSpec((1, tb, width), tok),
        out_shape=jax.ShapeDtypeStruct(u.shape, F32),
        scratch_shapes=[pltpu.VMEM((tb, 2 * S5_CHANNELS), F32),
                        pltpu.VMEM((SUBLANES, 2 * S5_CHANNELS), F32)],
        compiler_params=_params(("parallel", "arbitrary")),
        name="s5_scan",
    )(u, bmat, cmat, astep, apow, d_skip, w_glu)


def _s5_tables(a_re, a_im, log_dt, b_re, b_im, c_re, c_im):
    g, n, p = b_re.shape
    dt = jnp.exp(log_dt)[:, None]
    mag = jnp.exp(dt * a_re)
    ang = dt * a_im
    abar_r, abar_i = mag * jnp.cos(ang), mag * jnp.sin(ang)
    den = a_re * a_re + a_im * a_im
    pr, qi = abar_r - 1.0, abar_i
    coef_r = (pr * a_re + qi * a_im) / den
    coef_i = (qi * a_re - pr * a_im) / den
    bbar_r = coef_r[..., None] * b_re - coef_i[..., None] * b_im
    bbar_i = coef_r[..., None] * b_im + coef_i[..., None] * b_re
    eye = jnp.eye(g, dtype=F32)

    def in_block(bb):
        return jnp.einsum('gnp,gh->gphn', bb, eye).reshape(g * p, g * n)

    def out_block(cc):
        return jnp.einsum('gpn,gh->gnhp', cc, eye).reshape(g * n, g * p)

    bmat = jnp.concatenate([in_block(bbar_r), in_block(bbar_i)], axis=1).astype(BF16)
    cmat = jnp.concatenate([out_block(c_re), -out_block(c_im)], axis=0).astype(BF16)

    def power(e):
        e = jnp.asarray(e, F32)[:, None, None]
        m = jnp.exp(e * (dt * a_re)[None])
        th = e * ang[None]
        return (m * jnp.cos(th)).reshape(-1, g * n), (m * jnp.sin(th)).reshape(-1, g * n)

    rows = jnp.arange(SUBLANES)
    sr, si = power(SCAN_SHIFTS)
    keep = (rows[None, :] >= jnp.asarray(SCAN_SHIFTS)[:, None]).astype(F32)[:, :, None]
    astep = jnp.concatenate([keep * sr[:, None, :], keep * si[:, None, :]], axis=-1)
    wr, wi = power(rows + 1)
    apow = jnp.concatenate([wr, wi], axis=-1)
    return bmat, cmat, astep, apow


def _mlstm_kernel(qk_ref, v_ref, o_ref, grow_ref, gcol_ref, cw_ref, cb_ref, wq_ref, wk_ref,
                  ng_ref, out_ref, tail_ref, ckv_ref, n_ref, m_ref):
    chunk = qk_ref.shape[1]
    nh = MLSTM_HEADS

    @pl.when(pl.program_id(1) == 0)
    def _():
        tail_ref[...] = jnp.zeros_like(tail_ref)
        ckv_ref[...] = jnp.zeros_like(ckv_ref)
        n_ref[...] = jnp.zeros_like(n_ref)
        m_ref[...] = jnp.zeros_like(m_ref)

    x = qk_ref[0]
    ext = jnp.concatenate([tail_ref[...], x], axis=0)
    conv = cb_ref[...] + cw_ref[CONV_TAPS - 1:CONV_TAPS, :] * x
    for tap in range(CONV_TAPS - 1):
        back = CONV_TAPS - 1 - tap
        conv = conv + cw_ref[tap:tap + 1, :] * ext[SUBLANES - back:SUBLANES - back + chunk, :]
    tail_ref[...] = x[chunk - SUBLANES:, :]
    cx = _silu(conv)

    row_id = lax.broadcasted_iota(jnp.int32, (chunk, chunk), 0)
    col_id = lax.broadcasted_iota(jnp.int32, (chunk, chunk), 1)
    causal = row_id >= col_id
    k_scale = MLSTM_HEAD_DIM ** -0.5
    for hd in range(nh):
        cols = slice(hd * LANES, (hd + 1) * LANES)
        cxh = cx[:, cols].astype(BF16)
        q = _dot(cxh, wq_ref[hd])
        k = _dot(cxh, wk_ref[hd]) * k_scale
        v = v_ref[0, :, cols]
        qb, kb = q.astype(BF16), k.astype(BF16)
        li_row = grow_ref[0, hd:hd + 1, :]
        b_row = grow_ref[0, nh + hd:nh + hd + 1, :]
        li_col = gcol_ref[0, :, hd:hd + 1]
        b_col = gcol_ref[0, :, nh + hd:nh + hd + 1]
        m_st = m_ref[hd:hd + 1, 0:1]
        n_st = n_ref[hd:hd + 1, :]

        dmat = jnp.where(causal, b_col - b_row + li_row, -jnp.inf)
        m_inter = b_col + m_st
        m_t = jnp.maximum(m_inter, jnp.max(dmat, axis=1, keepdims=True))
        s = _dot_nt(qb, kb) * jnp.exp(dmat - m_t)
        inter = jnp.exp(m_inter - m_t)
        num = _dot(s.astype(BF16), v.astype(BF16)) + inter * _dot(qb, ckv_ref[hd].astype(BF16))
        den = jnp.sum(s, axis=1, keepdims=True) + inter * jnp.sum(q * n_st, axis=1, keepdims=True)
        h = num / jnp.maximum(jnp.abs(den), jnp.exp(-m_t))

        b_last = b_row[:, chunk - 1:chunk]
        w_row = b_last - b_row + li_row
        w_col = b_last - b_col + li_col
        m_new = jnp.maximum(b_last + m_st, jnp.max(w_row, axis=1, keepdims=True))
        decay = jnp.exp(b_last + m_st - m_new)
        ws_col = jnp.exp(w_col - m_new)
        ckv_ref[hd] = decay * ckv_ref[hd] + _dot(k.T.astype(BF16), (v * ws_col).astype(BF16))
        n_ref[hd:hd + 1, :] = decay * n_st + jnp.sum(ws_col * k, axis=0, keepdims=True)
        m_ref[hd:hd + 1, :] = jnp.broadcast_to(m_new, (1, LANES))

        hn = h * lax.rsqrt(jnp.sum(h * h, axis=1, keepdims=True) * (1.0 / MLSTM_HEAD_DIM) + NORM_EPS)
        out_ref[0, :, cols] = _sigmoid(o_ref[0, :, cols]) * (hn * ng_ref[:, cols])


def _mlstm(qk, v, o, gate_rows, gate_cols, conv_w, conv_b, wq, wk, norm_g):
    bsz, seq, width = qk.shape
    chunk = MLSTM_CHUNK
    tok = lambda b, i: (b, i, 0)
    tok_spec = pl.BlockSpec((1, chunk, width), tok)
    return pl.pallas_call(
        _mlstm_kernel,
        grid=(bsz, seq // chunk),
        in_specs=[tok_spec, tok_spec, tok_spec,
                  pl.BlockSpec((1, GATE_ROWS, chunk), lambda b, i: (b, 0, i)),
                  pl.BlockSpec((1, chunk, GATE_ROWS), tok)] + [
            _const_spec(a.shape) for a in (conv_w, conv_b, wq, wk, norm_g)],
        out_specs=tok_spec,
        out_shape=jax.ShapeDtypeStruct(qk.shape, F32),
        scratch_shapes=[pltpu.VMEM((SUBLANES, width), F32),
                        pltpu.VMEM((MLSTM_HEADS, LANES, LANES), F32),
                        pltpu.VMEM((SUBLANES, LANES), F32),
                        pltpu.VMEM((SUBLANES, LANES), F32)],
        compiler_params=_params(("parallel", "arbitrary")),
        name="mlstm_chunkwise",
    )(qk, v, o, gate_rows, gate_cols, conv_w, conv_b, wq, wk, norm_g)


def _fox_kernel(qi_ref, ki_ref, q_ref, k_ref, v_ref, fcol_ref, frow_ref, o_ref,
                m_ref, l_ref, acc_ref):
    step = pl.program_id(2)
    qi, ki = qi_ref[step], ki_ref[step]
    tq, tk = q_ref.shape[1], k_ref.shape[1]

    @pl.when(ki == 0)
    def _():
        m_ref[...] = jnp.full_like(m_ref, NEG_BIG)
        l_ref[...] = jnp.zeros_like(l_ref)
        acc_ref[...] = jnp.zeros_like(acc_ref)

    q = q_ref[0] * (FOX_HEAD_DIM ** -0.5)
    k = k_ref[0].astype(BF16)
    v = v_ref[0].astype(BF16)
    lane = lax.broadcasted_iota(jnp.int32, (1, LANES), 1)
    row_id = lax.broadcasted_iota(jnp.int32, (tq, tk), 0) + qi * tq
    col_id = lax.broadcasted_iota(jnp.int32, (tq, tk), 1) + ki * tk
    visible = row_id >= col_id
    for hh in range(2):
        in_head = (lane >= hh * FOX_HEAD_DIM) & (lane < (hh + 1) * FOX_HEAD_DIM)
        qh = jnp.where(in_head, q, 0.0).astype(BF16)
        s = _dot_nt(qh, k) + (fcol_ref[0, 0, :, hh:hh + 1] - frow_ref[0, 0, hh:hh + 1, :])
        s = jnp.where(visible, s, NEG_BIG)
        m_prev = m_ref[hh]
        m_new = jnp.maximum(m_prev, jnp.max(s, axis=1, keepdims=True))
        alpha = jnp.exp(m_prev - m_new)
        p = jnp.exp(s - m_new)
        l_ref[hh] = alpha * l_ref[hh] + jnp.sum(p, axis=1, keepdims=True)
        acc_ref[hh] = alpha * acc_ref[hh] + _dot(p.astype(BF16), v)
        m_ref[hh] = m_new

    @pl.when(ki == qi * (tq // tk) + (tq // tk - 1))
    def _():
        o_ref[0] = jnp.where(lane < FOX_HEAD_DIM, acc_ref[0] / l_ref[0], acc_ref[1] / l_ref[1])


def _fox(q, k, v, fcol, frow, tq):
    bsz, seq, width = q.shape
    nq = seq // tq
    q_idx = [i for i in range(nq) for _ in range(i + 1)]
    k_idx = [j for i in range(nq) for j in range(i + 1)]
    q_tbl, k_tbl = jnp.asarray(q_idx, jnp.int32), jnp.asarray(k_idx, jnp.int32)
    grid_spec = pltpu.PrefetchScalarGridSpec(
        num_scalar_prefetch=2,
        grid=(bsz, FOX_PAIRS, len(q_idx)),
        in_specs=[pl.BlockSpec((1, tq, LANES), lambda b, p, s, qt, kt: (b, qt[s], p)),
                  pl.BlockSpec((1, tq, LANES), lambda b, p, s, qt, kt: (b, kt[s], p)),
                  pl.BlockSpec((1, tq, LANES), lambda b, p, s, qt, kt: (b, kt[s], p)),
                  pl.BlockSpec((1, 1, tq, 2), lambda b, p, s, qt, kt: (b, p, qt[s], 0)),
                  pl.BlockSpec((1, 1, 2, tq), lambda b, p, s, qt, kt: (b, p, 0, kt[s]))],
        out_specs=pl.BlockSpec((1, tq, LANES), lambda b, p, s, qt, kt: (b, qt[s], p)),
        scratch_shapes=[pltpu.VMEM((2, tq, 1), F32), pltpu.VMEM((2, tq, 1), F32),
                        pltpu.VMEM((2, tq, LANES), F32)],
    )
    return pl.pallas_call(
        _fox_kernel, grid_spec=grid_spec,
        out_shape=jax.ShapeDtypeStruct(q.shape, F32),
        compiler_params=_params(("parallel", "parallel", "arbitrary")),
        name="forgetting_attention",
    )(q_tbl, k_tbl, q, k, v, fcol, frow)


def _merge_kernel(x_ref, ys_ref, ym_ref, yf_ref, gate_ref, g1_ref, ws_ref, wm_ref, wf_ref,
                  wo_ref, o_ref):
    d = x_ref.shape[-1]
    merged = (_sigmoid(gate_ref[0, :, :d]) * _dot(ys_ref[0].astype(BF16), ws_ref[...])
              + _sigmoid(gate_ref[0, :, d:2 * d]) * _dot(ym_ref[0].astype(BF16), wm_ref[...])
              + _sigmoid(gate_ref[0, :, 2 * d:]) * _dot(yf_ref[0].astype(BF16), wf_ref[...]))
    o_ref[0] = x_ref[0] + g1_ref[0] * _dot(merged.astype(BF16), wo_ref[...])


def _merge(x, y_s5, y_m, y_f, gates, g1, w_up_s5, w_up_m, w_up_f, w_out, tm):
    bsz, seq, d = x.shape
    tok = lambda b, i: (b, i, 0)
    acts = (x, y_s5, y_m, y_f, gates)
    weights = (w_up_s5, w_up_m, w_up_f, w_out)
    return pl.pallas_call(
        _merge_kernel,
        grid=(bsz, seq // tm),
        in_specs=[pl.BlockSpec((1, tm, a.shape[-1]), tok) for a in acts]
        + [pl.BlockSpec((1, 1, d), lambda b, i: (b, 0, 0))]
        + [_const_spec(w.shape) for w in weights],
        out_specs=pl.BlockSpec((1, tm, d), tok),
        out_shape=jax.ShapeDtypeStruct(x.shape, F32),
        compiler_params=_params(("parallel", "parallel")),
        name="merge_output_projection",
    )(*acts, g1, *weights)


FFN_COLS = 256


def _ffn_kernel(x_ref, sh_ref, sc_ref, g2_ref, gain_ref, w1_ref, w3_ref, w2_ref, fg_ref, o_ref,
                *, final_norm):
    x = x_ref[0]
    h = _rms_modulate(x, gain_ref[...], sc_ref[0], sh_ref[0]).astype(BF16)
    hidden = w1_ref.shape[1]
    acc = jnp.zeros(x.shape, F32)
    for c0 in range(0, hidden, FFN_COLS):
        a = _dot(h, w1_ref[:, c0:c0 + FFN_COLS])
        b = _dot(h, w3_ref[:, c0:c0 + FFN_COLS])
        acc = acc + _dot((_silu(a) * b).astype(BF16), w2_ref[c0:c0 + FFN_COLS, :])
    y = x + g2_ref[0] * acc
    if final_norm:
        y = (y * lax.rsqrt(jnp.mean(y * y, axis=-1, keepdims=True) + NORM_EPS)) * fg_ref[...]
    o_ref[0] = y


def _ffn(x, shift, scale, g2, gain, w1, w3, w2, final_g, final_norm, tm):
    bsz, seq, d = x.shape
    tok = lambda b, i: (b, i, 0)
    vec = pl.BlockSpec((1, 1, d), lambda b, i: (b, 0, 0))
    return pl.pallas_call(
        functools.partial(_ffn_kernel, final_norm=final_norm),
        grid=(bsz, seq // tm),
        in_specs=[pl.BlockSpec((1, tm, d), tok), vec, vec, vec]
        + [_const_spec(a.shape) for a in (gain, w1, w3, w2, final_g)],
        out_specs=pl.BlockSpec((1, tm, d), tok),
        out_shape=jax.ShapeDtypeStruct(x.shape, F32),
        compiler_params=_params(("parallel", "parallel")),
        name="norm_swiglu",
    )(x, shift, scale, g2, gain, w1, w3, w2, final_g)


def _pad_heads(w, heads, head_dim):
    lead = w.shape[:-1]
    w = w.reshape(lead + (heads, head_dim))
    w = jnp.pad(w, [(0, 0)] * len(lead) + [(0, 0), (0, LANES - head_dim)])
    return w.reshape(lead + (heads * LANES,))


def _layer_weights(w_in, b_in):
    sizes = (S5_WIDTH, MLSTM_WIDTH, MLSTM_WIDTH, MLSTM_WIDTH, MLSTM_HEADS, MLSTM_HEADS,
             FOX_WIDTH, FOX_WIDTH, FOX_WIDTH, FOX_HEADS)
    offs = [0]
    for n in sizes:
        offs.append(offs[-1] + n)
    both = jnp.concatenate([w_in, b_in[None, :]], axis=0)
    part = [both[:, offs[i]:offs[i + 1]] for i in range(len(sizes))] + [both[:, offs[-1]:]]
    pad_m = lambda w: _pad_heads(w, MLSTM_HEADS, MLSTM_HEAD_DIM)
    cat = jnp.concatenate([part[0], pad_m(part[1]), pad_m(part[2]), pad_m(part[3]),
                           part[6], part[7], part[8], part[10]], axis=1)
    small = jnp.concatenate([part[4], part[5], part[9]], axis=1)
    small = jnp.pad(small, ((0, 0), (0, GATE_ROWS - small.shape[1]))).T
    return (cat[:-1].astype(BF16), cat[-1:], small[:, :-1].astype(BF16), small[:, -1:])


def kernel(x, c, mod_w, mod_b, norm1_g, norm2_g, w_in, b_in, s5_a_re, s5_a_im, s5_log_dt, s5_b_re, s5_b_im, s5_c_re, s5_c_im, s5_d, s5_w_glu, mlstm_conv_w, mlstm_conv_b, mlstm_wq, mlstm_wk, mlstm_norm_g, w_up_s5, w_up_mlstm, w_up_fox, w_out, ffn_w1, ffn_w3, ffn_w2, final_g):
    bsz, seq, d = x.shape
    depth = mod_w.shape[0]
    assert seq % MLSTM_CHUNK == 0
    tm = _row_tile(seq, 256)
    tq = _row_tile(seq, 512)
    ts = _row_tile(seq, 512)

    mod = _modulation(c, mod_w, mod_b)
    final_gain = final_g.reshape(1, d)
    for l in range(depth):
        sh1, sc1, g1, sh2, sc2, g2 = [mod[l, :, i * d:(i + 1) * d].reshape(bsz, 1, d)
                                      for i in range(6)]
        w_cat, b_cat, w_small_t, b_small_t = _layer_weights(w_in[l], b_in[l])
        (s5_u, m_qk, m_v, m_o, f_q, f_k, f_v, gates, gate_pre) = _input_projection(
            x, sh1, sc1, norm1_g[l].reshape(1, d), w_cat, b_cat, w_small_t, b_small_t, tm)

        gate_rows = _gate_scan(gate_pre)
        gate_cols = jnp.swapaxes(gate_rows, 1, 2)
        fox_rows = gate_rows[:, 2 * MLSTM_HEADS:2 * MLSTM_HEADS + FOX_HEADS, :]
        fox_rows = fox_rows.reshape(bsz, FOX_PAIRS, 2, seq)
        fox_cols = jnp.swapaxes(fox_rows, 2, 3)

        bmat, cmat, astep, apow = _s5_tables(s5_a_re[l], s5_a_im[l], s5_log_dt[l], s5_b_re[l],
                                             s5_b_im[l], s5_c_re[l], s5_c_im[l])
        y_s5 = _s5(s5_u, bmat, cmat, astep, apow, s5_d[l].reshape(1, -1),
                   s5_w_glu[l].astype(BF16), ts)

        pad_hh = ((0, 0), (0, LANES - MLSTM_HEAD_DIM), (0, LANES - MLSTM_HEAD_DIM))
        y_m = _mlstm(m_qk, m_v, m_o, gate_rows, gate_cols,
                     _pad_heads(mlstm_conv_w[l], MLSTM_HEADS, MLSTM_HEAD_DIM),
                     _pad_heads(mlstm_conv_b[l].reshape(1, -1), MLSTM_HEADS, MLSTM_HEAD_DIM),
                     jnp.pad(mlstm_wq[l], pad_hh).astype(BF16),
                     jnp.pad(mlstm_wk[l], pad_hh).astype(BF16),
                     _pad_heads(mlstm_norm_g[l].reshape(1, -1), MLSTM_HEADS, MLSTM_HEAD_DIM))

        y_f = _fox(f_q, f_k, f_v, fox_cols, fox_rows, tq)

        w_up_m = _pad_heads(w_up_mlstm[l].T, MLSTM_HEADS, MLSTM_HEAD_DIM).T
        x = _merge(x, y_s5, y_m, y_f, gates, g1, w_up_s5[l].astype(BF16), w_up_m.astype(BF16),
                   w_up_fox[l].astype(BF16), w_out[l].astype(BF16), tm)
        x = _ffn(x, sh2, sc2, g2, norm2_g[l].reshape(1, d), ffn_w1[l].astype(BF16),
                 ffn_w3[l].astype(BF16), ffn_w2[l].astype(BF16), final_gain,
                 l == depth - 1, tm)
    return x
```

```python
import functools
import math

import jax
import jax.numpy as jnp
from jax import lax
from jax.experimental import pallas as pl
from jax.experimental.pallas import tpu as pltpu

F32 = jnp.float32
BF16 = jnp.bfloat16

LANES = 128
SUBLANES = 8
VMEM_LIMIT_BYTES = 56 * 1024 * 1024

NORM_EPS = 1e-6
S5_GROUPS = 16
S5_GROUP_DIM = 16
S5_STATE = 64
S5_WIDTH = S5_GROUPS * S5_GROUP_DIM
S5_CHANNELS = S5_GROUPS * S5_STATE
MLSTM_HEADS = 4
MLSTM_HEAD_DIM = 96
MLSTM_WIDTH = MLSTM_HEADS * MLSTM_HEAD_DIM
MLSTM_PAD_WIDTH = MLSTM_HEADS * LANES
MLSTM_CHUNK = 128
CONV_TAPS = 4
FOX_HEADS = 6
FOX_HEAD_DIM = 64
FOX_WIDTH = FOX_HEADS * FOX_HEAD_DIM
FOX_PAIRS = FOX_WIDTH // LANES
GATE_ROWS = 16
NEG_BIG = -1e30
LOG2_E = math.log2(math.e)

NT_DIMS = (((1,), (1,)), ((), ()))


def _dot(a, b):
    return jnp.dot(a, b, preferred_element_type=F32)


def _dot_nt(a, b):
    return lax.dot_general(a, b, NT_DIMS, preferred_element_type=F32)


def _sigmoid(x):
    return 1.0 / (1.0 + jnp.exp(-x))


def _silu(x):
    return x * _sigmoid(x)


def _log_sigmoid(x):
    return jnp.minimum(x, 0.0) - jnp.log1p(jnp.exp(-jnp.abs(x)))


def _rms_modulate(x, gain, scale, shift):
    y = x * lax.rsqrt(jnp.mean(x * x, axis=-1, keepdims=True) + NORM_EPS)
    return (y * gain) * (1.0 + scale) + shift


def _params(semantics):
    return pltpu.CompilerParams(dimension_semantics=semantics, vmem_limit_bytes=VMEM_LIMIT_BYTES)


def _const_spec(shape):
    nd = len(shape)
    return pl.BlockSpec(shape, lambda *_: (0,) * nd, pipeline_mode=pl.Buffered(1))


def _row_tile(seq, want):
    t = min(want, seq)
    assert seq % t == 0 and t % SUBLANES == 0
    return t


def _mod_kernel(c_ref, w_ref, b_ref, o_ref):
    cf = _silu(c_ref[...]).astype(BF16)
    o_ref[0] = _dot(cf, w_ref[0].astype(BF16)) + b_ref[0]


def _modulation(c, mod_w, mod_b):
    depth, d, six_d = mod_w.shape
    bsz = c.shape[0]
    n_col = six_d // d
    return pl.pallas_call(
        _mod_kernel,
        grid=(depth, n_col),
        in_specs=[pl.BlockSpec((bsz, d), lambda l, j: (0, 0)),
                  pl.BlockSpec((1, d, d), lambda l, j: (l, 0, j)),
                  pl.BlockSpec((1, 1, d), lambda l, j: (l, 0, j))],
        out_specs=pl.BlockSpec((1, bsz, d), lambda l, j: (l, 0, j)),
        out_shape=jax.ShapeDtypeStruct((depth, bsz, six_d), F32),
        compiler_params=_params(("parallel", "parallel")),
        name="adaln_modulation",
    )(c, mod_w, mod_b.reshape(depth, 1, six_d))


IN_SEGMENTS = ((S5_WIDTH, F32), (MLSTM_PAD_WIDTH, F32), (MLSTM_PAD_WIDTH, F32),
               (MLSTM_PAD_WIDTH, F32), (FOX_WIDTH, F32), (FOX_WIDTH, BF16))
IN_T_SEGMENTS = ((GATE_ROWS, F32), (FOX_WIDTH, BF16))
DOT_COLS = 512


def _inproj_kernel(x_ref, sh_ref, sc_ref, g_ref, w_ref, b_ref, wt_ref, bt_ref, *out_refs):
    h = _rms_modulate(x_ref[0], g_ref[...], sc_ref[0], sh_ref[0]).astype(BF16)
    n_t = len(IN_T_SEGMENTS)
    start = 0
    for o_ref in out_refs[:-n_t]:
        width = o_ref.shape[-1]
        for c0 in range(0, width, DOT_COLS):
            cw = min(DOT_COLS, width - c0)
            z = _dot(h, w_ref[:, start + c0:start + c0 + cw]) + b_ref[:, start + c0:start + c0 + cw]
            o_ref[0, :, c0:c0 + cw] = z.astype(o_ref.dtype)
        start += width
    start = 0
    for o_ref in out_refs[-n_t:]:
        rows = o_ref.shape[1]
        z = _dot_nt(wt_ref[start:start + rows, :], h) + bt_ref[start:start + rows, :]
        o_ref[0] = z.astype(o_ref.dtype)
        start += rows


def _input_projection(x, shift, scale, gain, w_cat, b_cat, w_t, b_t, tm):
    bsz, seq, d = x.shape
    segs = IN_SEGMENTS + ((w_cat.shape[1] - sum(w for w, _ in IN_SEGMENTS), F32),)
    tok = lambda b, i: (b, i, 0)
    vec = lambda b, i: (b, 0, 0)
    out_shape = [jax.ShapeDtypeStruct((bsz, seq, w), dt) for w, dt in segs]
    out_shape += [jax.ShapeDtypeStruct((bsz, r, seq), dt) for r, dt in IN_T_SEGMENTS]
    out_specs = [pl.BlockSpec((1, tm, w), tok) for w, _ in segs]
    out_specs += [pl.BlockSpec((1, r, tm), lambda b, i: (b, 0, i)) for r, _ in IN_T_SEGMENTS]
    return pl.pallas_call(
        _inproj_kernel,
        grid=(bsz, seq // tm),
        in_specs=[pl.BlockSpec((1, tm, d), tok),
                  pl.BlockSpec((1, 1, d), vec), pl.BlockSpec((1, 1, d), vec),
                  _const_spec(gain.shape), _const_spec(w_cat.shape), _const_spec(b_cat.shape),
                  _const_spec(w_t.shape), _const_spec(b_t.shape)],
        out_specs=out_specs,
        out_shape=out_shape,
        compiler_params=_params(("parallel", "parallel")),
        name="norm_input_projection",
    )(x, shift, scale, gain, w_cat, b_cat, w_t, b_t)


def _bf16_round(x):
    return x.astype(BF16).astype(F32)


def _gate_scan_kernel(x_ref, o_ref, split_ref):
    x = x_ref[0]
    seq = x.shape[1]
    seg = MLSTM_CHUNK
    lane = lax.broadcasted_iota(jnp.int32, x.shape, 1) & (seg - 1)
    v = _log_sigmoid(x)
    k = 1
    while k < seg:
        v = v + jnp.where(lane >= k, pltpu.roll(v, k, 1), 0.0)
        k *= 2
    row = lax.broadcasted_iota(jnp.int32, (GATE_ROWS, seg), 0)
    carry = jnp.zeros((GATE_ROWS, 1), F32)
    for j in range(seq // seg):
        sl = slice(j * seg, (j + 1) * seg)
        local = v[:, sl]
        total = local + carry
        o_ref[0, :, sl] = jnp.where(row < MLSTM_HEADS, x[:, sl],
                                    jnp.where(row < 2 * MLSTM_HEADS, local, total))
        scaled = total * LOG2_E
        hi = _bf16_round(scaled)
        mid = _bf16_round(scaled - hi)
        split_ref[0, 0, :, sl] = hi
        split_ref[0, 1, :, sl] = mid
        split_ref[0, 2, :, sl] = _bf16_round((scaled - hi) - mid)
        carry = carry + local[:, seg - 1:seg]


def _gate_scan(pre):
    bsz, rows, seq = pre.shape
    spec = pl.BlockSpec((1, rows, seq), lambda b: (b, 0, 0))
    return pl.pallas_call(
        _gate_scan_kernel, grid=(bsz,), in_specs=[spec],
        out_specs=[spec, pl.BlockSpec((1, 3, rows, seq), lambda b: (b, 0, 0, 0))],
        out_shape=[jax.ShapeDtypeStruct(pre.shape, F32),
                   jax.ShapeDtypeStruct((bsz, 3, rows, seq), F32)],
        compiler_params=_params(("parallel",)),
        name="gate_scan",
    )(pre)


SCAN_SHIFTS = (1, 2, 4)


def _s5_kernel(u_ref, bmat_ref, cmat_ref, astep_ref, apow_ref, d_ref, wglu_ref, o_ref,
               st_ref, carry_ref):
    nch = S5_CHANNELS

    @pl.when(pl.program_id(1) == 0)
    def _():
        carry_ref[...] = jnp.zeros_like(carry_ref)

    u = u_ref[0]
    st_ref[...] = _dot(u.astype(BF16), bmat_ref[...])
    n_blocks = u.shape[0] // SUBLANES

    def body(i, carry):
        cr, ci = carry
        r0 = pl.multiple_of(i * SUBLANES, SUBLANES)
        xr = st_ref[pl.ds(r0, SUBLANES), :nch]
        xi = st_ref[pl.ds(r0, SUBLANES), nch:]
        for k, shift in enumerate(SCAN_SHIFTS):
            ar, ai = astep_ref[k, :, :nch], astep_ref[k, :, nch:]
            sr, si = pltpu.roll(xr, shift, 0), pltpu.roll(xi, shift, 0)
            xr, xi = xr + (ar * sr - ai * si), xi + (ar * si + ai * sr)
        pr, pi = apow_ref[:, :nch], apow_ref[:, nch:]
        xr, xi = xr + (pr * cr - pi * ci), xi + (pr * ci + pi * cr)
        st_ref[pl.ds(r0, SUBLANES), :nch] = xr
        st_ref[pl.ds(r0, SUBLANES), nch:] = xi
        last = SUBLANES - 1
        return (jnp.broadcast_to(xr[last:, :], xr.shape), jnp.broadcast_to(xi[last:, :], xi.shape))

    cr, ci = lax.fori_loop(0, n_blocks, body, (carry_ref[:, :nch], carry_ref[:, nch:]))
    carry_ref[:, :nch] = cr
    carry_ref[:, nch:] = ci

    y = _dot(st_ref[...].astype(BF16), cmat_ref[...]) + d_ref[...] * u
    y = y * (0.5 * (1.0 + jnp.tanh(math.sqrt(2.0 / math.pi) * (y + 0.044715 * (y * y * y)))))
    o_ref[0] = y * _sigmoid(_dot(y.astype(BF16), wglu_ref[...]))


def _s5(u, bmat, cmat, astep, apow, d_skip, w_glu, tb):
    bsz, seq, width = u.shape
    tok = lambda b, i: (b, i, 0)
    return pl.pallas_call(
        _s5_kernel,
        grid=(bsz, seq // tb),
        in_specs=[pl.BlockSpec((1, tb, width), tok)] + [
            _const_spec(a.shape) for a in (bmat, cmat, astep, apow, d_skip, w_glu)],
        out_specs=pl.BlockSpec((1, tb, width), tok),
        out_shape=jax.ShapeDtypeStruct(u.shape, F32),
        scratch_shapes=[pltpu.VMEM((tb, 2 * S5_CHANNELS), F32),
                        pltpu.VMEM((SUBLANES, 2 * S5_CHANNELS), F32)],
        compiler_params=_params(("parallel", "arbitrary")),
        name="s5_scan",
    )(u, bmat, cmat, astep, apow, d_skip, w_glu)


def _s5_tables(a_re, a_im, log_dt, b_re, b_im, c_re, c_im):
    g, n, p = b_re.shape
    dt = jnp.exp(log_dt)[:, None]
    mag = jnp.exp(dt * a_re)
    ang = dt * a_im
    abar_r, abar_i = mag * jnp.cos(ang), mag * jnp.sin(ang)
    den = a_re * a_re + a_im * a_im
    pr, qi = abar_r - 1.0, abar_i
    coef_r = (pr * a_re + qi * a_im) / den
    coef_i = (qi * a_re - pr * a_im) / den
    bbar_r = coef_r[..., None] * b_re - coef_i[..., None] * b_im
    bbar_i = coef_r[..., None] * b_im + coef_i[..., None] * b_re
    eye = jnp.eye(g, dtype=F32)

    def in_block(bb):
        return jnp.einsum('gnp,gh->gphn', bb, eye).reshape(g * p, g * n)

    def out_block(cc):
        return jnp.einsum('gpn,gh->gnhp', cc, eye).reshape(g * n, g * p)

    bmat = jnp.concatenate([in_block(bbar_r), in_block(bbar_i)], axis=1).astype(BF16)
    cmat = jnp.concatenate([out_block(c_re), -out_block(c_im)], axis=0).astype(BF16)

    def power(e):
        e = jnp.asarray(e, F32)[:, None, None]
        m = jnp.exp(e * (dt * a_re)[None])
        th = e * ang[None]
        return (m * jnp.cos(th)).reshape(-1, g * n), (m * jnp.sin(th)).reshape(-1, g * n)

    rows = jnp.arange(SUBLANES)
    sr, si = power(SCAN_SHIFTS)
    keep = (rows[None, :] >= jnp.asarray(SCAN_SHIFTS)[:, None]).astype(F32)[:, :, None]
    astep = jnp.concatenate([keep * sr[:, None, :], keep * si[:, None, :]], axis=-1)
    wr, wi = power(rows + 1)
    apow = jnp.concatenate([wr, wi], axis=-1)
    return bmat, cmat, astep, apow


def _mlstm_kernel(qk_ref, v_ref, o_ref, grow_ref, gcol_ref, cw_ref, cb_ref, wq_ref, wk_ref,
                  ng_ref, out_ref, tail_ref, ckv_ref, n_ref, m_ref):
    chunk = qk_ref.shape[1]
    nh = MLSTM_HEADS

    @pl.when(pl.program_id(1) == 0)
    def _():
        tail_ref[...] = jnp.zeros_like(tail_ref)
        ckv_ref[...] = jnp.zeros_like(ckv_ref)
        n_ref[...] = jnp.zeros_like(n_ref)
        m_ref[...] = jnp.zeros_like(m_ref)

    x = qk_ref[0]
    ext = jnp.concatenate([tail_ref[...], x], axis=0)
    conv = cb_ref[...] + cw_ref[CONV_TAPS - 1:CONV_TAPS, :] * x
    for tap in range(CONV_TAPS - 1):
        back = CONV_TAPS - 1 - tap
        conv = conv + cw_ref[tap:tap + 1, :] * ext[SUBLANES - back:SUBLANES - back + chunk, :]
    tail_ref[...] = x[chunk - SUBLANES:, :]
    cx = _silu(conv)

    row_id = lax.broadcasted_iota(jnp.int32, (chunk, chunk), 0)
    col_id = lax.broadcasted_iota(jnp.int32, (chunk, chunk), 1)
    causal = row_id >= col_id
    k_scale = MLSTM_HEAD_DIM ** -0.5
    for hd in range(nh):
        cols = slice(hd * LANES, (hd + 1) * LANES)
        cxh = cx[:, cols].astype(BF16)
        q = _dot(cxh, wq_ref[hd])
        k = _dot(cxh, wk_ref[hd]) * k_scale
        v = v_ref[0, :, cols]
        qb, kb = q.astype(BF16), k.astype(BF16)
        li_row = grow_ref[0, hd:hd + 1, :]
        b_row = grow_ref[0, nh + hd:nh + hd + 1, :]
        li_col = gcol_ref[0, :, hd:hd + 1]
        b_col = gcol_ref[0, :, nh + hd:nh + hd + 1]
        m_st = m_ref[hd:hd + 1, 0:1]
        n_st = n_ref[hd:hd + 1, :]

        dmat = jnp.where(causal, b_col - b_row + li_row, -jnp.inf)
        m_inter = b_col + m_st
        m_t = jnp.maximum(m_inter, jnp.max(dmat, axis=1, keepdims=True))
        s = _dot_nt(qb, kb) * jnp.exp(dmat - m_t)
        inter = jnp.exp(m_inter - m_t)
        num = _dot(s.astype(BF16), v.astype(BF16)) + inter * _dot(qb, ckv_ref[hd].astype(BF16))
        den = jnp.sum(s, axis=1, keepdims=True) + inter * jnp.sum(q * n_st, axis=1, keepdims=True)
        h = num / jnp.maximum(jnp.abs(den), jnp.exp(-m_t))

        b_last = b_row[:, chunk - 1:chunk]
        w_row = b_last - b_row + li_row
        w_col = b_last - b_col + li_col
        m_new = jnp.maximum(b_last + m_st, jnp.max(w_row, axis=1, keepdims=True))
        decay = jnp.exp(b_last + m_st - m_new)
        ws_col = jnp.exp(w_col - m_new)
        ckv_ref[hd] = decay * ckv_ref[hd] + _dot(k.T.astype(BF16), (v * ws_col).astype(BF16))
        n_ref[hd:hd + 1, :] = decay * n_st + jnp.sum(ws_col * k, axis=0, keepdims=True)
        m_ref[hd:hd + 1, :] = jnp.broadcast_to(m_new, (1, LANES))

        hn = h * lax.rsqrt(jnp.sum(h * h, axis=1, keepdims=True) * (1.0 / MLSTM_HEAD_DIM) + NORM_EPS)
        out_ref[0, :, cols] = _sigmoid(o_ref[0, :, cols]) * (hn * ng_ref[:, cols])


def _mlstm(qk, v, o, gate_rows, gate_cols, conv_w, conv_b, wq, wk, norm_g):
    bsz, seq, width = qk.shape
    chunk = MLSTM_CHUNK
    tok = lambda b, i: (b, i, 0)
    tok_spec = pl.BlockSpec((1, chunk, width), tok)
    return pl.pallas_call(
        _mlstm_kernel,
        grid=(bsz, seq // chunk),
        in_specs=[tok_spec, tok_spec, tok_spec,
                  pl.BlockSpec((1, GATE_ROWS, chunk), lambda b, i: (b, 0, i)),
                  pl.BlockSpec((1, chunk, GATE_ROWS), tok)] + [
            _const_spec(a.shape) for a in (conv_w, conv_b, wq, wk, norm_g)],
        out_specs=tok_spec,
        out_shape=jax.ShapeDtypeStruct(qk.shape, F32),
        scratch_shapes=[pltpu.VMEM((SUBLANES, width), F32),
                        pltpu.VMEM((MLSTM_HEADS, LANES, LANES), F32),
                        pltpu.VMEM((SUBLANES, LANES), F32),
                        pltpu.VMEM((SUBLANES, LANES), F32)],
        compiler_params=_params(("parallel", "arbitrary")),
        name="mlstm_chunkwise",
    )(qk, v, o, gate_rows, gate_cols, conv_w, conv_b, wq, wk, norm_g)


FOX_BIAS_LANES = 6
FOX_QUERY_SPLIT = 2
FOX_VT_ROWS = FOX_HEAD_DIM + 16


def _fox_kernel(qi_ref, ki_ref, q_ref, qb_ref, k_ref, kb_ref, vt_ref, o_ref,
                qx_ref, m_ref, acc_ref):
    step = pl.program_id(2)
    qi, ki = qi_ref[step], ki_ref[step]
    tq, tk = q_ref.shape[1], k_ref.shape[1]
    hd = FOX_HEAD_DIM

    @pl.when(ki == 0)
    def _():
        lane = lax.broadcasted_iota(jnp.int32, (1, LANES), 1)
        q = q_ref[0] * (hd ** -0.5 * LOG2_E)
        qb = qb_ref[0]
        for hh in range(2):
            in_head = (lane >= hh * hd) & (lane < (hh + 1) * hd)
            in_bias = (lane >= hh * FOX_BIAS_LANES) & (lane < (hh + 1) * FOX_BIAS_LANES)
            qx_ref[hh, :, :LANES] = jnp.where(in_head, q, 0.0).astype(BF16)
            qx_ref[hh, :, LANES:] = jnp.where(in_bias, qb, jnp.zeros_like(qb))
        m_ref[...] = jnp.full_like(m_ref, NEG_BIG)
        acc_ref[...] = jnp.zeros_like(acc_ref)

    def attend(diagonal):
        kx = jnp.concatenate([k_ref[0], kb_ref[0]], axis=1)
        tc = tq // FOX_QUERY_SPLIT
        chains = [(hh, c * tc) for hh in range(2) for c in range(FOX_QUERY_SPLIT)]
        scores, probs = {}, {}

        def score(hh, c0):
            s = _dot_nt(kx, qx_ref[hh, c0:c0 + tc, :])
            if diagonal:
                visible = (lax.broadcasted_iota(jnp.int32, (tk, tc), 0)
                           <= lax.broadcasted_iota(jnp.int32, (tk, tc), 1) + c0)
                s = jnp.where(visible, s, NEG_BIG)
            scores[hh, c0] = s

        def softmax(hh, c0):
            s = scores.pop((hh, c0))
            m_prev = m_ref[hh, :, c0:c0 + tc]
            m_new = jnp.maximum(m_prev, jnp.max(s, axis=0, keepdims=True))
            m_ref[hh, :, c0:c0 + tc] = m_new
            probs[hh, c0] = (jnp.exp2(m_prev - m_new), jnp.exp2(s - m_new).astype(BF16))

        def value(hh, c0):
            alpha, p = probs.pop((hh, c0))
            acc_ref[hh, :, c0:c0 + tc] = (alpha * acc_ref[hh, :, c0:c0 + tc]
                                          + _dot(vt_ref[0, 0, hh * FOX_VT_ROWS:(hh + 1) * FOX_VT_ROWS, :], p))

        n = len(chains)
        for i in range(n + 2):
            if i < n:
                score(*chains[i])
            if 1 <= i <= n:
                softmax(*chains[i - 1])
            if i >= 2:
                value(*chains[i - 2])

    @pl.when(ki < qi)
    def _():
        attend(False)

    @pl.when(ki == qi)
    def _():
        attend(True)
        out_t = jnp.concatenate([acc_ref[hh, :hd, :] / acc_ref[hh, hd:hd + 1, :] for hh in range(2)],
                                axis=0)
        o_ref[0] = out_t.T


def _fox(q, qb, k, kb, v_t, tq):
    bsz, seq, width = q.shape
    nq = seq // tq
    q_idx = [i for i in range(nq) for _ in range(i + 1)]
    k_idx = [j for i in range(nq) for j in range(i + 1)]
    q_tbl, k_tbl = jnp.asarray(q_idx, jnp.int32), jnp.asarray(k_idx, jnp.int32)
    q_spec = pl.BlockSpec((1, tq, LANES), lambda b, p, s, qt, kt: (b, qt[s], p))
    k_spec = pl.BlockSpec((1, tq, LANES), lambda b, p, s, qt, kt: (b, kt[s], p))
    grid_spec = pltpu.PrefetchScalarGridSpec(
        num_scalar_prefetch=2,
        grid=(bsz, FOX_PAIRS, len(q_idx)),
        in_specs=[q_spec, q_spec, k_spec, k_spec,
                  pl.BlockSpec((1, 1, 2 * FOX_VT_ROWS, tq), lambda b, p, s, qt, kt: (b, p, 0, kt[s]))],
        out_specs=q_spec,
        scratch_shapes=[pltpu.VMEM((2, tq, 2 * LANES), BF16),
                        pltpu.VMEM((2, 1, tq), F32),
                        pltpu.VMEM((2, FOX_VT_ROWS, tq), F32)],
    )
    v_heads = v_t.reshape(bsz, FOX_HEADS, FOX_HEAD_DIM, seq)
    ones = jnp.ones((bsz, FOX_HEADS, FOX_VT_ROWS - FOX_HEAD_DIM, seq), v_t.dtype)
    v_ext = jnp.concatenate([v_heads, ones], axis=2).reshape(bsz, FOX_PAIRS, 2 * FOX_VT_ROWS, seq)
    return pl.pallas_call(
        _fox_kernel, grid_spec=grid_spec,
        out_shape=jax.ShapeDtypeStruct((bsz, seq, width), F32),
        compiler_params=_params(("parallel", "parallel", "arbitrary")),
        name="forgetting_attention",
    )(q_tbl, k_tbl, q, qb, k, kb, v_ext)


def _fox_bias_lanes(split):
    bsz, _, _, seq = split.shape
    lo = 2 * MLSTM_HEADS
    pieces = jnp.transpose(split[:, :, lo:lo + FOX_HEADS, :], (0, 3, 2, 1))
    ones = jnp.ones_like(pieces)

    def lanes(per_head):
        per_pair = per_head.reshape(bsz, seq, FOX_PAIRS, 2 * FOX_BIAS_LANES)
        per_pair = jnp.pad(per_pair, ((0, 0), (0, 0), (0, 0), (0, LANES - 2 * FOX_BIAS_LANES)))
        return per_pair.reshape(bsz, seq, FOX_PAIRS * LANES).astype(BF16)

    return (lanes(jnp.concatenate([pieces, ones], axis=-1)),
            lanes(jnp.concatenate([ones, -pieces], axis=-1)))


def _merge_kernel(x_ref, ys_ref, ym_ref, yf_ref, gate_ref, g1_ref, ws_ref, wm_ref, wf_ref,
                  wo_ref, o_ref):
    d = x_ref.shape[-1]
    merged = (_sigmoid(gate_ref[0, :, :d]) * _dot(ys_ref[0].astype(BF16), ws_ref[...])
              + _sigmoid(gate_ref[0, :, d:2 * d]) * _dot(ym_ref[0].astype(BF16), wm_ref[...])
              + _sigmoid(gate_ref[0, :, 2 * d:]) * _dot(yf_ref[0].astype(BF16), wf_ref[...]))
    o_ref[0] = x_ref[0] + g1_ref[0] * _dot(merged.astype(BF16), wo_ref[...])


def _merge(x, y_s5, y_m, y_f, gates, g1, w_up_s5, w_up_m, w_up_f, w_out, tm):
    bsz, seq, d = x.shape
    tok = lambda b, i: (b, i, 0)
    acts = (x, y_s5, y_m, y_f, gates)
    weights = (w_up_s5, w_up_m, w_up_f, w_out)
    return pl.pallas_call(
        _merge_kernel,
        grid=(bsz, seq // tm),
        in_specs=[pl.BlockSpec((1, tm, a.shape[-1]), tok) for a in acts]
        + [pl.BlockSpec((1, 1, d), lambda b, i: (b, 0, 0))]
        + [_const_spec(w.shape) for w in weights],
        out_specs=pl.BlockSpec((1, tm, d), tok),
        out_shape=jax.ShapeDtypeStruct(x.shape, F32),
        compiler_params=_params(("parallel", "parallel")),
        name="merge_output_projection",
    )(*acts, g1, *weights)


FFN_COLS = 256


def _ffn_kernel(x_ref, sh_ref, sc_ref, g2_ref, gain_ref, w1_ref, w3_ref, w2_ref, fg_ref, o_ref,
                *, final_norm):
    x = x_ref[0]
    h = _rms_modulate(x, gain_ref[...], sc_ref[0], sh_ref[0]).astype(BF16)
    hidden = w1_ref.shape[1]
    acc = jnp.zeros(x.shape, F32)
    for c0 in range(0, hidden, FFN_COLS):
        a = _dot(h, w1_ref[:, c0:c0 + FFN_COLS])
        b = _dot(h, w3_ref[:, c0:c0 + FFN_COLS])
        acc = acc + _dot((_silu(a) * b).astype(BF16), w2_ref[c0:c0 + FFN_COLS, :])
    y = x + g2_ref[0] * acc
    if final_norm:
        y = (y * lax.rsqrt(jnp.mean(y * y, axis=-1, keepdims=True) + NORM_EPS)) * fg_ref[...]
    o_ref[0] = y


def _ffn(x, shift, scale, g2, gain, w1, w3, w2, final_g, final_norm, tm):
    bsz, seq, d = x.shape
    tok = lambda b, i: (b, i, 0)
    vec = pl.BlockSpec((1, 1, d), lambda b, i: (b, 0, 0))
    return pl.pallas_call(
        functools.partial(_ffn_kernel, final_norm=final_norm),
        grid=(bsz, seq // tm),
        in_specs=[pl.BlockSpec((1, tm, d), tok), vec, vec, vec]
        + [_const_spec(a.shape) for a in (gain, w1, w3, w2, final_g)],
        out_specs=pl.BlockSpec((1, tm, d), tok),
        out_shape=jax.ShapeDtypeStruct(x.shape, F32),
        compiler_params=_params(("parallel", "parallel")),
        name="norm_swiglu",
    )(x, shift, scale, g2, gain, w1, w3, w2, final_g)


def _pad_heads(w, heads, head_dim):
    lead = w.shape[:-1]
    w = w.reshape(lead + (heads, head_dim))
    w = jnp.pad(w, [(0, 0)] * len(lead) + [(0, 0), (0, LANES - head_dim)])
    return w.reshape(lead + (heads * LANES,))


def _layer_weights(w_in, b_in):
    sizes = (S5_WIDTH, MLSTM_WIDTH, MLSTM_WIDTH, MLSTM_WIDTH, MLSTM_HEADS, MLSTM_HEADS,
             FOX_WIDTH, FOX_WIDTH, FOX_WIDTH, FOX_HEADS)
    offs = [0]
    for n in sizes:
        offs.append(offs[-1] + n)
    both = jnp.concatenate([w_in, b_in[None, :]], axis=0)
    part = [both[:, offs[i]:offs[i + 1]] for i in range(len(sizes))] + [both[:, offs[-1]:]]
    pad_m = lambda w: _pad_heads(w, MLSTM_HEADS, MLSTM_HEAD_DIM)
    cat = jnp.concatenate([part[0], pad_m(part[1]), pad_m(part[2]), pad_m(part[3]),
                           part[6], part[7], part[10]], axis=1)
    small = jnp.concatenate([part[4], part[5], part[9]], axis=1)
    small = jnp.pad(small, ((0, 0), (0, GATE_ROWS - small.shape[1])))
    cat_t = jnp.concatenate([small, part[8]], axis=1).T
    return (cat[:-1].astype(BF16), cat[-1:], cat_t[:, :-1].astype(BF16), cat_t[:, -1:])


def kernel(x, c, mod_w, mod_b, norm1_g, norm2_g, w_in, b_in, s5_a_re, s5_a_im, s5_log_dt, s5_b_re, s5_b_im, s5_c_re, s5_c_im, s5_d, s5_w_glu, mlstm_conv_w, mlstm_conv_b, mlstm_wq, mlstm_wk, mlstm_norm_g, w_up_s5, w_up_mlstm, w_up_fox, w_out, ffn_w1, ffn_w3, ffn_w2, final_g):
    bsz, seq, d = x.shape
    depth = mod_w.shape[0]
    assert seq % MLSTM_CHUNK == 0
    tm = _row_tile(seq, 256)
    tq = _row_tile(seq, 512)
    ts = _row_tile(seq, 512)

    mod = _modulation(c, mod_w, mod_b)
    final_gain = final_g.reshape(1, d)
    for l in range(depth):
        sh1, sc1, g1, sh2, sc2, g2 = [mod[l, :, i * d:(i + 1) * d].reshape(bsz, 1, d)
                                      for i in range(6)]
        w_cat, b_cat, w_t, b_t = _layer_weights(w_in[l], b_in[l])
        (s5_u, m_qk, m_v, m_o, f_q, f_k, gates, gate_pre, f_vt) = _input_projection(
            x, sh1, sc1, norm1_g[l].reshape(1, d), w_cat, b_cat, w_t, b_t, tm)

        gate_rows, gate_split = _gate_scan(gate_pre)
        gate_cols = jnp.swapaxes(gate_rows, 1, 2)
        f_qb, f_kb = _fox_bias_lanes(gate_split)

        bmat, cmat, astep, apow = _s5_tables(s5_a_re[l], s5_a_im[l], s5_log_dt[l], s5_b_re[l],
                                             s5_b_im[l], s5_c_re[l], s5_c_im[l])
        y_s5 = _s5(s5_u, bmat, cmat, astep, apow, s5_d[l].reshape(1, -1),
                   s5_w_glu[l].astype(BF16), ts)

        pad_hh = ((0, 0), (0, LANES - MLSTM_HEAD_DIM), (0, LANES - MLSTM_HEAD_DIM))
        y_m = _mlstm(m_qk, m_v, m_o, gate_rows, gate_cols,
                     _pad_heads(mlstm_conv_w[l], MLSTM_HEADS, MLSTM_HEAD_DIM),
                     _pad_heads(mlstm_conv_b[l].reshape(1, -1), MLSTM_HEADS, MLSTM_HEAD_DIM),
                     jnp.pad(mlstm_wq[l], pad_hh).astype(BF16),
                     jnp.pad(mlstm_wk[l], pad_hh).astype(BF16),
                     _pad_heads(mlstm_norm_g[l].reshape(1, -1), MLSTM_HEADS, MLSTM_HEAD_DIM))

        y_f = _fox(f_q, f_qb, f_k, f_kb, f_vt, tq)

        w_up_m = _pad_heads(w_up_mlstm[l].T, MLSTM_HEADS, MLSTM_HEAD_DIM).T
        x = _merge(x, y_s5, y_m, y_f, gates, g1, w_up_s5[l].astype(BF16), w_up_m.astype(BF16),
                   w_up_fox[l].astype(BF16), w_out[l].astype(BF16), tm)
        x = _ffn(x, sh2, sc2, g2, norm2_g[l].reshape(1, d), ffn_w1[l].astype(BF16),
                 ffn_w3[l].astype(BF16), ffn_w2[l].astype(BF16), final_gain,
                 l == depth - 1, tm)
    return x
```

```python
import functools
import math

import jax
import jax.numpy as jnp
from jax import lax
from jax.experimental import pallas as pl
from jax.experimental.pallas import tpu as pltpu

F32 = jnp.float32
BF16 = jnp.bfloat16

LANES = 128
SUBLANES = 8
VMEM_LIMIT_BYTES = 56 * 1024 * 1024

NORM_EPS = 1e-6
S5_GROUPS = 16
S5_GROUP_DIM = 16
S5_STATE = 64
S5_WIDTH = S5_GROUPS * S5_GROUP_DIM
S5_CHANNELS = S5_GROUPS * S5_STATE
MLSTM_HEADS = 4
MLSTM_HEAD_DIM = 96
MLSTM_WIDTH = MLSTM_HEADS * MLSTM_HEAD_DIM
MLSTM_PAD_WIDTH = MLSTM_HEADS * LANES
MLSTM_CHUNK = 128
CONV_TAPS = 4
FOX_HEADS = 6
FOX_HEAD_DIM = 64
FOX_WIDTH = FOX_HEADS * FOX_HEAD_DIM
FOX_PAIRS = FOX_WIDTH // LANES
GATE_ROWS = 16
NEG_BIG = -1e30
LOG2_E = math.log2(math.e)

NT_DIMS = (((1,), (1,)), ((), ()))


def _dot(a, b):
    return jnp.dot(a, b, preferred_element_type=F32)


def _dot_nt(a, b):
    return lax.dot_general(a, b, NT_DIMS, preferred_element_type=F32)


def _sigmoid(x):
    return 1.0 / (1.0 + jnp.exp(-x))


def _silu(x):
    return x * _sigmoid(x)


def _log_sigmoid(x):
    return jnp.minimum(x, 0.0) - jnp.log1p(jnp.exp(-jnp.abs(x)))


def _rms_modulate(x, gain, scale, shift):
    y = x * lax.rsqrt(jnp.mean(x * x, axis=-1, keepdims=True) + NORM_EPS)
    return (y * gain) * (1.0 + scale) + shift


def _params(semantics):
    return pltpu.CompilerParams(dimension_semantics=semantics, vmem_limit_bytes=VMEM_LIMIT_BYTES)


def _const_spec(shape):
    nd = len(shape)
    return pl.BlockSpec(shape, lambda *_: (0,) * nd, pipeline_mode=pl.Buffered(1))


def _row_tile(seq, want):
    t = min(want, seq)
    assert seq % t == 0 and t % SUBLANES == 0
    return t


def _mod_kernel(c_ref, w_ref, b_ref, o_ref):
    cf = _silu(c_ref[...]).astype(BF16)
    o_ref[0] = _dot(cf, w_ref[0].astype(BF16)) + b_ref[0]


def _modulation(c, mod_w, mod_b):
    depth, d, six_d = mod_w.shape
    bsz = c.shape[0]
    n_col = six_d // d
    return pl.pallas_call(
        _mod_kernel,
        grid=(depth, n_col),
        in_specs=[pl.BlockSpec((bsz, d), lambda l, j: (0, 0)),
                  pl.BlockSpec((1, d, d), lambda l, j: (l, 0, j)),
                  pl.BlockSpec((1, 1, d), lambda l, j: (l, 0, j))],
        out_specs=pl.BlockSpec((1, bsz, d), lambda l, j: (l, 0, j)),
        out_shape=jax.ShapeDtypeStruct((depth, bsz, six_d), F32),
        compiler_params=_params(("parallel", "parallel")),
        name="adaln_modulation",
    )(c, mod_w, mod_b.reshape(depth, 1, six_d))


IN_SEGMENTS = ((S5_WIDTH, F32), (MLSTM_PAD_WIDTH, F32), (MLSTM_PAD_WIDTH, F32),
               (MLSTM_PAD_WIDTH, F32), (FOX_WIDTH, F32), (FOX_WIDTH, BF16))
IN_T_SEGMENTS = ((GATE_ROWS, F32), (FOX_WIDTH, BF16))
DOT_COLS = 512


def _inproj_kernel(x_ref, sh_ref, sc_ref, g_ref, w_ref, b_ref, wt_ref, bt_ref, *out_refs):
    h = _rms_modulate(x_ref[0], g_ref[...], sc_ref[0], sh_ref[0]).astype(BF16)
    n_t = len(IN_T_SEGMENTS)
    start = 0
    for o_ref in out_refs[:-n_t]:
        width = o_ref.shape[-1]
        for c0 in range(0, width, DOT_COLS):
            cw = min(DOT_COLS, width - c0)
            z = _dot(h, w_ref[:, start + c0:start + c0 + cw]) + b_ref[:, start + c0:start + c0 + cw]
            o_ref[0, :, c0:c0 + cw] = z.astype(o_ref.dtype)
        start += width
    start = 0
    for o_ref in out_refs[-n_t:]:
        rows = o_ref.shape[1]
        z = _dot_nt(wt_ref[start:start + rows, :], h) + bt_ref[start:start + rows, :]
        o_ref[0] = z.astype(o_ref.dtype)
        start += rows


def _input_projection(x, shift, scale, gain, w_cat, b_cat, w_t, b_t, tm):
    bsz, seq, d = x.shape
    segs = IN_SEGMENTS + ((w_cat.shape[1] - sum(w for w, _ in IN_SEGMENTS), F32),)
    tok = lambda b, i: (b, i, 0)
    vec = lambda b, i: (b, 0, 0)
    out_shape = [jax.ShapeDtypeStruct((bsz, seq, w), dt) for w, dt in segs]
    out_shape += [jax.ShapeDtypeStruct((bsz, r, seq), dt) for r, dt in IN_T_SEGMENTS]
    out_specs = [pl.BlockSpec((1, tm, w), tok) for w, _ in segs]
    out_specs += [pl.BlockSpec((1, r, tm), lambda b, i: (b, 0, i)) for r, _ in IN_T_SEGMENTS]
    return pl.pallas_call(
        _inproj_kernel,
        grid=(bsz, seq // tm),
        in_specs=[pl.BlockSpec((1, tm, d), tok),
                  pl.BlockSpec((1, 1, d), vec), pl.BlockSpec((1, 1, d), vec),
                  _const_spec(gain.shape), _const_spec(w_cat.shape), _const_spec(b_cat.shape),
                  _const_spec(w_t.shape), _const_spec(b_t.shape)],
        out_specs=out_specs,
        out_shape=out_shape,
        compiler_params=_params(("parallel", "parallel")),
        name="norm_input_projection",
    )(x, shift, scale, gain, w_cat, b_cat, w_t, b_t)


def _bf16_round(x):
    return x.astype(BF16).astype(F32)


def _gate_scan_kernel(x_ref, o_ref, split_ref):
    x = x_ref[0]
    seq = x.shape[1]
    seg = MLSTM_CHUNK
    lane = lax.broadcasted_iota(jnp.int32, x.shape, 1) & (seg - 1)
    v = _log_sigmoid(x)
    k = 1
    while k < seg:
        v = v + jnp.where(lane >= k, pltpu.roll(v, k, 1), 0.0)
        k *= 2
    row = lax.broadcasted_iota(jnp.int32, (GATE_ROWS, seg), 0)
    carry = jnp.zeros((GATE_ROWS, 1), F32)
    for j in range(seq // seg):
        sl = slice(j * seg, (j + 1) * seg)
        local = v[:, sl]
        total = local + carry
        o_ref[0, :, sl] = jnp.where(row < MLSTM_HEADS, x[:, sl],
                                    jnp.where(row < 2 * MLSTM_HEADS, local, total))
        scaled = total * LOG2_E
        hi = _bf16_round(scaled)
        mid = _bf16_round(scaled - hi)
        split_ref[0, 0, :, sl] = hi
        split_ref[0, 1, :, sl] = mid
        split_ref[0, 2, :, sl] = _bf16_round((scaled - hi) - mid)
        carry = carry + local[:, seg - 1:seg]


def _gate_scan(pre):
    bsz, rows, seq = pre.shape
    spec = pl.BlockSpec((1, rows, seq), lambda b: (b, 0, 0))
    return pl.pallas_call(
        _gate_scan_kernel, grid=(bsz,), in_specs=[spec],
        out_specs=[spec, pl.BlockSpec((1, 3, rows, seq), lambda b: (b, 0, 0, 0))],
        out_shape=[jax.ShapeDtypeStruct(pre.shape, F32),
                   jax.ShapeDtypeStruct((bsz, 3, rows, seq), F32)],
        compiler_params=_params(("parallel",)),
        name="gate_scan",
    )(pre)


SCAN_SHIFTS = (1, 2, 4)


def _s5_kernel(u_ref, bmat_ref, cmat_ref, astep_ref, apow_ref, d_ref, wglu_ref, o_ref,
               st_ref, carry_ref):
    nch = S5_CHANNELS

    @pl.when(pl.program_id(1) == 0)
    def _():
        carry_ref[...] = jnp.zeros_like(carry_ref)

    u = u_ref[0]
    st_ref[...] = _dot(u.astype(BF16), bmat_ref[...])
    n_blocks = u.shape[0] // SUBLANES

    def body(i, carry):
        cr, ci = carry
        r0 = pl.multiple_of(i * SUBLANES, SUBLANES)
        xr = st_ref[pl.ds(r0, SUBLANES), :nch]
        xi = st_ref[pl.ds(r0, SUBLANES), nch:]
        for k, shift in enumerate(SCAN_SHIFTS):
            ar, ai = astep_ref[k, :, :nch], astep_ref[k, :, nch:]
            sr, si = pltpu.roll(xr, shift, 0), pltpu.roll(xi, shift, 0)
            xr, xi = xr + (ar * sr - ai * si), xi + (ar * si + ai * sr)
        pr, pi = apow_ref[:, :nch], apow_ref[:, nch:]
        xr, xi = xr + (pr * cr - pi * ci), xi + (pr * ci + pi * cr)
        st_ref[pl.ds(r0, SUBLANES), :nch] = xr
        st_ref[pl.ds(r0, SUBLANES), nch:] = xi
        last = SUBLANES - 1
        return (jnp.broadcast_to(xr[last:, :], xr.shape), jnp.broadcast_to(xi[last:, :], xi.shape))

    cr, ci = lax.fori_loop(0, n_blocks, body, (carry_ref[:, :nch], carry_ref[:, nch:]))
    carry_ref[:, :nch] = cr
    carry_ref[:, nch:] = ci

    y = _dot(st_ref[...].astype(BF16), cmat_ref[...]) + d_ref[...] * u
    y = y * (0.5 * (1.0 + jnp.tanh(math.sqrt(2.0 / math.pi) * (y + 0.044715 * (y * y * y)))))
    o_ref[0] = y * _sigmoid(_dot(y.astype(BF16), wglu_ref[...]))


def _s5(u, bmat, cmat, astep, apow, d_skip, w_glu, tb):
    bsz, seq, width = u.shape
    tok = lambda b, i: (b, i, 0)
    return pl.pallas_call(
        _s5_kernel,
        grid=(bsz, seq // tb),
        in_specs=[pl.BlockSpec((1, tb, width), tok)] + [
            _const_spec(a.shape) for a in (bmat, cmat, astep, apow, d_skip, w_glu)],
        out_specs=pl.BlockSpec((1, tb, width), tok),
        out_shape=jax.ShapeDtypeStruct(u.shape, F32),
        scratch_shapes=[pltpu.VMEM((tb, 2 * S5_CHANNELS), F32),
                        pltpu.VMEM((SUBLANES, 2 * S5_CHANNELS), F32)],
        compiler_params=_params(("parallel", "arbitrary")),
        name="s5_scan",
    )(u, bmat, cmat, astep, apow, d_skip, w_glu)


def _s5_tables(a_re, a_im, log_dt, b_re, b_im, c_re, c_im):
    g, n, p = b_re.shape
    dt = jnp.exp(log_dt)[:, None]
    mag = jnp.exp(dt * a_re)
    ang = dt * a_im
    abar_r, abar_i = mag * jnp.cos(ang), mag * jnp.sin(ang)
    den = a_re * a_re + a_im * a_im
    pr, qi = abar_r - 1.0, abar_i
    coef_r = (pr * a_re + qi * a_im) / den
    coef_i = (qi * a_re - pr * a_im) / den
    bbar_r = coef_r[..., None] * b_re - coef_i[..., None] * b_im
    bbar_i = coef_r[..., None] * b_im + coef_i[..., None] * b_re
    eye = jnp.eye(g, dtype=F32)

    def in_block(bb):
        return jnp.einsum('gnp,gh->gphn', bb, eye).reshape(g * p, g * n)

    def out_block(cc):
        return jnp.einsum('gpn,gh->gnhp', cc, eye).reshape(g * n, g * p)

    bmat = jnp.concatenate([in_block(bbar_r), in_block(bbar_i)], axis=1).astype(BF16)
    cmat = jnp.concatenate([out_block(c_re), -out_block(c_im)], axis=0).astype(BF16)

    def power(e):
        e = jnp.asarray(e, F32)[:, None, None]
        m = jnp.exp(e * (dt * a_re)[None])
        th = e * ang[None]
        return (m * jnp.cos(th)).reshape(-1, g * n), (m * jnp.sin(th)).reshape(-1, g * n)

    rows = jnp.arange(SUBLANES)
    sr, si = power(SCAN_SHIFTS)
    keep = (rows[None, :] >= jnp.asarray(SCAN_SHIFTS)[:, None]).astype(F32)[:, :, None]
    astep = jnp.concatenate([keep * sr[:, None, :], keep * si[:, None, :]], axis=-1)
    wr, wi = power(rows + 1)
    apow = jnp.concatenate([wr, wi], axis=-1)
    return bmat, cmat, astep, apow


def _mlstm_kernel(qk_ref, v_ref, o_ref, grow_ref, gcol_ref, cw_ref, cb_ref, wq_ref, wk_ref,
                  ng_ref, out_ref, tail_ref, ckv_ref, n_ref, m_ref):
    chunk = qk_ref.shape[1]
    nh = MLSTM_HEADS

    @pl.when(pl.program_id(1) == 0)
    def _():
        tail_ref[...] = jnp.zeros_like(tail_ref)
        ckv_ref[...] = jnp.zeros_like(ckv_ref)
        n_ref[...] = jnp.zeros_like(n_ref)
        m_ref[...] = jnp.zeros_like(m_ref)

    x = qk_ref[0]
    ext = jnp.concatenate([tail_ref[...], x], axis=0)
    conv = cb_ref[...] + cw_ref[CONV_TAPS - 1:CONV_TAPS, :] * x
    for tap in range(CONV_TAPS - 1):
        back = CONV_TAPS - 1 - tap
        conv = conv + cw_ref[tap:tap + 1, :] * ext[SUBLANES - back:SUBLANES - back + chunk, :]
    tail_ref[...] = x[chunk - SUBLANES:, :]
    cx = _silu(conv)

    row_id = lax.broadcasted_iota(jnp.int32, (chunk, chunk), 0)
    col_id = lax.broadcasted_iota(jnp.int32, (chunk, chunk), 1)
    causal = row_id >= col_id
    k_scale = MLSTM_HEAD_DIM ** -0.5
    for hd in range(nh):
        cols = slice(hd * LANES, (hd + 1) * LANES)
        cxh = cx[:, cols].astype(BF16)
        q = _dot(cxh, wq_ref[hd])
        k = _dot(cxh, wk_ref[hd]) * k_scale
        v = v_ref[0, :, cols]
        qb, kb = q.astype(BF16), k.astype(BF16)
        li_row = grow_ref[0, hd:hd + 1, :]
        b_row = grow_ref[0, nh + hd:nh + hd + 1, :]
        li_col = gcol_ref[0, :, hd:hd + 1]
        b_col = gcol_ref[0, :, nh + hd:nh + hd + 1]
        m_st = m_ref[hd:hd + 1, 0:1]
        n_st = n_ref[hd:hd + 1, :]

        dmat = jnp.where(causal, b_col - b_row + li_row, -jnp.inf)
        m_inter = b_col + m_st
        m_t = jnp.maximum(m_inter, jnp.max(dmat, axis=1, keepdims=True))
        s = _dot_nt(qb, kb) * jnp.exp(dmat - m_t)
        inter = jnp.exp(m_inter - m_t)
        num = _dot(s.astype(BF16), v.astype(BF16)) + inter * _dot(qb, ckv_ref[hd].astype(BF16))
        den = jnp.sum(s, axis=1, keepdims=True) + inter * jnp.sum(q * n_st, axis=1, keepdims=True)
        h = num / jnp.maximum(jnp.abs(den), jnp.exp(-m_t))

        b_last = b_row[:, chunk - 1:chunk]
        w_row = b_last - b_row + li_row
        w_col = b_last - b_col + li_col
        m_new = jnp.maximum(b_last + m_st, jnp.max(w_row, axis=1, keepdims=True))
        decay = jnp.exp(b_last + m_st - m_new)
        ws_col = jnp.exp(w_col - m_new)
        ckv_ref[hd] = decay * ckv_ref[hd] + _dot(k.T.astype(BF16), (v * ws_col).astype(BF16))
        n_ref[hd:hd + 1, :] = decay * n_st + jnp.sum(ws_col * k, axis=0, keepdims=True)
        m_ref[hd:hd + 1, :] = jnp.broadcast_to(m_new, (1, LANES))

        hn = h * lax.rsqrt(jnp.sum(h * h, axis=1, keepdims=True) * (1.0 / MLSTM_HEAD_DIM) + NORM_EPS)
        out_ref[0, :, cols] = _sigmoid(o_ref[0, :, cols]) * (hn * ng_ref[:, cols])


def _mlstm(qk, v, o, gate_rows, gate_cols, conv_w, conv_b, wq, wk, norm_g):
    bsz, seq, width = qk.shape
    chunk = MLSTM_CHUNK
    tok = lambda b, i: (b, i, 0)
    tok_spec = pl.BlockSpec((1, chunk, width), tok)
    return pl.pallas_call(
        _mlstm_kernel,
        grid=(bsz, seq // chunk),
        in_specs=[tok_spec, tok_spec, tok_spec,
                  pl.BlockSpec((1, GATE_ROWS, chunk), lambda b, i: (b, 0, i)),
                  pl.BlockSpec((1, chunk, GATE_ROWS), tok)] + [
            _const_spec(a.shape) for a in (conv_w, conv_b, wq, wk, norm_g)],
        out_specs=tok_spec,
        out_shape=jax.ShapeDtypeStruct(qk.shape, F32),
        scratch_shapes=[pltpu.VMEM((SUBLANES, width), F32),
                        pltpu.VMEM((MLSTM_HEADS, LANES, LANES), F32),
                        pltpu.VMEM((SUBLANES, LANES), F32),
                        pltpu.VMEM((SUBLANES, LANES), F32)],
        compiler_params=_params(("parallel", "arbitrary")),
        name="mlstm_chunkwise",
    )(qk, v, o, gate_rows, gate_cols, conv_w, conv_b, wq, wk, norm_g)


FOX_BIAS_LANES = 6
FOX_QUERY_SPLIT = 2
FOX_VT_ROWS = FOX_HEAD_DIM + 16


def _fox_kernel(qi_ref, ki_ref, q_ref, qb_ref, k_ref, kb_ref, vt_ref, o_ref,
                qx_ref, m_ref, acc_ref):
    step = pl.program_id(1)
    qi, ki = qi_ref[step], ki_ref[step]
    tq, tk = q_ref.shape[1], k_ref.shape[1]
    hd = FOX_HEAD_DIM

    def pair_lanes(pair):
        return slice(pair * LANES, (pair + 1) * LANES)

    @pl.when(ki == 0)
    def _():
        lane = lax.broadcasted_iota(jnp.int32, (1, LANES), 1)
        for pair in range(FOX_PAIRS):
            q = q_ref[0, :, pair_lanes(pair)] * (hd ** -0.5 * LOG2_E)
            qb = qb_ref[0, :, pair_lanes(pair)]
            for hh in range(2):
                in_head = (lane >= hh * hd) & (lane < (hh + 1) * hd)
                in_bias = (lane >= hh * FOX_BIAS_LANES) & (lane < (hh + 1) * FOX_BIAS_LANES)
                qx_ref[2 * pair + hh, :, :LANES] = jnp.where(in_head, q, 0.0).astype(BF16)
                qx_ref[2 * pair + hh, :, LANES:] = jnp.where(in_bias, qb, jnp.zeros_like(qb))
        m_ref[...] = jnp.full_like(m_ref, NEG_BIG)
        acc_ref[...] = jnp.zeros_like(acc_ref)

    def attend(diagonal):
        kx = [jnp.concatenate([k_ref[0, :, pair_lanes(pair)], kb_ref[0, :, pair_lanes(pair)]], axis=1)
              for pair in range(FOX_PAIRS)]
        tc = tq // FOX_QUERY_SPLIT
        chains = [(hh, c * tc) for hh in range(FOX_HEADS) for c in range(FOX_QUERY_SPLIT)]
        scores, probs = {}, {}

        def score(hh, c0):
            s = _dot_nt(kx[hh // 2], qx_ref[hh, c0:c0 + tc, :])
            if diagonal:
                visible = (lax.broadcasted_iota(jnp.int32, (tk, tc), 0)
                           <= lax.broadcasted_iota(jnp.int32, (tk, tc), 1) + c0)
                s = jnp.where(visible, s, NEG_BIG)
            scores[hh, c0] = s

        def softmax(hh, c0):
            s = scores.pop((hh, c0))
            m_prev = m_ref[hh, :, c0:c0 + tc]
            m_new = jnp.maximum(m_prev, jnp.max(s, axis=0, keepdims=True))
            m_ref[hh, :, c0:c0 + tc] = m_new
            probs[hh, c0] = (jnp.exp2(m_prev - m_new), jnp.exp2(s - m_new).astype(BF16))

        def value(hh, c0):
            alpha, p = probs.pop((hh, c0))
            acc_ref[hh, :, c0:c0 + tc] = alpha * acc_ref[hh, :, c0:c0 + tc] + _dot(vt_ref[0, hh], p)

        n = len(chains)
        for i in range(n + 2):
            if i < n:
                score(*chains[i])
            if 1 <= i <= n:
                softmax(*chains[i - 1])
            if i >= 2:
                value(*chains[i - 2])

    @pl.when(ki < qi)
    def _():
        attend(False)

    @pl.when(ki == qi)
    def _():
        attend(True)
        for pair in range(FOX_PAIRS):
            out_t = jnp.concatenate([acc_ref[hh, :hd, :] / acc_ref[hh, hd:hd + 1, :]
                                     for hh in (2 * pair, 2 * pair + 1)], axis=0)
            o_ref[0, :, pair_lanes(pair)] = out_t.T


def _fox(q, qb, k, kb, v_t, tq):
    bsz, seq, width = q.shape
    nq = seq // tq
    q_idx = [i for i in range(nq) for _ in range(i + 1)]
    k_idx = [j for i in range(nq) for j in range(i + 1)]
    q_tbl, k_tbl = jnp.asarray(q_idx, jnp.int32), jnp.asarray(k_idx, jnp.int32)
    q_spec = pl.BlockSpec((1, tq, width), lambda b, s, qt, kt: (b, qt[s], 0))
    k_spec = pl.BlockSpec((1, tq, width), lambda b, s, qt, kt: (b, kt[s], 0))
    grid_spec = pltpu.PrefetchScalarGridSpec(
        num_scalar_prefetch=2,
        grid=(bsz, len(q_idx)),
        in_specs=[q_spec, q_spec, k_spec, k_spec,
                  pl.BlockSpec((1, FOX_HEADS, FOX_VT_ROWS, tq), lambda b, s, qt, kt: (b, 0, 0, kt[s]))],
        out_specs=q_spec,
        scratch_shapes=[pltpu.VMEM((FOX_HEADS, tq, 2 * LANES), BF16),
                        pltpu.VMEM((FOX_HEADS, 1, tq), F32),
                        pltpu.VMEM((FOX_HEADS, FOX_VT_ROWS, tq), F32)],
    )
    v_heads = v_t.reshape(bsz, FOX_HEADS, FOX_HEAD_DIM, seq)
    ones = jnp.ones((bsz, FOX_HEADS, FOX_VT_ROWS - FOX_HEAD_DIM, seq), v_t.dtype)
    v_ext = jnp.concatenate([v_heads, ones], axis=2)
    return pl.pallas_call(
        _fox_kernel, grid_spec=grid_spec,
        out_shape=jax.ShapeDtypeStruct((bsz, seq, width), F32),
        compiler_params=_params(("parallel", "arbitrary")),
        name="forgetting_attention",
    )(q_tbl, k_tbl, q, qb, k, kb, v_ext)


def _fox_bias_lanes(split):
    bsz, _, _, seq = split.shape
    lo = 2 * MLSTM_HEADS
    pieces = jnp.transpose(split[:, :, lo:lo + FOX_HEADS, :], (0, 3, 2, 1))
    ones = jnp.ones_like(pieces)

    def lanes(per_head):
        per_pair = per_head.reshape(bsz, seq, FOX_PAIRS, 2 * FOX_BIAS_LANES)
        per_pair = jnp.pad(per_pair, ((0, 0), (0, 0), (0, 0), (0, LANES - 2 * FOX_BIAS_LANES)))
        return per_pair.reshape(bsz, seq, FOX_PAIRS * LANES).astype(BF16)

    return (lanes(jnp.concatenate([pieces, ones], axis=-1)),
            lanes(jnp.concatenate([ones, -pieces], axis=-1)))


def _merge_kernel(x_ref, ys_ref, ym_ref, yf_ref, gate_ref, g1_ref, ws_ref, wm_ref, wf_ref,
                  wo_ref, o_ref):
    d = x_ref.shape[-1]
    merged = (_sigmoid(gate_ref[0, :, :d]) * _dot(ys_ref[0].astype(BF16), ws_ref[...])
              + _sigmoid(gate_ref[0, :, d:2 * d]) * _dot(ym_ref[0].astype(BF16), wm_ref[...])
              + _sigmoid(gate_ref[0, :, 2 * d:]) * _dot(yf_ref[0].astype(BF16), wf_ref[...]))
    o_ref[0] = x_ref[0] + g1_ref[0] * _dot(merged.astype(BF16), wo_ref[...])


def _merge(x, y_s5, y_m, y_f, gates, g1, w_up_s5, w_up_m, w_up_f, w_out, tm):
    bsz, seq, d = x.shape
    tok = lambda b, i: (b, i, 0)
    acts = (x, y_s5, y_m, y_f, gates)
    weights = (w_up_s5, w_up_m, w_up_f, w_out)
    return pl.pallas_call(
        _merge_kernel,
        grid=(bsz, seq // tm),
        in_specs=[pl.BlockSpec((1, tm, a.shape[-1]), tok) for a in acts]
        + [pl.BlockSpec((1, 1, d), lambda b, i: (b, 0, 0))]
        + [_const_spec(w.shape) for w in weights],
        out_specs=pl.BlockSpec((1, tm, d), tok),
        out_shape=jax.ShapeDtypeStruct(x.shape, F32),
        compiler_params=_params(("parallel", "parallel")),
        name="merge_output_projection",
    )(*acts, g1, *weights)


FFN_COLS = 256


def _ffn_kernel(x_ref, sh_ref, sc_ref, g2_ref, gain_ref, w1_ref, w3_ref, w2_ref, fg_ref, o_ref,
                *, final_norm):
    x = x_ref[0]
    h = _rms_modulate(x, gain_ref[...], sc_ref[0], sh_ref[0]).astype(BF16)
    starts = list(range(0, w1_ref.shape[1], FFN_COLS))

    def up(c0):
        return _dot(h, w1_ref[:, c0:c0 + FFN_COLS]), _dot(h, w3_ref[:, c0:c0 + FFN_COLS])

    acc = jnp.zeros(x.shape, F32)
    nxt = up(starts[0])
    for i, c0 in enumerate(starts):
        a, b = nxt
        if i + 1 < len(starts):
            nxt = up(starts[i + 1])
        acc = acc + _dot((_silu(a) * b).astype(BF16), w2_ref[c0:c0 + FFN_COLS, :])
    y = x + g2_ref[0] * acc
    if final_norm:
        y = (y * lax.rsqrt(jnp.mean(y * y, axis=-1, keepdims=True) + NORM_EPS)) * fg_ref[...]
    o_ref[0] = y


def _ffn(x, shift, scale, g2, gain, w1, w3, w2, final_g, final_norm, tm):
    bsz, seq, d = x.shape
    tok = lambda b, i: (b, i, 0)
    vec = pl.BlockSpec((1, 1, d), lambda b, i: (b, 0, 0))
    return pl.pallas_call(
        functools.partial(_ffn_kernel, final_norm=final_norm),
        grid=(bsz, seq // tm),
        in_specs=[pl.BlockSpec((1, tm, d), tok), vec, vec, vec]
        + [_const_spec(a.shape) for a in (gain, w1, w3, w2, final_g)],
        out_specs=pl.BlockSpec((1, tm, d), tok),
        out_shape=jax.ShapeDtypeStruct(x.shape, F32),
        compiler_params=_params(("parallel", "parallel")),
        name="norm_swiglu",
    )(x, shift, scale, g2, gain, w1, w3, w2, final_g)


def _pad_heads(w, heads, head_dim):
    lead = w.shape[:-1]
    w = w.reshape(lead + (heads, head_dim))
    w = jnp.pad(w, [(0, 0)] * len(lead) + [(0, 0), (0, LANES - head_dim)])
    return w.reshape(lead + (heads * LANES,))


def _layer_weights(w_in, b_in):
    sizes = (S5_WIDTH, MLSTM_WIDTH, MLSTM_WIDTH, MLSTM_WIDTH, MLSTM_HEADS, MLSTM_HEADS,
             FOX_WIDTH, FOX_WIDTH, FOX_WIDTH, FOX_HEADS)
    offs = [0]
    for n in sizes:
        offs.append(offs[-1] + n)
    both = jnp.concatenate([w_in, b_in[None, :]], axis=0)
    part = [both[:, offs[i]:offs[i + 1]] for i in range(len(sizes))] + [both[:, offs[-1]:]]
    pad_m = lambda w: _pad_heads(w, MLSTM_HEADS, MLSTM_HEAD_DIM)
    cat = jnp.concatenate([part[0], pad_m(part[1]), pad_m(part[2]), pad_m(part[3]),
                           part[6], part[7], part[10]], axis=1)
    small = jnp.concatenate([part[4], part[5], part[9]], axis=1)
    small = jnp.pad(small, ((0, 0), (0, GATE_ROWS - small.shape[1])))
    cat_t = jnp.concatenate([small, part[8]], axis=1).T
    return (cat[:-1].astype(BF16), cat[-1:], cat_t[:, :-1].astype(BF16), cat_t[:, -1:])


def kernel(x, c, mod_w, mod_b, norm1_g, norm2_g, w_in, b_in, s5_a_re, s5_a_im, s5_log_dt, s5_b_re, s5_b_im, s5_c_re, s5_c_im, s5_d, s5_w_glu, mlstm_conv_w, mlstm_conv_b, mlstm_wq, mlstm_wk, mlstm_norm_g, w_up_s5, w_up_mlstm, w_up_fox, w_out, ffn_w1, ffn_w3, ffn_w2, final_g):
    bsz, seq, d = x.shape
    depth = mod_w.shape[0]
    assert seq % MLSTM_CHUNK == 0
    tm = _row_tile(seq, 512)
    tq = _row_tile(seq, 512)
    ts = _row_tile(seq, 512)

    mod = _modulation(c, mod_w, mod_b)
    final_gain = final_g.reshape(1, d)
    for l in range(depth):
        sh1, sc1, g1, sh2, sc2, g2 = [mod[l, :, i * d:(i + 1) * d].reshape(bsz, 1, d)
                                      for i in range(6)]
        w_cat, b_cat, w_t, b_t = _layer_weights(w_in[l], b_in[l])
        (s5_u, m_qk, m_v, m_o, f_q, f_k, gates, gate_pre, f_vt) = _input_projection(
            x, sh1, sc1, norm1_g[l].reshape(1, d), w_cat, b_cat, w_t, b_t, tm)

        gate_rows, gate_split = _gate_scan(gate_pre)
        gate_cols = jnp.swapaxes(gate_rows, 1, 2)
        f_qb, f_kb = _fox_bias_lanes(gate_split)

        bmat, cmat, astep, apow = _s5_tables(s5_a_re[l], s5_a_im[l], s5_log_dt[l], s5_b_re[l],
                                             s5_b_im[l], s5_c_re[l], s5_c_im[l])
        y_s5 = _s5(s5_u, bmat, cmat, astep, apow, s5_d[l].reshape(1, -1),
                   s5_w_glu[l].astype(BF16), ts)

        pad_hh = ((0, 0), (0, LANES - MLSTM_HEAD_DIM), (0, LANES - MLSTM_HEAD_DIM))
        y_m = _mlstm(m_qk, m_v, m_o, gate_rows, gate_cols,
                     _pad_heads(mlstm_conv_w[l], MLSTM_HEADS, MLSTM_HEAD_DIM),
                     _pad_heads(mlstm_conv_b[l].reshape(1, -1), MLSTM_HEADS, MLSTM_HEAD_DIM),
                     jnp.pad(mlstm_wq[l], pad_hh).astype(BF16),
                     jnp.pad(mlstm_wk[l], pad_hh).astype(BF16),
                     _pad_heads(mlstm_norm_g[l].reshape(1, -1), MLSTM_HEADS, MLSTM_HEAD_DIM))

        y_f = _fox(f_q, f_qb, f_k, f_kb, f_vt, tq)

        w_up_m = _pad_heads(w_up_mlstm[l].T, MLSTM_HEADS, MLSTM_HEAD_DIM).T
        x = _merge(x, y_s5, y_m, y_f, gates, g1, w_up_s5[l].astype(BF16), w_up_m.astype(BF16),
                   w_up_fox[l].astype(BF16), w_out[l].astype(BF16), tm)
        x = _ffn(x, sh2, sc2, g2, norm2_g[l].reshape(1, d), ffn_w1[l].astype(BF16),
                 ffn_w3[l].astype(BF16), ffn_w2[l].astype(BF16), final_gain,
                 l == depth - 1, tm)
    return x
```

```python
import functools
import math

import jax
import jax.numpy as jnp
from jax import lax
from jax.experimental import pallas as pl
from jax.experimental.pallas import tpu as pltpu

F32 = jnp.float32
BF16 = jnp.bfloat16

LANES = 128
SUBLANES = 8
VMEM_LIMIT_BYTES = 56 * 1024 * 1024

NORM_EPS = 1e-6
S5_GROUPS = 16
S5_GROUP_DIM = 16
S5_STATE = 64
S5_WIDTH = S5_GROUPS * S5_GROUP_DIM
S5_CHANNELS = S5_GROUPS * S5_STATE
MLSTM_HEADS = 4
MLSTM_HEAD_DIM = 96
MLSTM_WIDTH = MLSTM_HEADS * MLSTM_HEAD_DIM
MLSTM_PAD_WIDTH = MLSTM_HEADS * LANES
MLSTM_CHUNK = 128
CONV_TAPS = 4
FOX_HEADS = 6
FOX_HEAD_DIM = 64
FOX_WIDTH = FOX_HEADS * FOX_HEAD_DIM
FOX_PAIRS = FOX_WIDTH // LANES
GATE_ROWS = 16
NEG_BIG = -1e30
LOG2_E = math.log2(math.e)

NT_DIMS = (((1,), (1,)), ((), ()))


def _dot(a, b):
    return jnp.dot(a, b, preferred_element_type=F32)


def _dot_nt(a, b):
    return lax.dot_general(a, b, NT_DIMS, preferred_element_type=F32)


def _sigmoid(x):
    return 1.0 / (1.0 + jnp.exp(-x))


def _silu(x):
    return x * _sigmoid(x)


def _log_sigmoid(x):
    return jnp.minimum(x, 0.0) - jnp.log1p(jnp.exp(-jnp.abs(x)))


def _rms_modulate(x, gain, scale, shift):
    y = x * lax.rsqrt(jnp.mean(x * x, axis=-1, keepdims=True) + NORM_EPS)
    return (y * gain) * (1.0 + scale) + shift


def _params(semantics):
    return pltpu.CompilerParams(dimension_semantics=semantics, vmem_limit_bytes=VMEM_LIMIT_BYTES)


def _const_spec(shape):
    nd = len(shape)
    return pl.BlockSpec(shape, lambda *_: (0,) * nd, pipeline_mode=pl.Buffered(1))


def _row_tile(seq, want):
    t = min(want, seq)
    assert seq % t == 0 and t % SUBLANES == 0
    return t


def _mod_kernel(c_ref, w_ref, b_ref, o_ref):
    cf = _silu(c_ref[...]).astype(BF16)
    o_ref[0] = _dot(cf, w_ref[0].astype(BF16)) + b_ref[0]


def _modulation(c, mod_w, mod_b):
    depth, d, six_d = mod_w.shape
    bsz = c.shape[0]
    n_col = six_d // d
    return pl.pallas_call(
        _mod_kernel,
        grid=(depth, n_col),
        in_specs=[pl.BlockSpec((bsz, d), lambda l, j: (0, 0)),
                  pl.BlockSpec((1, d, d), lambda l, j: (l, 0, j)),
                  pl.BlockSpec((1, 1, d), lambda l, j: (l, 0, j))],
        out_specs=pl.BlockSpec((1, bsz, d), lambda l, j: (l, 0, j)),
        out_shape=jax.ShapeDtypeStruct((depth, bsz, six_d), F32),
        compiler_params=_params(("parallel", "parallel")),
        name="adaln_modulation",
    )(c, mod_w, mod_b.reshape(depth, 1, six_d))


IN_SEGMENTS = ((S5_WIDTH, F32), (MLSTM_PAD_WIDTH, F32), (MLSTM_PAD_WIDTH, F32),
               (MLSTM_PAD_WIDTH, F32), (FOX_WIDTH, F32), (FOX_WIDTH, BF16))
IN_T_SEGMENTS = ((GATE_ROWS, F32), (FOX_WIDTH, BF16))
DOT_COLS = 512


def _inproj_kernel(x_ref, sh_ref, sc_ref, g_ref, w_ref, b_ref, wt_ref, bt_ref, *out_refs):
    h = _rms_modulate(x_ref[0], g_ref[...], sc_ref[0], sh_ref[0]).astype(BF16)
    n_t = len(IN_T_SEGMENTS)
    start = 0
    for o_ref in out_refs[:-n_t]:
        width = o_ref.shape[-1]
        for c0 in range(0, width, DOT_COLS):
            cw = min(DOT_COLS, width - c0)
            z = _dot(h, w_ref[:, start + c0:start + c0 + cw]) + b_ref[:, start + c0:start + c0 + cw]
            o_ref[0, :, c0:c0 + cw] = z.astype(o_ref.dtype)
        start += width
    start = 0
    for o_ref in out_refs[-n_t:]:
        rows = o_ref.shape[1]
        z = _dot_nt(wt_ref[start:start + rows, :], h) + bt_ref[start:start + rows, :]
        o_ref[0] = z.astype(o_ref.dtype)
        start += rows


def _input_projection(x, shift, scale, gain, w_cat, b_cat, w_t, b_t, tm):
    bsz, seq, d = x.shape
    segs = IN_SEGMENTS + ((w_cat.shape[1] - sum(w for w, _ in IN_SEGMENTS), F32),)
    tok = lambda b, i: (b, i, 0)
    vec = lambda b, i: (b, 0, 0)
    out_shape = [jax.ShapeDtypeStruct((bsz, seq, w), dt) for w, dt in segs]
    out_shape += [jax.ShapeDtypeStruct((bsz, r, seq), dt) for r, dt in IN_T_SEGMENTS]
    out_specs = [pl.BlockSpec((1, tm, w), tok) for w, _ in segs]
    out_specs += [pl.BlockSpec((1, r, tm), lambda b, i: (b, 0, i)) for r, _ in IN_T_SEGMENTS]
    return pl.pallas_call(
        _inproj_kernel,
        grid=(bsz, seq // tm),
        in_specs=[pl.BlockSpec((1, tm, d), tok),
                  pl.BlockSpec((1, 1, d), vec), pl.BlockSpec((1, 1, d), vec),
                  _const_spec(gain.shape), _const_spec(w_cat.shape), _const_spec(b_cat.shape),
                  _const_spec(w_t.shape), _const_spec(b_t.shape)],
        out_specs=out_specs,
        out_shape=out_shape,
        compiler_params=_params(("parallel", "parallel")),
        name="norm_input_projection",
    )(x, shift, scale, gain, w_cat, b_cat, w_t, b_t)


def _bf16_round(x):
    return x.astype(BF16).astype(F32)


def _gate_scan_kernel(x_ref, o_ref, split_ref):
    x = x_ref[0]
    seq = x.shape[1]
    seg = MLSTM_CHUNK
    lane = lax.broadcasted_iota(jnp.int32, x.shape, 1) & (seg - 1)
    v = _log_sigmoid(x)
    k = 1
    while k < seg:
        v = v + jnp.where(lane >= k, pltpu.roll(v, k, 1), 0.0)
        k *= 2
    row = lax.broadcasted_iota(jnp.int32, (GATE_ROWS, seg), 0)
    carry = jnp.zeros((GATE_ROWS, 1), F32)
    for j in range(seq // seg):
        sl = slice(j * seg, (j + 1) * seg)
        local = v[:, sl]
        total = local + carry
        o_ref[0, :, sl] = jnp.where(row < MLSTM_HEADS, x[:, sl],
                                    jnp.where(row < 2 * MLSTM_HEADS, local, total))
        scaled = total * LOG2_E
        hi = _bf16_round(scaled)
        mid = _bf16_round(scaled - hi)
        split_ref[0, 0, :, sl] = hi
        split_ref[0, 1, :, sl] = mid
        split_ref[0, 2, :, sl] = _bf16_round((scaled - hi) - mid)
        carry = carry + local[:, seg - 1:seg]


def _gate_scan(pre):
    bsz, rows, seq = pre.shape
    spec = pl.BlockSpec((1, rows, seq), lambda b: (b, 0, 0))
    return pl.pallas_call(
        _gate_scan_kernel, grid=(bsz,), in_specs=[spec],
        out_specs=[spec, pl.BlockSpec((1, 3, rows, seq), lambda b: (b, 0, 0, 0))],
        out_shape=[jax.ShapeDtypeStruct(pre.shape, F32),
                   jax.ShapeDtypeStruct((bsz, 3, rows, seq), F32)],
        compiler_params=_params(("parallel",)),
        name="gate_scan",
    )(pre)


SCAN_SHIFTS = (1, 2, 4)


def _s5_kernel(u_ref, bmat_ref, cmat_ref, astep_ref, apow_ref, d_ref, wglu_ref, o_ref,
               st_ref, carry_ref):
    nch = S5_CHANNELS

    @pl.when(pl.program_id(1) == 0)
    def _():
        carry_ref[...] = jnp.zeros_like(carry_ref)

    u = u_ref[0]
    st_ref[...] = _dot(u.astype(BF16), bmat_ref[...])
    n_blocks = u.shape[0] // SUBLANES

    def body(i, carry):
        cr, ci = carry
        r0 = pl.multiple_of(i * SUBLANES, SUBLANES)
        xr = st_ref[pl.ds(r0, SUBLANES), :nch]
        xi = st_ref[pl.ds(r0, SUBLANES), nch:]
        for k, shift in enumerate(SCAN_SHIFTS):
            ar, ai = astep_ref[k, :, :nch], astep_ref[k, :, nch:]
            sr, si = pltpu.roll(xr, shift, 0), pltpu.roll(xi, shift, 0)
            xr, xi = xr + (ar * sr - ai * si), xi + (ar * si + ai * sr)
        pr, pi = apow_ref[:, :nch], apow_ref[:, nch:]
        xr, xi = xr + (pr * cr - pi * ci), xi + (pr * ci + pi * cr)
        st_ref[pl.ds(r0, SUBLANES), :nch] = xr
        st_ref[pl.ds(r0, SUBLANES), nch:] = xi
        last = SUBLANES - 1
        return (jnp.broadcast_to(xr[last:, :], xr.shape), jnp.broadcast_to(xi[last:, :], xi.shape))

    cr, ci = lax.fori_loop(0, n_blocks, body, (carry_ref[:, :nch], carry_ref[:, nch:]))
    carry_ref[:, :nch] = cr
    carry_ref[:, nch:] = ci

    y = _dot(st_ref[...].astype(BF16), cmat_ref[...]) + d_ref[...] * u
    y = y * (0.5 * (1.0 + jnp.tanh(math.sqrt(2.0 / math.pi) * (y + 0.044715 * (y * y * y)))))
    o_ref[0] = y * _sigmoid(_dot(y.astype(BF16), wglu_ref[...]))


def _s5(u, bmat, cmat, astep, apow, d_skip, w_glu, tb):
    bsz, seq, width = u.shape
    tok = lambda b, i: (b, i, 0)
    return pl.pallas_call(
        _s5_kernel,
        grid=(bsz, seq // tb),
        in_specs=[pl.BlockSpec((1, tb, width), tok)] + [
            _const_spec(a.shape) for a in (bmat, cmat, astep, apow, d_skip, w_glu)],
        out_specs=pl.BlockSpec((1, tb, width), tok),
        out_shape=jax.ShapeDtypeStruct(u.shape, F32),
        scratch_shapes=[pltpu.VMEM((tb, 2 * S5_CHANNELS), F32),
                        pltpu.VMEM((SUBLANES, 2 * S5_CHANNELS), F32)],
        compiler_params=_params(("parallel", "arbitrary")),
        name="s5_scan",
    )(u, bmat, cmat, astep, apow, d_skip, w_glu)


def _s5_tables(a_re, a_im, log_dt, b_re, b_im, c_re, c_im):
    g, n, p = b_re.shape
    dt = jnp.exp(log_dt)[:, None]
    mag = jnp.exp(dt * a_re)
    ang = dt * a_im
    abar_r, abar_i = mag * jnp.cos(ang), mag * jnp.sin(ang)
    den = a_re * a_re + a_im * a_im
    pr, qi = abar_r - 1.0, abar_i
    coef_r = (pr * a_re + qi * a_im) / den
    coef_i = (qi * a_re - pr * a_im) / den
    bbar_r = coef_r[..., None] * b_re - coef_i[..., None] * b_im
    bbar_i = coef_r[..., None] * b_im + coef_i[..., None] * b_re
    eye = jnp.eye(g, dtype=F32)

    def in_block(bb):
        return jnp.einsum('gnp,gh->gphn', bb, eye).reshape(g * p, g * n)

    def out_block(cc):
        return jnp.einsum('gpn,gh->gnhp', cc, eye).reshape(g * n, g * p)

    bmat = jnp.concatenate([in_block(bbar_r), in_block(bbar_i)], axis=1).astype(BF16)
    cmat = jnp.concatenate([out_block(c_re), -out_block(c_im)], axis=0).astype(BF16)

    def power(e):
        e = jnp.asarray(e, F32)[:, None, None]
        m = jnp.exp(e * (dt * a_re)[None])
        th = e * ang[None]
        return (m * jnp.cos(th)).reshape(-1, g * n), (m * jnp.sin(th)).reshape(-1, g * n)

    rows = jnp.arange(SUBLANES)
    sr, si = power(SCAN_SHIFTS)
    keep = (rows[None, :] >= jnp.asarray(SCAN_SHIFTS)[:, None]).astype(F32)[:, :, None]
    astep = jnp.concatenate([keep * sr[:, None, :], keep * si[:, None, :]], axis=-1)
    wr, wi = power(rows + 1)
    apow = jnp.concatenate([wr, wi], axis=-1)
    return bmat, cmat, astep, apow


def _mlstm_kernel(qk_ref, v_ref, o_ref, grow_ref, gcol_ref, cw_ref, cb_ref, wq_ref, wk_ref,
                  ng_ref, out_ref, tail_ref, ckv_ref, n_ref, m_ref):
    chunk = qk_ref.shape[1]
    nh = MLSTM_HEADS

    @pl.when(pl.program_id(1) == 0)
    def _():
        tail_ref[...] = jnp.zeros_like(tail_ref)
        ckv_ref[...] = jnp.zeros_like(ckv_ref)
        n_ref[...] = jnp.zeros_like(n_ref)
        m_ref[...] = jnp.zeros_like(m_ref)

    x = qk_ref[0]
    ext = jnp.concatenate([tail_ref[...], x], axis=0)
    conv = cb_ref[...] + cw_ref[CONV_TAPS - 1:CONV_TAPS, :] * x
    for tap in range(CONV_TAPS - 1):
        back = CONV_TAPS - 1 - tap
        conv = conv + cw_ref[tap:tap + 1, :] * ext[SUBLANES - back:SUBLANES - back + chunk, :]
    tail_ref[...] = x[chunk - SUBLANES:, :]
    cx = _silu(conv)

    row_id = lax.broadcasted_iota(jnp.int32, (chunk, chunk), 0)
    col_id = lax.broadcasted_iota(jnp.int32, (chunk, chunk), 1)
    causal = row_id >= col_id
    k_scale = MLSTM_HEAD_DIM ** -0.5
    heads = range(nh)
    cols = [slice(hd * LANES, (hd + 1) * LANES) for hd in heads]
    cxh = [cx[:, cols[hd]].astype(BF16) for hd in heads]
    q = [_dot(cxh[hd], wq_ref[hd]) for hd in heads]
    k = [_dot(cxh[hd], wk_ref[hd]) * k_scale for hd in heads]
    v = [v_ref[0, :, cols[hd]] for hd in heads]
    qb = [x.astype(BF16) for x in q]
    qk = [_dot_nt(qb[hd], k[hd].astype(BF16)) for hd in heads]
    carry = [_dot(qb[hd], ckv_ref[hd].astype(BF16)) for hd in heads]
    k_t = [k[hd].T.astype(BF16) for hd in heads]

    s, inter, m_t, stats = [], [], [], []
    for hd in heads:
        li_row = grow_ref[0, hd:hd + 1, :]
        b_row = grow_ref[0, nh + hd:nh + hd + 1, :]
        li_col = gcol_ref[0, :, hd:hd + 1]
        b_col = gcol_ref[0, :, nh + hd:nh + hd + 1]
        m_st = m_ref[hd:hd + 1, 0:1]
        dmat = jnp.where(causal, b_col - b_row + li_row, -jnp.inf)
        m_inter = b_col + m_st
        m_t.append(jnp.maximum(m_inter, jnp.max(dmat, axis=1, keepdims=True)))
        s.append(qk[hd] * jnp.exp(dmat - m_t[hd]))
        inter.append(jnp.exp(m_inter - m_t[hd]))

        b_last = b_row[:, chunk - 1:chunk]
        w_row = b_last - b_row + li_row
        w_col = b_last - b_col + li_col
        m_new = jnp.maximum(b_last + m_st, jnp.max(w_row, axis=1, keepdims=True))
        stats.append((m_new, jnp.exp(b_last + m_st - m_new), jnp.exp(w_col - m_new)))

    sv = [_dot(s[hd].astype(BF16), v[hd].astype(BF16)) for hd in heads]
    kv = [_dot(k_t[hd], (v[hd] * stats[hd][2]).astype(BF16)) for hd in heads]

    for hd in heads:
        m_new, decay, ws_col = stats[hd]
        n_st = n_ref[hd:hd + 1, :]
        num = sv[hd] + inter[hd] * carry[hd]
        den = (jnp.sum(s[hd], axis=1, keepdims=True)
               + inter[hd] * jnp.sum(q[hd] * n_st, axis=1, keepdims=True))
        h = num / jnp.maximum(jnp.abs(den), jnp.exp(-m_t[hd]))
        ckv_ref[hd] = decay * ckv_ref[hd] + kv[hd]
        n_ref[hd:hd + 1, :] = decay * n_st + jnp.sum(ws_col * k[hd], axis=0, keepdims=True)
        m_ref[hd:hd + 1, :] = jnp.broadcast_to(m_new, (1, LANES))
        hn = h * lax.rsqrt(jnp.sum(h * h, axis=1, keepdims=True) * (1.0 / MLSTM_HEAD_DIM) + NORM_EPS)
        out_ref[0, :, cols[hd]] = _sigmoid(o_ref[0, :, cols[hd]]) * (hn * ng_ref[:, cols[hd]])


def _mlstm(qk, v, o, gate_rows, gate_cols, conv_w, conv_b, wq, wk, norm_g):
    bsz, seq, width = qk.shape
    chunk = MLSTM_CHUNK
    tok = lambda b, i: (b, i, 0)
    tok_spec = pl.BlockSpec((1, chunk, width), tok)
    return pl.pallas_call(
        _mlstm_kernel,
        grid=(bsz, seq // chunk),
        in_specs=[tok_spec, tok_spec, tok_spec,
                  pl.BlockSpec((1, GATE_ROWS, chunk), lambda b, i: (b, 0, i)),
                  pl.BlockSpec((1, chunk, GATE_ROWS), tok)] + [
            _const_spec(a.shape) for a in (conv_w, conv_b, wq, wk, norm_g)],
        out_specs=tok_spec,
        out_shape=jax.ShapeDtypeStruct(qk.shape, F32),
        scratch_shapes=[pltpu.VMEM((SUBLANES, width), F32),
                        pltpu.VMEM((MLSTM_HEADS, LANES, LANES), F32),
                        pltpu.VMEM((SUBLANES, LANES), F32),
                        pltpu.VMEM((SUBLANES, LANES), F32)],
        compiler_params=_params(("parallel", "arbitrary")),
        name="mlstm_chunkwise",
    )(qk, v, o, gate_rows, gate_cols, conv_w, conv_b, wq, wk, norm_g)


FOX_BIAS_LANES = 6
FOX_QUERY_SPLIT = 1
FOX_SCORE_LOOKAHEAD = 2
FOX_VT_ROWS = FOX_HEAD_DIM + 16


def _fox_kernel(qi_ref, ki_ref, q_ref, qb_ref, k_ref, kb_ref, vt_ref, o_ref,
                qx_ref, m_ref, acc_ref):
    step = pl.program_id(1)
    qi, ki = qi_ref[step], ki_ref[step]
    tq, tk = q_ref.shape[1], k_ref.shape[1]
    hd = FOX_HEAD_DIM

    def pair_lanes(pair):
        return slice(pair * LANES, (pair + 1) * LANES)

    @pl.when(ki == 0)
    def _():
        lane = lax.broadcasted_iota(jnp.int32, (1, LANES), 1)
        for pair in range(FOX_PAIRS):
            q = q_ref[0, :, pair_lanes(pair)] * (hd ** -0.5 * LOG2_E)
            qb = qb_ref[0, :, pair_lanes(pair)]
            for hh in range(2):
                in_head = (lane >= hh * hd) & (lane < (hh + 1) * hd)
                in_bias = (lane >= hh * FOX_BIAS_LANES) & (lane < (hh + 1) * FOX_BIAS_LANES)
                qx_ref[2 * pair + hh, :, :LANES] = jnp.where(in_head, q, 0.0).astype(BF16)
                qx_ref[2 * pair + hh, :, LANES:] = jnp.where(in_bias, qb, jnp.zeros_like(qb))
        m_ref[...] = jnp.full_like(m_ref, NEG_BIG)
        acc_ref[...] = jnp.zeros_like(acc_ref)

    def attend(diagonal):
        kx = [jnp.concatenate([k_ref[0, :, pair_lanes(pair)], kb_ref[0, :, pair_lanes(pair)]], axis=1)
              for pair in range(FOX_PAIRS)]
        tc = tq // FOX_QUERY_SPLIT
        chains = [(hh, c * tc) for hh in range(FOX_HEADS) for c in range(FOX_QUERY_SPLIT)]
        scores, probs = {}, {}

        def score(hh, c0):
            s = _dot_nt(kx[hh // 2], qx_ref[hh, c0:c0 + tc, :])
            if diagonal:
                visible = (lax.broadcasted_iota(jnp.int32, (tk, tc), 0)
                           <= lax.broadcasted_iota(jnp.int32, (tk, tc), 1) + c0)
                s = jnp.where(visible, s, NEG_BIG)
            scores[hh, c0] = s

        def softmax(hh, c0):
            s = scores.pop((hh, c0))
            m_prev = m_ref[hh, :, c0:c0 + tc]
            m_new = jnp.maximum(m_prev, jnp.max(s, axis=0, keepdims=True))
            m_ref[hh, :, c0:c0 + tc] = m_new
            probs[hh, c0] = (jnp.exp2(m_prev - m_new), jnp.exp2(s - m_new).astype(BF16))

        def value(hh, c0):
            alpha, p = probs.pop((hh, c0))
            acc_ref[hh, :, c0:c0 + tc] = alpha * acc_ref[hh, :, c0:c0 + tc] + _dot(vt_ref[0, hh], p)

        n, ahead = len(chains), FOX_SCORE_LOOKAHEAD
        for i in range(n + ahead + 1):
            if i < n:
                score(*chains[i])
            if ahead <= i < n + ahead:
                softmax(*chains[i - ahead])
            if i >= ahead + 1:
                value(*chains[i - ahead - 1])

    @pl.when(ki < qi)
    def _():
        attend(False)

    @pl.when(ki == qi)
    def _():
        attend(True)
        for pair in range(FOX_PAIRS):
            out_t = jnp.concatenate([acc_ref[hh, :hd, :] / acc_ref[hh, hd:hd + 1, :]
                                     for hh in (2 * pair, 2 * pair + 1)], axis=0)
            o_ref[0, :, pair_lanes(pair)] = out_t.T


def _fox(q, qb, k, kb, v_t, tq):
    bsz, seq, width = q.shape
    nq = seq // tq
    q_idx = [i for i in range(nq) for _ in range(i + 1)]
    k_idx = [j for i in range(nq) for j in range(i + 1)]
    q_tbl, k_tbl = jnp.asarray(q_idx, jnp.int32), jnp.asarray(k_idx, jnp.int32)
    q_spec = pl.BlockSpec((1, tq, width), lambda b, s, qt, kt: (b, qt[s], 0))
    k_spec = pl.BlockSpec((1, tq, width), lambda b, s, qt, kt: (b, kt[s], 0))
    grid_spec = pltpu.PrefetchScalarGridSpec(
        num_scalar_prefetch=2,
        grid=(bsz, len(q_idx)),
        in_specs=[q_spec, q_spec, k_spec, k_spec,
                  pl.BlockSpec((1, FOX_HEADS, FOX_VT_ROWS, tq), lambda b, s, qt, kt: (b, 0, 0, kt[s]))],
        out_specs=q_spec,
        scratch_shapes=[pltpu.VMEM((FOX_HEADS, tq, 2 * LANES), BF16),
                        pltpu.VMEM((FOX_HEADS, 1, tq), F32),
                        pltpu.VMEM((FOX_HEADS, FOX_VT_ROWS, tq), F32)],
    )
    v_heads = v_t.reshape(bsz, FOX_HEADS, FOX_HEAD_DIM, seq)
    ones = jnp.ones((bsz, FOX_HEADS, FOX_VT_ROWS - FOX_HEAD_DIM, seq), v_t.dtype)
    v_ext = jnp.concatenate([v_heads, ones], axis=2)
    return pl.pallas_call(
        _fox_kernel, grid_spec=grid_spec,
        out_shape=jax.ShapeDtypeStruct((bsz, seq, width), F32),
        compiler_params=_params(("parallel", "arbitrary")),
        name="forgetting_attention",
    )(q_tbl, k_tbl, q, qb, k, kb, v_ext)


def _fox_bias_lanes(split):
    bsz, _, _, seq = split.shape
    lo = 2 * MLSTM_HEADS
    pieces = jnp.transpose(split[:, :, lo:lo + FOX_HEADS, :], (0, 3, 2, 1))
    ones = jnp.ones_like(pieces)

    def lanes(per_head):
        per_pair = per_head.reshape(bsz, seq, FOX_PAIRS, 2 * FOX_BIAS_LANES)
        per_pair = jnp.pad(per_pair, ((0, 0), (0, 0), (0, 0), (0, LANES - 2 * FOX_BIAS_LANES)))
        return per_pair.reshape(bsz, seq, FOX_PAIRS * LANES).astype(BF16)

    return (lanes(jnp.concatenate([pieces, ones], axis=-1)),
            lanes(jnp.concatenate([ones, -pieces], axis=-1)))


def _merge_kernel(x_ref, ys_ref, ym_ref, yf_ref, gate_ref, g1_ref, ws_ref, wm_ref, wf_ref,
                  wo_ref, o_ref):
    d = x_ref.shape[-1]
    merged = (_sigmoid(gate_ref[0, :, :d]) * _dot(ys_ref[0].astype(BF16), ws_ref[...])
              + _sigmoid(gate_ref[0, :, d:2 * d]) * _dot(ym_ref[0].astype(BF16), wm_ref[...])
              + _sigmoid(gate_ref[0, :, 2 * d:]) * _dot(yf_ref[0].astype(BF16), wf_ref[...]))
    o_ref[0] = x_ref[0] + g1_ref[0] * _dot(merged.astype(BF16), wo_ref[...])


def _merge(x, y_s5, y_m, y_f, gates, g1, w_up_s5, w_up_m, w_up_f, w_out, tm):
    bsz, seq, d = x.shape
    tok = lambda b, i: (b, i, 0)
    acts = (x, y_s5, y_m, y_f, gates)
    weights = (w_up_s5, w_up_m, w_up_f, w_out)
    return pl.pallas_call(
        _merge_kernel,
        grid=(bsz, seq // tm),
        in_specs=[pl.BlockSpec((1, tm, a.shape[-1]), tok) for a in acts]
        + [pl.BlockSpec((1, 1, d), lambda b, i: (b, 0, 0))]
        + [_const_spec(w.shape) for w in weights],
        out_specs=pl.BlockSpec((1, tm, d), tok),
        out_shape=jax.ShapeDtypeStruct(x.shape, F32),
        compiler_params=_params(("parallel", "parallel")),
        name="merge_output_projection",
    )(*acts, g1, *weights)


FFN_COLS = 256


def _ffn_kernel(x_ref, sh_ref, sc_ref, g2_ref, gain_ref, w1_ref, w3_ref, w2_ref, fg_ref, o_ref,
                *, final_norm):
    x = x_ref[0]
    h = _rms_modulate(x, gain_ref[...], sc_ref[0], sh_ref[0]).astype(BF16)
    starts = list(range(0, w1_ref.shape[1], FFN_COLS))

    def up(c0):
        return _dot(h, w1_ref[:, c0:c0 + FFN_COLS]), _dot(h, w3_ref[:, c0:c0 + FFN_COLS])

    acc = jnp.zeros(x.shape, F32)
    nxt = up(starts[0])
    for i, c0 in enumerate(starts):
        a, b = nxt
        if i + 1 < len(starts):
            nxt = up(starts[i + 1])
        acc = acc + _dot((_silu(a) * b).astype(BF16), w2_ref[c0:c0 + FFN_COLS, :])
    y = x + g2_ref[0] * acc
    if final_norm:
        y = (y * lax.rsqrt(jnp.mean(y * y, axis=-1, keepdims=True) + NORM_EPS)) * fg_ref[...]
    o_ref[0] = y


def _ffn(x, shift, scale, g2, gain, w1, w3, w2, final_g, final_norm, tm):
    bsz, seq, d = x.shape
    tok = lambda b, i: (b, i, 0)
    vec = pl.BlockSpec((1, 1, d), lambda b, i: (b, 0, 0))
    return pl.pallas_call(
        functools.partial(_ffn_kernel, final_norm=final_norm),
        grid=(bsz, seq // tm),
        in_specs=[pl.BlockSpec((1, tm, d), tok), vec, vec, vec]
        + [_const_spec(a.shape) for a in (gain, w1, w3, w2, final_g)],
        out_specs=pl.BlockSpec((1, tm, d), tok),
        out_shape=jax.ShapeDtypeStruct(x.shape, F32),
        compiler_params=_params(("parallel", "parallel")),
        name="norm_swiglu",
    )(x, shift, scale, g2, gain, w1, w3, w2, final_g)


def _pad_heads(w, heads, head_dim):
    lead = w.shape[:-1]
    w = w.reshape(lead + (heads, head_dim))
    w = jnp.pad(w, [(0, 0)] * len(lead) + [(0, 0), (0, LANES - head_dim)])
    return w.reshape(lead + (heads * LANES,))


def _layer_weights(w_in, b_in):
    sizes = (S5_WIDTH, MLSTM_WIDTH, MLSTM_WIDTH, MLSTM_WIDTH, MLSTM_HEADS, MLSTM_HEADS,
             FOX_WIDTH, FOX_WIDTH, FOX_WIDTH, FOX_HEADS)
    offs = [0]
    for n in sizes:
        offs.append(offs[-1] + n)
    both = jnp.concatenate([w_in, b_in[None, :]], axis=0)
    part = [both[:, offs[i]:offs[i + 1]] for i in range(len(sizes))] + [both[:, offs[-1]:]]
    pad_m = lambda w: _pad_heads(w, MLSTM_HEADS, MLSTM_HEAD_DIM)
    cat = jnp.concatenate([part[0], pad_m(part[1]), pad_m(part[2]), pad_m(part[3]),
                           part[6], part[7], part[10]], axis=1)
    small = jnp.concatenate([part[4], part[5], part[9]], axis=1)
    small = jnp.pad(small, ((0, 0), (0, GATE_ROWS - small.shape[1])))
    cat_t = jnp.concatenate([small, part[8]], axis=1).T
    return (cat[:-1].astype(BF16), cat[-1:], cat_t[:, :-1].astype(BF16), cat_t[:, -1:])


def kernel(x, c, mod_w, mod_b, norm1_g, norm2_g, w_in, b_in, s5_a_re, s5_a_im, s5_log_dt, s5_b_re, s5_b_im, s5_c_re, s5_c_im, s5_d, s5_w_glu, mlstm_conv_w, mlstm_conv_b, mlstm_wq, mlstm_wk, mlstm_norm_g, w_up_s5, w_up_mlstm, w_up_fox, w_out, ffn_w1, ffn_w3, ffn_w2, final_g):
    bsz, seq, d = x.shape
    depth = mod_w.shape[0]
    assert seq % MLSTM_CHUNK == 0
    tm = _row_tile(seq, 512)
    tq = _row_tile(seq, 512)
    ts = _row_tile(seq, 512)

    mod = _modulation(c, mod_w, mod_b)
    final_gain = final_g.reshape(1, d)
    for l in range(depth):
        sh1, sc1, g1, sh2, sc2, g2 = [mod[l, :, i * d:(i + 1) * d].reshape(bsz, 1, d)
                                      for i in range(6)]
        w_cat, b_cat, w_t, b_t = _layer_weights(w_in[l], b_in[l])
        (s5_u, m_qk, m_v, m_o, f_q, f_k, gates, gate_pre, f_vt) = _input_projection(
            x, sh1, sc1, norm1_g[l].reshape(1, d), w_cat, b_cat, w_t, b_t, tm)

        gate_rows, gate_split = _gate_scan(gate_pre)
        gate_cols = jnp.swapaxes(gate_rows, 1, 2)
        f_qb, f_kb = _fox_bias_lanes(gate_split)

        bmat, cmat, astep, apow = _s5_tables(s5_a_re[l], s5_a_im[l], s5_log_dt[l], s5_b_re[l],
                                             s5_b_im[l], s5_c_re[l], s5_c_im[l])
        y_s5 = _s5(s5_u, bmat, cmat, astep, apow, s5_d[l].reshape(1, -1),
                   s5_w_glu[l].astype(BF16), ts)

        pad_hh = ((0, 0), (0, LANES - MLSTM_HEAD_DIM), (0, LANES - MLSTM_HEAD_DIM))
        y_m = _mlstm(m_qk, m_v, m_o, gate_rows, gate_cols,
                     _pad_heads(mlstm_conv_w[l], MLSTM_HEADS, MLSTM_HEAD_DIM),
                     _pad_heads(mlstm_conv_b[l].reshape(1, -1), MLSTM_HEADS, MLSTM_HEAD_DIM),
                     jnp.pad(mlstm_wq[l], pad_hh).astype(BF16),
                     jnp.pad(mlstm_wk[l], pad_hh).astype(BF16),
                     _pad_heads(mlstm_norm_g[l].reshape(1, -1), MLSTM_HEADS, MLSTM_HEAD_DIM))

        y_f = _fox(f_q, f_qb, f_k, f_kb, f_vt, tq)

        w_up_m = _pad_heads(w_up_mlstm[l].T, MLSTM_HEADS, MLSTM_HEAD_DIM).T
        x = _merge(x, y_s5, y_m, y_f, gates, g1, w_up_s5[l].astype(BF16), w_up_m.astype(BF16),
                   w_up_fox[l].astype(BF16), w_out[l].astype(BF16), tm)
        x = _ffn(x, sh2, sc2, g2, norm2_g[l].reshape(1, d), ffn_w1[l].astype(BF16),
                 ffn_w3[l].astype(BF16), ffn_w2[l].astype(BF16), final_gain,
                 l == depth - 1, tm)
    return x
```

```python
import functools
import math

import jax
import jax.numpy as jnp
from jax import lax
from jax.experimental import pallas as pl
from jax.experimental.pallas import tpu as pltpu

F32 = jnp.float32
BF16 = jnp.bfloat16

LANES = 128
SUBLANES = 8
VMEM_LIMIT_BYTES = 56 * 1024 * 1024

NORM_EPS = 1e-6
S5_GROUPS = 16
S5_GROUP_DIM = 16
S5_STATE = 64
S5_WIDTH = S5_GROUPS * S5_GROUP_DIM
S5_CHANNELS = S5_GROUPS * S5_STATE
MLSTM_HEADS = 4
MLSTM_HEAD_DIM = 96
MLSTM_WIDTH = MLSTM_HEADS * MLSTM_HEAD_DIM
MLSTM_PAD_WIDTH = MLSTM_HEADS * LANES
MLSTM_CHUNK = 128
CONV_TAPS = 4
FOX_HEADS = 6
FOX_HEAD_DIM = 64
FOX_WIDTH = FOX_HEADS * FOX_HEAD_DIM
FOX_PAIRS = FOX_WIDTH // LANES
GATE_ROWS = 16
NEG_BIG = -1e30
LOG2_E = math.log2(math.e)

NT_DIMS = (((1,), (1,)), ((), ()))


def _dot(a, b):
    return jnp.dot(a, b, preferred_element_type=F32)


def _dot_nt(a, b):
    return lax.dot_general(a, b, NT_DIMS, preferred_element_type=F32)


def _sigmoid(x):
    return 1.0 / (1.0 + jnp.exp(-x))


def _silu(x):
    return x * _sigmoid(x)


def _log_sigmoid(x):
    return jnp.minimum(x, 0.0) - jnp.log1p(jnp.exp(-jnp.abs(x)))


def _rms_modulate(x, gain, scale, shift):
    y = x * lax.rsqrt(jnp.mean(x * x, axis=-1, keepdims=True) + NORM_EPS)
    return (y * gain) * (1.0 + scale) + shift


def _params(semantics):
    return pltpu.CompilerParams(dimension_semantics=semantics, vmem_limit_bytes=VMEM_LIMIT_BYTES)


def _const_spec(shape):
    nd = len(shape)
    return pl.BlockSpec(shape, lambda *_: (0,) * nd, pipeline_mode=pl.Buffered(1))


def _row_tile(seq, want):
    t = min(want, seq)
    assert seq % t == 0 and t % SUBLANES == 0
    return t


def _mod_kernel(c_ref, w_ref, b_ref, o_ref):
    cf = _silu(c_ref[...]).astype(BF16)
    o_ref[0] = _dot(cf, w_ref[0].astype(BF16)) + b_ref[0]


def _modulation(c, mod_w, mod_b):
    depth, d, six_d = mod_w.shape
    bsz = c.shape[0]
    n_col = six_d // d
    return pl.pallas_call(
        _mod_kernel,
        grid=(depth, n_col),
        in_specs=[pl.BlockSpec((bsz, d), lambda l, j: (0, 0)),
                  pl.BlockSpec((1, d, d), lambda l, j: (l, 0, j)),
                  pl.BlockSpec((1, 1, d), lambda l, j: (l, 0, j))],
        out_specs=pl.BlockSpec((1, bsz, d), lambda l, j: (l, 0, j)),
        out_shape=jax.ShapeDtypeStruct((depth, bsz, six_d), F32),
        compiler_params=_params(("parallel", "parallel")),
        name="adaln_modulation",
    )(c, mod_w, mod_b.reshape(depth, 1, six_d))


IN_SEGMENTS = ((S5_WIDTH, F32), (MLSTM_PAD_WIDTH, F32), (MLSTM_PAD_WIDTH, F32),
               (FOX_WIDTH, F32), (FOX_WIDTH, BF16))
IN_T_SEGMENTS = ((GATE_ROWS, F32), (FOX_WIDTH, BF16), (MLSTM_PAD_WIDTH, F32))
DOT_COLS = 512


def _inproj_kernel(x_ref, sh_ref, sc_ref, g_ref, w_ref, b_ref, wt_ref, bt_ref, *out_refs):
    h = _rms_modulate(x_ref[0], g_ref[...], sc_ref[0], sh_ref[0]).astype(BF16)
    n_t = len(IN_T_SEGMENTS)
    start = 0
    for o_ref in out_refs[:-n_t]:
        width = o_ref.shape[-1]
        for c0 in range(0, width, DOT_COLS):
            cw = min(DOT_COLS, width - c0)
            z = _dot(h, w_ref[:, start + c0:start + c0 + cw]) + b_ref[:, start + c0:start + c0 + cw]
            o_ref[0, :, c0:c0 + cw] = z.astype(o_ref.dtype)
        start += width
    start = 0
    for o_ref in out_refs[-n_t:]:
        rows = o_ref.shape[1]
        z = _dot_nt(wt_ref[start:start + rows, :], h) + bt_ref[start:start + rows, :]
        o_ref[0] = z.astype(o_ref.dtype)
        start += rows


def _input_projection(x, shift, scale, gain, w_cat, b_cat, w_t, b_t, tm):
    bsz, seq, d = x.shape
    segs = IN_SEGMENTS + ((w_cat.shape[1] - sum(w for w, _ in IN_SEGMENTS), F32),)
    tok = lambda b, i: (b, i, 0)
    vec = lambda b, i: (b, 0, 0)
    out_shape = [jax.ShapeDtypeStruct((bsz, seq, w), dt) for w, dt in segs]
    out_shape += [jax.ShapeDtypeStruct((bsz, r, seq), dt) for r, dt in IN_T_SEGMENTS]
    out_specs = [pl.BlockSpec((1, tm, w), tok) for w, _ in segs]
    out_specs += [pl.BlockSpec((1, r, tm), lambda b, i: (b, 0, i)) for r, _ in IN_T_SEGMENTS]
    return pl.pallas_call(
        _inproj_kernel,
        grid=(bsz, seq // tm),
        in_specs=[pl.BlockSpec((1, tm, d), tok),
                  pl.BlockSpec((1, 1, d), vec), pl.BlockSpec((1, 1, d), vec),
                  _const_spec(gain.shape), _const_spec(w_cat.shape), _const_spec(b_cat.shape),
                  _const_spec(w_t.shape), _const_spec(b_t.shape)],
        out_specs=out_specs,
        out_shape=out_shape,
        compiler_params=_params(("parallel", "parallel")),
        name="norm_input_projection",
    )(x, shift, scale, gain, w_cat, b_cat, w_t, b_t)


def _bf16_round(x):
    return x.astype(BF16).astype(F32)


def _gate_scan_kernel(x_ref, o_ref, split_ref):
    x = x_ref[0]
    seq = x.shape[1]
    seg = MLSTM_CHUNK
    lane = lax.broadcasted_iota(jnp.int32, x.shape, 1) & (seg - 1)
    v = _log_sigmoid(x)
    k = 1
    while k < seg:
        v = v + jnp.where(lane >= k, pltpu.roll(v, k, 1), 0.0)
        k *= 2
    row = lax.broadcasted_iota(jnp.int32, (GATE_ROWS, seg), 0)
    carry = jnp.zeros((GATE_ROWS, 1), F32)
    for j in range(seq // seg):
        sl = slice(j * seg, (j + 1) * seg)
        local = v[:, sl]
        total = local + carry
        o_ref[0, :, sl] = jnp.where(row < MLSTM_HEADS, x[:, sl],
                                    jnp.where(row < 2 * MLSTM_HEADS, local, total))
        scaled = total * LOG2_E
        hi = _bf16_round(scaled)
        mid = _bf16_round(scaled - hi)
        split_ref[0, 0, :, sl] = hi
        split_ref[0, 1, :, sl] = mid
        split_ref[0, 2, :, sl] = _bf16_round((scaled - hi) - mid)
        carry = carry + local[:, seg - 1:seg]


def _gate_scan(pre):
    bsz, rows, seq = pre.shape
    spec = pl.BlockSpec((1, rows, seq), lambda b: (b, 0, 0))
    return pl.pallas_call(
        _gate_scan_kernel, grid=(bsz,), in_specs=[spec],
        out_specs=[spec, pl.BlockSpec((1, 3, rows, seq), lambda b: (b, 0, 0, 0))],
        out_shape=[jax.ShapeDtypeStruct(pre.shape, F32),
                   jax.ShapeDtypeStruct((bsz, 3, rows, seq), F32)],
        compiler_params=_params(("parallel",)),
        name="gate_scan",
    )(pre)


SCAN_SHIFTS = (1, 2, 4)


def _s5_kernel(u_ref, bmat_ref, cmat_ref, astep_ref, apow_ref, d_ref, wglu_ref, o_ref,
               st_ref, carry_ref):
    nch = S5_CHANNELS

    @pl.when(pl.program_id(1) == 0)
    def _():
        carry_ref[...] = jnp.zeros_like(carry_ref)

    u = u_ref[0]
    st_ref[...] = _dot(u.astype(BF16), bmat_ref[...])
    n_blocks = u.shape[0] // SUBLANES

    def body(i, carry):
        cr, ci = carry
        r0 = pl.multiple_of(i * SUBLANES, SUBLANES)
        xr = st_ref[pl.ds(r0, SUBLANES), :nch]
        xi = st_ref[pl.ds(r0, SUBLANES), nch:]
        for k, shift in enumerate(SCAN_SHIFTS):
            ar, ai = astep_ref[k, :, :nch], astep_ref[k, :, nch:]
            sr, si = pltpu.roll(xr, shift, 0), pltpu.roll(xi, shift, 0)
            xr, xi = xr + (ar * sr - ai * si), xi + (ar * si + ai * sr)
        pr, pi = apow_ref[:, :nch], apow_ref[:, nch:]
        xr, xi = xr + (pr * cr - pi * ci), xi + (pr * ci + pi * cr)
        st_ref[pl.ds(r0, SUBLANES), :nch] = xr
        st_ref[pl.ds(r0, SUBLANES), nch:] = xi
        last = SUBLANES - 1
        return (jnp.broadcast_to(xr[last:, :], xr.shape), jnp.broadcast_to(xi[last:, :], xi.shape))

    cr, ci = lax.fori_loop(0, n_blocks, body, (carry_ref[:, :nch], carry_ref[:, nch:]))
    carry_ref[:, :nch] = cr
    carry_ref[:, nch:] = ci

    y = _dot(st_ref[...].astype(BF16), cmat_ref[...]) + d_ref[...] * u
    y = y * (0.5 * (1.0 + jnp.tanh(math.sqrt(2.0 / math.pi) * (y + 0.044715 * (y * y * y)))))
    o_ref[0] = y * _sigmoid(_dot(y.astype(BF16), wglu_ref[...]))


def _s5(u, bmat, cmat, astep, apow, d_skip, w_glu, tb):
    bsz, seq, width = u.shape
    tok = lambda b, i: (b, i, 0)
    return pl.pallas_call(
        _s5_kernel,
        grid=(bsz, seq // tb),
        in_specs=[pl.BlockSpec((1, tb, width), tok)] + [
            _const_spec(a.shape) for a in (bmat, cmat, astep, apow, d_skip, w_glu)],
        out_specs=pl.BlockSpec((1, tb, width), tok),
        out_shape=jax.ShapeDtypeStruct(u.shape, F32),
        scratch_shapes=[pltpu.VMEM((tb, 2 * S5_CHANNELS), F32),
                        pltpu.VMEM((SUBLANES, 2 * S5_CHANNELS), F32)],
        compiler_params=_params(("parallel", "arbitrary")),
        name="s5_scan",
    )(u, bmat, cmat, astep, apow, d_skip, w_glu)


def _s5_tables(a_re, a_im, log_dt, b_re, b_im, c_re, c_im):
    g, n, p = b_re.shape
    dt = jnp.exp(log_dt)[:, None]
    mag = jnp.exp(dt * a_re)
    ang = dt * a_im
    abar_r, abar_i = mag * jnp.cos(ang), mag * jnp.sin(ang)
    den = a_re * a_re + a_im * a_im
    pr, qi = abar_r - 1.0, abar_i
    coef_r = (pr * a_re + qi * a_im) / den
    coef_i = (qi * a_re - pr * a_im) / den
    bbar_r = coef_r[..., None] * b_re - coef_i[..., None] * b_im
    bbar_i = coef_r[..., None] * b_im + coef_i[..., None] * b_re
    eye = jnp.eye(g, dtype=F32)

    def in_block(bb):
        return jnp.einsum('gnp,gh->gphn', bb, eye).reshape(g * p, g * n)

    def out_block(cc):
        return jnp.einsum('gpn,gh->gnhp', cc, eye).reshape(g * n, g * p)

    bmat = jnp.concatenate([in_block(bbar_r), in_block(bbar_i)], axis=1).astype(BF16)
    cmat = jnp.concatenate([out_block(c_re), -out_block(c_im)], axis=0).astype(BF16)

    def power(e):
        e = jnp.asarray(e, F32)[:, None, None]
        m = jnp.exp(e * (dt * a_re)[None])
        th = e * ang[None]
        return (m * jnp.cos(th)).reshape(-1, g * n), (m * jnp.sin(th)).reshape(-1, g * n)

    rows = jnp.arange(SUBLANES)
    sr, si = power(SCAN_SHIFTS)
    keep = (rows[None, :] >= jnp.asarray(SCAN_SHIFTS)[:, None]).astype(F32)[:, :, None]
    astep = jnp.concatenate([keep * sr[:, None, :], keep * si[:, None, :]], axis=-1)
    wr, wi = power(rows + 1)
    apow = jnp.concatenate([wr, wi], axis=-1)
    return bmat, cmat, astep, apow


def _mlstm_kernel(qk_ref, vt_ref, o_ref, grow_ref, gcol_ref, cw_ref, cb_ref, wq_ref, wk_ref,
                  ng_ref, out_ref, tail_ref, c_ref, n_ref, m_ref):
    chunk = qk_ref.shape[1]
    nh = MLSTM_HEADS

    @pl.when(pl.program_id(1) == 0)
    def _():
        tail_ref[...] = jnp.zeros_like(tail_ref)
        c_ref[...] = jnp.zeros_like(c_ref)
        n_ref[...] = jnp.zeros_like(n_ref)
        m_ref[...] = jnp.zeros_like(m_ref)

    x = qk_ref[0]
    ext = jnp.concatenate([tail_ref[...], x], axis=0)
    conv = cb_ref[...] + cw_ref[CONV_TAPS - 1:CONV_TAPS, :] * x
    for tap in range(CONV_TAPS - 1):
        back = CONV_TAPS - 1 - tap
        conv = conv + cw_ref[tap:tap + 1, :] * ext[SUBLANES - back:SUBLANES - back + chunk, :]
    tail_ref[...] = x[chunk - SUBLANES:, :]
    cx = _silu(conv)

    key_id = lax.broadcasted_iota(jnp.int32, (chunk, chunk), 0)
    query_id = lax.broadcasted_iota(jnp.int32, (chunk, chunk), 1)
    causal = key_id <= query_id
    k_scale = MLSTM_HEAD_DIM ** -0.5
    heads = range(nh)
    cols = [slice(hd * LANES, (hd + 1) * LANES) for hd in heads]
    cxh = [cx[:, cols[hd]].astype(BF16) for hd in heads]
    q = [_dot(cxh[hd], wq_ref[hd]).astype(BF16) for hd in heads]
    k = [(_dot(cxh[hd], wk_ref[hd]) * k_scale).astype(BF16) for hd in heads]
    v_t = [vt_ref[0, cols[hd], :] for hd in heads]
    kq = [_dot_nt(k[hd], q[hd]) for hd in heads]
    carry = [_dot_nt(c_ref[hd].astype(BF16), q[hd]) for hd in heads]
    n_q = [_dot_nt(n_ref[...].astype(BF16), q[hd]) for hd in heads]

    s, inter, m_t, stats = [], [], [], []
    for hd in heads:
        li_row = grow_ref[0, hd:hd + 1, :]
        b_row = grow_ref[0, nh + hd:nh + hd + 1, :]
        d_col = gcol_ref[0, :, hd:hd + 1] - gcol_ref[0, :, nh + hd:nh + hd + 1]
        m_st = m_ref[hd:hd + 1, 0:1]
        dmat = jnp.where(causal, b_row + d_col, -jnp.inf)
        m_inter = b_row + m_st
        m_t.append(jnp.maximum(m_inter, jnp.max(dmat, axis=0, keepdims=True)))
        s.append(kq[hd] * jnp.exp(dmat - m_t[hd]))
        inter.append(jnp.exp(m_inter - m_t[hd]))

        b_last = b_row[:, chunk - 1:chunk]
        w_row = b_last - b_row + li_row
        m_new = jnp.maximum(b_last + m_st, jnp.max(w_row, axis=1, keepdims=True))
        stats.append((m_new, jnp.exp(b_last + m_st - m_new), jnp.exp(w_row - m_new)))

    sv = [_dot(v_t[hd].astype(BF16), s[hd].astype(BF16)) for hd in heads]
    vk = [_dot((v_t[hd] * stats[hd][2]).astype(BF16), k[hd]) for hd in heads]
    row8 = lax.broadcasted_iota(jnp.int32, (SUBLANES, chunk), 0)
    ws_rows = jnp.zeros((SUBLANES, chunk), F32)
    for hd in heads:
        ws_rows = jnp.where(row8 == hd, stats[hd][2], ws_rows)
    ws_k = [_dot(ws_rows.astype(BF16), k[hd]) for hd in heads]

    for hd in heads:
        m_new, decay, _ = stats[hd]
        num = sv[hd] + inter[hd] * carry[hd]
        den = jnp.sum(s[hd], axis=0, keepdims=True) + inter[hd] * n_q[hd][hd:hd + 1, :]
        h = num * (1.0 / jnp.maximum(jnp.abs(den), jnp.exp(-m_t[hd])))
        c_ref[hd] = decay * c_ref[hd] + vk[hd]
        n_ref[hd:hd + 1, :] = decay * n_ref[hd:hd + 1, :] + ws_k[hd][hd:hd + 1, :]
        m_ref[hd:hd + 1, :] = jnp.broadcast_to(m_new, (1, LANES))
        hn = h * lax.rsqrt(jnp.sum(h * h, axis=0, keepdims=True) * (1.0 / MLSTM_HEAD_DIM) + NORM_EPS)
        out_ref[0, :, cols[hd]] = _sigmoid(o_ref[0, :, cols[hd]]) * (hn.T * ng_ref[:, cols[hd]])


def _mlstm(qk, v_t, o, gate_rows, gate_cols, conv_w, conv_b, wq, wk, norm_g):
    bsz, seq, width = qk.shape
    chunk = MLSTM_CHUNK
    tok = lambda b, i: (b, i, 0)
    tok_spec = pl.BlockSpec((1, chunk, width), tok)
    return pl.pallas_call(
        _mlstm_kernel,
        grid=(bsz, seq // chunk),
        in_specs=[tok_spec, pl.BlockSpec((1, width, chunk), lambda b, i: (b, 0, i)), tok_spec,
                  pl.BlockSpec((1, GATE_ROWS, chunk), lambda b, i: (b, 0, i)),
                  pl.BlockSpec((1, chunk, GATE_ROWS), tok)] + [
            _const_spec(a.shape) for a in (conv_w, conv_b, wq, wk, norm_g)],
        out_specs=tok_spec,
        out_shape=jax.ShapeDtypeStruct(qk.shape, F32),
        scratch_shapes=[pltpu.VMEM((SUBLANES, width), F32),
                        pltpu.VMEM((MLSTM_HEADS, LANES, LANES), F32),
                        pltpu.VMEM((SUBLANES, LANES), F32),
                        pltpu.VMEM((SUBLANES, LANES), F32)],
        compiler_params=_params(("parallel", "arbitrary")),
        name="mlstm_chunkwise",
    )(qk, v_t, o, gate_rows, gate_cols, conv_w, conv_b, wq, wk, norm_g)


FOX_BIAS_LANES = 6
FOX_QUERY_SPLIT = 1
FOX_SCORE_LOOKAHEAD = 2
FOX_VT_ROWS = FOX_HEAD_DIM + 16


def _fox_kernel(qi_ref, ki_ref, q_ref, qb_ref, k_ref, kb_ref, vt_ref, o_ref,
                qx_ref, m_ref, acc_ref):
    step = pl.program_id(1)
    qi, ki = qi_ref[step], ki_ref[step]
    tq, tk = q_ref.shape[1], k_ref.shape[1]
    hd = FOX_HEAD_DIM

    def pair_lanes(pair):
        return slice(pair * LANES, (pair + 1) * LANES)

    @pl.when(ki == 0)
    def _():
        lane = lax.broadcasted_iota(jnp.int32, (1, LANES), 1)
        for pair in range(FOX_PAIRS):
            q = q_ref[0, :, pair_lanes(pair)] * (hd ** -0.5 * LOG2_E)
            qb = qb_ref[0, :, pair_lanes(pair)]
            for hh in range(2):
                in_head = (lane >= hh * hd) & (lane < (hh + 1) * hd)
                in_bias = (lane >= hh * FOX_BIAS_LANES) & (lane < (hh + 1) * FOX_BIAS_LANES)
                qx_ref[2 * pair + hh, :, :LANES] = jnp.where(in_head, q, 0.0).astype(BF16)
                qx_ref[2 * pair + hh, :, LANES:] = jnp.where(in_bias, qb, jnp.zeros_like(qb))
        m_ref[...] = jnp.full_like(m_ref, NEG_BIG)
        acc_ref[...] = jnp.zeros_like(acc_ref)

    def attend(diagonal):
        kx = [jnp.concatenate([k_ref[0, :, pair_lanes(pair)], kb_ref[0, :, pair_lanes(pair)]], axis=1)
              for pair in range(FOX_PAIRS)]
        tc = tq // FOX_QUERY_SPLIT
        chains = [(hh, c * tc) for hh in range(FOX_HEADS) for c in range(FOX_QUERY_SPLIT)]
        scores, probs = {}, {}

        def score(hh, c0):
            s = _dot_nt(kx[hh // 2], qx_ref[hh, c0:c0 + tc, :])
            if diagonal:
                visible = (lax.broadcasted_iota(jnp.int32, (tk, tc), 0)
                           <= lax.broadcasted_iota(jnp.int32, (tk, tc), 1) + c0)
                s = jnp.where(visible, s, NEG_BIG)
            scores[hh, c0] = s

        def softmax(hh, c0):
            s = scores.pop((hh, c0))
            m_prev = m_ref[hh, :, c0:c0 + tc]
            m_new = jnp.maximum(m_prev, jnp.max(s, axis=0, keepdims=True))
            m_ref[hh, :, c0:c0 + tc] = m_new
            probs[hh, c0] = (jnp.exp2(m_prev - m_new), jnp.exp2(s - m_new).astype(BF16))

        def value(hh, c0):
            alpha, p = probs.pop((hh, c0))
            acc_ref[hh, :, c0:c0 + tc] = alpha * acc_ref[hh, :, c0:c0 + tc] + _dot(vt_ref[0, hh], p)

        n, ahead = len(chains), FOX_SCORE_LOOKAHEAD
        for i in range(n + ahead + 1):
            if i < n:
                score(*chains[i])
            if ahead <= i < n + ahead:
                softmax(*chains[i - ahead])
            if i >= ahead + 1:
                value(*chains[i - ahead - 1])

    @pl.when(ki < qi)
    def _():
        attend(False)

    @pl.when(ki == qi)
    def _():
        attend(True)
        for pair in range(FOX_PAIRS):
            out_t = jnp.concatenate([acc_ref[hh, :hd, :] / acc_ref[hh, hd:hd + 1, :]
                                     for hh in (2 * pair, 2 * pair + 1)], axis=0)
            o_ref[0, :, pair_lanes(pair)] = out_t.T


def _fox(q, qb, k, kb, v_t, tq):
    bsz, seq, width = q.shape
    nq = seq // tq
    q_idx = [i for i in range(nq) for _ in range(i + 1)]
    k_idx = [j for i in range(nq) for j in range(i + 1)]
    q_tbl, k_tbl = jnp.asarray(q_idx, jnp.int32), jnp.asarray(k_idx, jnp.int32)
    q_spec = pl.BlockSpec((1, tq, width), lambda b, s, qt, kt: (b, qt[s], 0))
    k_spec = pl.BlockSpec((1, tq, width), lambda b, s, qt, kt: (b, kt[s], 0))
    grid_spec = pltpu.PrefetchScalarGridSpec(
        num_scalar_prefetch=2,
        grid=(bsz, len(q_idx)),
        in_specs=[q_spec, q_spec, k_spec, k_spec,
                  pl.BlockSpec((1, FOX_HEADS, FOX_VT_ROWS, tq), lambda b, s, qt, kt: (b, 0, 0, kt[s]))],
        out_specs=q_spec,
        scratch_shapes=[pltpu.VMEM((FOX_HEADS, tq, 2 * LANES), BF16),
                        pltpu.VMEM((FOX_HEADS, 1, tq), F32),
                        pltpu.VMEM((FOX_HEADS, FOX_VT_ROWS, tq), F32)],
    )
    v_heads = v_t.reshape(bsz, FOX_HEADS, FOX_HEAD_DIM, seq)
    ones = jnp.ones((bsz, FOX_HEADS, FOX_VT_ROWS - FOX_HEAD_DIM, seq), v_t.dtype)
    v_ext = jnp.concatenate([v_heads, ones], axis=2)
    return pl.pallas_call(
        _fox_kernel, grid_spec=grid_spec,
        out_shape=jax.ShapeDtypeStruct((bsz, seq, width), F32),
        compiler_params=_params(("parallel", "arbitrary")),
        name="forgetting_attention",
    )(q_tbl, k_tbl, q, qb, k, kb, v_ext)


def _fox_bias_lanes(split):
    bsz, _, _, seq = split.shape
    lo = 2 * MLSTM_HEADS
    pieces = jnp.transpose(split[:, :, lo:lo + FOX_HEADS, :], (0, 3, 2, 1))
    ones = jnp.ones_like(pieces)

    def lanes(per_head):
        per_pair = per_head.reshape(bsz, seq, FOX_PAIRS, 2 * FOX_BIAS_LANES)
        per_pair = jnp.pad(per_pair, ((0, 0), (0, 0), (0, 0), (0, LANES - 2 * FOX_BIAS_LANES)))
        return per_pair.reshape(bsz, seq, FOX_PAIRS * LANES).astype(BF16)

    return (lanes(jnp.concatenate([pieces, ones], axis=-1)),
            lanes(jnp.concatenate([ones, -pieces], axis=-1)))


def _merge_kernel(x_ref, ys_ref, ym_ref, yf_ref, gate_ref, g1_ref, ws_ref, wm_ref, wf_ref,
                  wo_ref, o_ref):
    d = x_ref.shape[-1]
    merged = (_sigmoid(gate_ref[0, :, :d]) * _dot(ys_ref[0].astype(BF16), ws_ref[...])
              + _sigmoid(gate_ref[0, :, d:2 * d]) * _dot(ym_ref[0].astype(BF16), wm_ref[...])
              + _sigmoid(gate_ref[0, :, 2 * d:]) * _dot(yf_ref[0].astype(BF16), wf_ref[...]))
    o_ref[0] = x_ref[0] + g1_ref[0] * _dot(merged.astype(BF16), wo_ref[...])


def _merge(x, y_s5, y_m, y_f, gates, g1, w_up_s5, w_up_m, w_up_f, w_out, tm):
    bsz, seq, d = x.shape
    tok = lambda b, i: (b, i, 0)
    acts = (x, y_s5, y_m, y_f, gates)
    weights = (w_up_s5, w_up_m, w_up_f, w_out)
    return pl.pallas_call(
        _merge_kernel,
        grid=(bsz, seq // tm),
        in_specs=[pl.BlockSpec((1, tm, a.shape[-1]), tok) for a in acts]
        + [pl.BlockSpec((1, 1, d), lambda b, i: (b, 0, 0))]
        + [_const_spec(w.shape) for w in weights],
        out_specs=pl.BlockSpec((1, tm, d), tok),
        out_shape=jax.ShapeDtypeStruct(x.shape, F32),
        compiler_params=_params(("parallel", "parallel")),
        name="merge_output_projection",
    )(*acts, g1, *weights)


FFN_COLS = 256


def _ffn_kernel(x_ref, sh_ref, sc_ref, g2_ref, gain_ref, w1_ref, w3_ref, w2_ref, fg_ref, o_ref,
                *, final_norm):
    x = x_ref[0]
    h = _rms_modulate(x, gain_ref[...], sc_ref[0], sh_ref[0]).astype(BF16)
    starts = list(range(0, w1_ref.shape[1], FFN_COLS))

    def up(c0):
        return _dot(h, w1_ref[:, c0:c0 + FFN_COLS]), _dot(h, w3_ref[:, c0:c0 + FFN_COLS])

    acc = jnp.zeros(x.shape, F32)
    nxt = up(starts[0])
    for i, c0 in enumerate(starts):
        a, b = nxt
        if i + 1 < len(starts):
            nxt = up(starts[i + 1])
        acc = acc + _dot((_silu(a) * b).astype(BF16), w2_ref[c0:c0 + FFN_COLS, :])
    y = x + g2_ref[0] * acc
    if final_norm:
        y = (y * lax.rsqrt(jnp.mean(y * y, axis=-1, keepdims=True) + NORM_EPS)) * fg_ref[...]
    o_ref[0] = y


def _ffn(x, shift, scale, g2, gain, w1, w3, w2, final_g, final_norm, tm):
    bsz, seq, d = x.shape
    tok = lambda b, i: (b, i, 0)
    vec = pl.BlockSpec((1, 1, d), lambda b, i: (b, 0, 0))
    return pl.pallas_call(
        functools.partial(_ffn_kernel, final_norm=final_norm),
        grid=(bsz, seq // tm),
        in_specs=[pl.BlockSpec((1, tm, d), tok), vec, vec, vec]
        + [_const_spec(a.shape) for a in (gain, w1, w3, w2, final_g)],
        out_specs=pl.BlockSpec((1, tm, d), tok),
        out_shape=jax.ShapeDtypeStruct(x.shape, F32),
        compiler_params=_params(("parallel", "parallel")),
        name="norm_swiglu",
    )(x, shift, scale, g2, gain, w1, w3, w2, final_g)


def _pad_heads(w, heads, head_dim):
    lead = w.shape[:-1]
    w = w.reshape(lead + (heads, head_dim))
    w = jnp.pad(w, [(0, 0)] * len(lead) + [(0, 0), (0, LANES - head_dim)])
    return w.reshape(lead + (heads * LANES,))


def _layer_weights(w_in, b_in):
    sizes = (S5_WIDTH, MLSTM_WIDTH, MLSTM_WIDTH, MLSTM_WIDTH, MLSTM_HEADS, MLSTM_HEADS,
             FOX_WIDTH, FOX_WIDTH, FOX_WIDTH, FOX_HEADS)
    offs = [0]
    for n in sizes:
        offs.append(offs[-1] + n)
    both = jnp.concatenate([w_in, b_in[None, :]], axis=0)
    part = [both[:, offs[i]:offs[i + 1]] for i in range(len(sizes))] + [both[:, offs[-1]:]]
    pad_m = lambda w: _pad_heads(w, MLSTM_HEADS, MLSTM_HEAD_DIM)
    cat = jnp.concatenate([part[0], pad_m(part[1]), pad_m(part[3]),
                           part[6], part[7], part[10]], axis=1)
    small = jnp.concatenate([part[4], part[5], part[9]], axis=1)
    small = jnp.pad(small, ((0, 0), (0, GATE_ROWS - small.shape[1])))
    cat_t = jnp.concatenate([small, part[8], pad_m(part[2])], axis=1).T
    return (cat[:-1].astype(BF16), cat[-1:], cat_t[:, :-1].astype(BF16), cat_t[:, -1:])


def kernel(x, c, mod_w, mod_b, norm1_g, norm2_g, w_in, b_in, s5_a_re, s5_a_im, s5_log_dt, s5_b_re, s5_b_im, s5_c_re, s5_c_im, s5_d, s5_w_glu, mlstm_conv_w, mlstm_conv_b, mlstm_wq, mlstm_wk, mlstm_norm_g, w_up_s5, w_up_mlstm, w_up_fox, w_out, ffn_w1, ffn_w3, ffn_w2, final_g):
    bsz, seq, d = x.shape
    depth = mod_w.shape[0]
    assert seq % MLSTM_CHUNK == 0
    tm = _row_tile(seq, 512)
    tq = _row_tile(seq, 512)
    ts = _row_tile(seq, 512)

    mod = _modulation(c, mod_w, mod_b)
    final_gain = final_g.reshape(1, d)
    for l in range(depth):
        sh1, sc1, g1, sh2, sc2, g2 = [mod[l, :, i * d:(i + 1) * d].reshape(bsz, 1, d)
                                      for i in range(6)]
        w_cat, b_cat, w_t, b_t = _layer_weights(w_in[l], b_in[l])
        (s5_u, m_qk, m_o, f_q, f_k, gates, gate_pre, f_vt, m_vt) = _input_projection(
            x, sh1, sc1, norm1_g[l].reshape(1, d), w_cat, b_cat, w_t, b_t, tm)

        gate_rows, gate_split = _gate_scan(gate_pre)
        gate_cols = jnp.swapaxes(gate_rows, 1, 2)
        f_qb, f_kb = _fox_bias_lanes(gate_split)

        bmat, cmat, astep, apow = _s5_tables(s5_a_re[l], s5_a_im[l], s5_log_dt[l], s5_b_re[l],
                                             s5_b_im[l], s5_c_re[l], s5_c_im[l])
        y_s5 = _s5(s5_u, bmat, cmat, astep, apow, s5_d[l].reshape(1, -1),
                   s5_w_glu[l].astype(BF16), ts)

        pad_hh = ((0, 0), (0, LANES - MLSTM_HEAD_DIM), (0, LANES - MLSTM_HEAD_DIM))
        y_m = _mlstm(m_qk, m_vt, m_o, gate_rows, gate_cols,
                     _pad_heads(mlstm_conv_w[l], MLSTM_HEADS, MLSTM_HEAD_DIM),
                     _pad_heads(mlstm_conv_b[l].reshape(1, -1), MLSTM_HEADS, MLSTM_HEAD_DIM),
                     jnp.pad(mlstm_wq[l], pad_hh).astype(BF16),
                     jnp.pad(mlstm_wk[l], pad_hh).astype(BF16),
                     _pad_heads(mlstm_norm_g[l].reshape(1, -1), MLSTM_HEADS, MLSTM_HEAD_DIM))

        y_f = _fox(f_q, f_qb, f_k, f_kb, f_vt, tq)

        w_up_m = _pad_heads(w_up_mlstm[l].T, MLSTM_HEADS, MLSTM_HEAD_DIM).T
        x = _merge(x, y_s5, y_m, y_f, gates, g1, w_up_s5[l].astype(BF16), w_up_m.astype(BF16),
                   w_up_fox[l].astype(BF16), w_out[l].astype(BF16), tm)
        x = _ffn(x, sh2, sc2, g2, norm2_g[l].reshape(1, d), ffn_w1[l].astype(BF16),
                 ffn_w3[l].astype(BF16), ffn_w2[l].astype(BF16), final_gain,
                 l == depth - 1, tm)
    return x
```

```python
import functools
import math

import jax
import jax.numpy as jnp
from jax import lax
from jax.experimental import pallas as pl
from jax.experimental.pallas import tpu as pltpu

F32 = jnp.float32
BF16 = jnp.bfloat16

LANES = 128
SUBLANES = 8
VMEM_LIMIT_BYTES = 56 * 1024 * 1024

NORM_EPS = 1e-6
S5_GROUPS = 16
S5_GROUP_DIM = 16
S5_STATE = 64
S5_WIDTH = S5_GROUPS * S5_GROUP_DIM
S5_CHANNELS = S5_GROUPS * S5_STATE
MLSTM_HEADS = 4
MLSTM_HEAD_DIM = 96
MLSTM_WIDTH = MLSTM_HEADS * MLSTM_HEAD_DIM
MLSTM_PAD_WIDTH = MLSTM_HEADS * LANES
MLSTM_CHUNK = 128
CONV_TAPS = 4
FOX_HEADS = 6
FOX_HEAD_DIM = 64
FOX_WIDTH = FOX_HEADS * FOX_HEAD_DIM
FOX_PAIRS = FOX_WIDTH // LANES
GATE_ROWS = 16
NEG_BIG = -1e30
LOG2_E = math.log2(math.e)

NT_DIMS = (((1,), (1,)), ((), ()))


def _dot(a, b):
    return jnp.dot(a, b, preferred_element_type=F32)


def _dot_nt(a, b):
    return lax.dot_general(a, b, NT_DIMS, preferred_element_type=F32)


def _sigmoid(x):
    return 1.0 / (1.0 + jnp.exp(-x))


def _silu(x):
    return x * _sigmoid(x)


def _log_sigmoid(x):
    return jnp.minimum(x, 0.0) - jnp.log1p(jnp.exp(-jnp.abs(x)))


def _rms_modulate(x, gain, scale, shift):
    y = x * lax.rsqrt(jnp.mean(x * x, axis=-1, keepdims=True) + NORM_EPS)
    return (y * gain) * (1.0 + scale) + shift


def _params(semantics):
    return pltpu.CompilerParams(dimension_semantics=semantics, vmem_limit_bytes=VMEM_LIMIT_BYTES)


def _const_spec(shape):
    nd = len(shape)
    return pl.BlockSpec(shape, lambda *_: (0,) * nd, pipeline_mode=pl.Buffered(1))


def _row_tile(seq, want):
    t = min(want, seq)
    assert seq % t == 0 and t % SUBLANES == 0
    return t


def _mod_kernel(c_ref, w_ref, b_ref, o_ref):
    cf = _silu(c_ref[...]).astype(BF16)
    o_ref[0] = _dot(cf, w_ref[0].astype(BF16)) + b_ref[0]


def _modulation(c, mod_w, mod_b):
    depth, d, six_d = mod_w.shape
    bsz = c.shape[0]
    n_col = six_d // d
    return pl.pallas_call(
        _mod_kernel,
        grid=(depth, n_col),
        in_specs=[pl.BlockSpec((bsz, d), lambda l, j: (0, 0)),
                  pl.BlockSpec((1, d, d), lambda l, j: (l, 0, j)),
                  pl.BlockSpec((1, 1, d), lambda l, j: (l, 0, j))],
        out_specs=pl.BlockSpec((1, bsz, d), lambda l, j: (l, 0, j)),
        out_shape=jax.ShapeDtypeStruct((depth, bsz, six_d), F32),
        compiler_params=_params(("parallel", "parallel")),
        name="adaln_modulation",
    )(c, mod_w, mod_b.reshape(depth, 1, six_d))


IN_SEGMENTS = ((S5_WIDTH, F32), (MLSTM_PAD_WIDTH, F32), (MLSTM_PAD_WIDTH, F32),
               (FOX_WIDTH, F32), (FOX_WIDTH, BF16))
IN_T_SEGMENTS = ((GATE_ROWS, F32), (FOX_WIDTH, BF16), (MLSTM_PAD_WIDTH, F32))
DOT_COLS = 512


def _inproj_kernel(x_ref, sh_ref, sc_ref, g_ref, w_ref, b_ref, wt_ref, bt_ref, *out_refs):
    h = _rms_modulate(x_ref[0], g_ref[...], sc_ref[0], sh_ref[0]).astype(BF16)
    n_t = len(IN_T_SEGMENTS)
    start = 0
    for o_ref in out_refs[:-n_t]:
        width = o_ref.shape[-1]
        for c0 in range(0, width, DOT_COLS):
            cw = min(DOT_COLS, width - c0)
            z = _dot(h, w_ref[:, start + c0:start + c0 + cw]) + b_ref[:, start + c0:start + c0 + cw]
            o_ref[0, :, c0:c0 + cw] = z.astype(o_ref.dtype)
        start += width
    start = 0
    for o_ref in out_refs[-n_t:]:
        rows = o_ref.shape[1]
        z = _dot_nt(wt_ref[start:start + rows, :], h) + bt_ref[start:start + rows, :]
        o_ref[0] = z.astype(o_ref.dtype)
        start += rows


def _input_projection(x, shift, scale, gain, w_cat, b_cat, w_t, b_t, tm):
    bsz, seq, d = x.shape
    segs = IN_SEGMENTS + ((w_cat.shape[1] - sum(w for w, _ in IN_SEGMENTS), F32),)
    tok = lambda b, i: (b, i, 0)
    vec = lambda b, i: (b, 0, 0)
    out_shape = [jax.ShapeDtypeStruct((bsz, seq, w), dt) for w, dt in segs]
    out_shape += [jax.ShapeDtypeStruct((bsz, r, seq), dt) for r, dt in IN_T_SEGMENTS]
    out_specs = [pl.BlockSpec((1, tm, w), tok) for w, _ in segs]
    out_specs += [pl.BlockSpec((1, r, tm), lambda b, i: (b, 0, i)) for r, _ in IN_T_SEGMENTS]
    return pl.pallas_call(
        _inproj_kernel,
        grid=(bsz, seq // tm),
        in_specs=[pl.BlockSpec((1, tm, d), tok),
                  pl.BlockSpec((1, 1, d), vec), pl.BlockSpec((1, 1, d), vec),
                  _const_spec(gain.shape), _const_spec(w_cat.shape), _const_spec(b_cat.shape),
                  _const_spec(w_t.shape), _const_spec(b_t.shape)],
        out_specs=out_specs,
        out_shape=out_shape,
        compiler_params=_params(("parallel", "parallel")),
        name="norm_input_projection",
    )(x, shift, scale, gain, w_cat, b_cat, w_t, b_t)


def _bf16_round(x):
    return x.astype(BF16).astype(F32)


def _gate_scan_kernel(x_ref, o_ref, split_ref):
    x = x_ref[0]
    seq = x.shape[1]
    seg = MLSTM_CHUNK
    lane = lax.broadcasted_iota(jnp.int32, x.shape, 1) & (seg - 1)
    v = _log_sigmoid(x)
    k = 1
    while k < seg:
        v = v + jnp.where(lane >= k, pltpu.roll(v, k, 1), 0.0)
        k *= 2
    row = lax.broadcasted_iota(jnp.int32, (GATE_ROWS, seg), 0)
    carry = jnp.zeros((GATE_ROWS, 1), F32)
    for j in range(seq // seg):
        sl = slice(j * seg, (j + 1) * seg)
        local = v[:, sl]
        total = local + carry
        o_ref[0, :, sl] = jnp.where(row < MLSTM_HEADS, x[:, sl],
                                    jnp.where(row < 2 * MLSTM_HEADS, local, total))
        scaled = total * LOG2_E
        hi = _bf16_round(scaled)
        mid = _bf16_round(scaled - hi)
        split_ref[0, 0, :, sl] = hi
        split_ref[0, 1, :, sl] = mid
        split_ref[0, 2, :, sl] = _bf16_round((scaled - hi) - mid)
        carry = carry + local[:, seg - 1:seg]


def _gate_scan(pre):
    bsz, rows, seq = pre.shape
    spec = pl.BlockSpec((1, rows, seq), lambda b: (b, 0, 0))
    return pl.pallas_call(
        _gate_scan_kernel, grid=(bsz,), in_specs=[spec],
        out_specs=[spec, pl.BlockSpec((1, 3, rows, seq), lambda b: (b, 0, 0, 0))],
        out_shape=[jax.ShapeDtypeStruct(pre.shape, F32),
                   jax.ShapeDtypeStruct((bsz, 3, rows, seq), F32)],
        compiler_params=_params(("parallel",)),
        name="gate_scan",
    )(pre)


SCAN_SHIFTS = (1, 2, 4)


def _s5_kernel(u_ref, bmat_ref, cmat_ref, astep_ref, apow_ref, d_ref, wglu_ref, o_ref,
               st_ref, carry_ref):
    nch = S5_CHANNELS

    @pl.when(pl.program_id(1) == 0)
    def _():
        carry_ref[...] = jnp.zeros_like(carry_ref)

    u = u_ref[0]
    st_ref[...] = _dot(u.astype(BF16), bmat_ref[...])
    n_blocks = u.shape[0] // SUBLANES

    def body(i, carry):
        cr, ci = carry
        r0 = pl.multiple_of(i * SUBLANES, SUBLANES)
        xr = st_ref[pl.ds(r0, SUBLANES), :nch]
        xi = st_ref[pl.ds(r0, SUBLANES), nch:]
        for k, shift in enumerate(SCAN_SHIFTS):
            ar, ai = astep_ref[k, :, :nch], astep_ref[k, :, nch:]
            sr, si = pltpu.roll(xr, shift, 0), pltpu.roll(xi, shift, 0)
            xr, xi = xr + (ar * sr - ai * si), xi + (ar * si + ai * sr)
        pr, pi = apow_ref[:, :nch], apow_ref[:, nch:]
        xr, xi = xr + (pr * cr - pi * ci), xi + (pr * ci + pi * cr)
        st_ref[pl.ds(r0, SUBLANES), :nch] = xr
        st_ref[pl.ds(r0, SUBLANES), nch:] = xi
        last = SUBLANES - 1
        return (jnp.broadcast_to(xr[last:, :], xr.shape), jnp.broadcast_to(xi[last:, :], xi.shape))

    cr, ci = lax.fori_loop(0, n_blocks, body, (carry_ref[:, :nch], carry_ref[:, nch:]))
    carry_ref[:, :nch] = cr
    carry_ref[:, nch:] = ci

    y = _dot(st_ref[...].astype(BF16), cmat_ref[...]) + d_ref[...] * u
    y = y * (0.5 * (1.0 + jnp.tanh(math.sqrt(2.0 / math.pi) * (y + 0.044715 * (y * y * y)))))
    o_ref[0] = y * _sigmoid(_dot(y.astype(BF16), wglu_ref[...]))


def _s5(u, bmat, cmat, astep, apow, d_skip, w_glu, tb):
    bsz, seq, width = u.shape
    tok = lambda b, i: (b, i, 0)
    return pl.pallas_call(
        _s5_kernel,
        grid=(bsz, seq // tb),
        in_specs=[pl.BlockSpec((1, tb, width), tok)] + [
            _const_spec(a.shape) for a in (bmat, cmat, astep, apow, d_skip, w_glu)],
        out_specs=pl.BlockSpec((1, tb, width), tok),
        out_shape=jax.ShapeDtypeStruct(u.shape, F32),
        scratch_shapes=[pltpu.VMEM((tb, 2 * S5_CHANNELS), F32),
                        pltpu.VMEM((SUBLANES, 2 * S5_CHANNELS), F32)],
        compiler_params=_params(("parallel", "arbitrary")),
        name="s5_scan",
    )(u, bmat, cmat, astep, apow, d_skip, w_glu)


def _s5_tables(a_re, a_im, log_dt, b_re, b_im, c_re, c_im):
    g, n, p = b_re.shape
    dt = jnp.exp(log_dt)[:, None]
    mag = jnp.exp(dt * a_re)
    ang = dt * a_im
    abar_r, abar_i = mag * jnp.cos(ang), mag * jnp.sin(ang)
    den = a_re * a_re + a_im * a_im
    pr, qi = abar_r - 1.0, abar_i
    coef_r = (pr * a_re + qi * a_im) / den
    coef_i = (qi * a_re - pr * a_im) / den
    bbar_r = coef_r[..., None] * b_re - coef_i[..., None] * b_im
    bbar_i = coef_r[..., None] * b_im + coef_i[..., None] * b_re
    eye = jnp.eye(g, dtype=F32)

    def in_block(bb):
        return jnp.einsum('gnp,gh->gphn', bb, eye).reshape(g * p, g * n)

    def out_block(cc):
        return jnp.einsum('gpn,gh->gnhp', cc, eye).reshape(g * n, g * p)

    bmat = jnp.concatenate([in_block(bbar_r), in_block(bbar_i)], axis=1).astype(BF16)
    cmat = jnp.concatenate([out_block(c_re), -out_block(c_im)], axis=0).astype(BF16)

    def power(e):
        e = jnp.asarray(e, F32)[:, None, None]
        m = jnp.exp(e * (dt * a_re)[None])
        th = e * ang[None]
        return (m * jnp.cos(th)).reshape(-1, g * n), (m * jnp.sin(th)).reshape(-1, g * n)

    rows = jnp.arange(SUBLANES)
    sr, si = power(SCAN_SHIFTS)
    keep = (rows[None, :] >= jnp.asarray(SCAN_SHIFTS)[:, None]).astype(F32)[:, :, None]
    astep = jnp.concatenate([keep * sr[:, None, :], keep * si[:, None, :]], axis=-1)
    wr, wi = power(rows + 1)
    apow = jnp.concatenate([wr, wi], axis=-1)
    return bmat, cmat, astep, apow


def _mlstm_kernel(qk_ref, vt_ref, o_ref, grow_ref, gcol_ref, cw_ref, cb_ref, wq_ref, wk_ref,
                  ng_ref, out_ref, tail_ref, c_ref, n_ref, m_ref):
    chunk = qk_ref.shape[1]
    nh = MLSTM_HEADS

    @pl.when(pl.program_id(1) == 0)
    def _():
        tail_ref[...] = jnp.zeros_like(tail_ref)
        c_ref[...] = jnp.zeros_like(c_ref)
        n_ref[...] = jnp.zeros_like(n_ref)
        m_ref[...] = jnp.zeros_like(m_ref)

    x = qk_ref[0]
    ext = jnp.concatenate([tail_ref[...], x], axis=0)
    conv = cb_ref[...] + cw_ref[CONV_TAPS - 1:CONV_TAPS, :] * x
    for tap in range(CONV_TAPS - 1):
        back = CONV_TAPS - 1 - tap
        conv = conv + cw_ref[tap:tap + 1, :] * ext[SUBLANES - back:SUBLANES - back + chunk, :]
    tail_ref[...] = x[chunk - SUBLANES:, :]
    cx = _silu(conv)

    key_id = lax.broadcasted_iota(jnp.int32, (chunk, chunk), 0)
    query_id = lax.broadcasted_iota(jnp.int32, (chunk, chunk), 1)
    causal = key_id <= query_id
    k_scale = MLSTM_HEAD_DIM ** -0.5
    heads = range(nh)
    cols = [slice(hd * LANES, (hd + 1) * LANES) for hd in heads]
    cxh = [cx[:, cols[hd]].astype(BF16) for hd in heads]
    q = [_dot(cxh[hd], wq_ref[hd]).astype(BF16) for hd in heads]
    k = [(_dot(cxh[hd], wk_ref[hd]) * k_scale).astype(BF16) for hd in heads]
    v_t = [vt_ref[0, cols[hd], :] for hd in heads]
    kq = [_dot_nt(k[hd], q[hd]) for hd in heads]
    carry = [_dot_nt(c_ref[hd].astype(BF16), q[hd]) for hd in heads]
    n_q = [_dot_nt(n_ref[...].astype(BF16), q[hd]) for hd in heads]

    s, inter, m_t, stats = [], [], [], []
    for hd in heads:
        li_row = grow_ref[0, hd:hd + 1, :]
        b_row = grow_ref[0, nh + hd:nh + hd + 1, :]
        d_col = gcol_ref[0, :, hd:hd + 1] - gcol_ref[0, :, nh + hd:nh + hd + 1]
        m_st = m_ref[hd:hd + 1, 0:1]
        dmat = jnp.where(causal, b_row + d_col, -jnp.inf)
        m_inter = b_row + m_st
        m_t.append(jnp.maximum(m_inter, jnp.max(dmat, axis=0, keepdims=True)))
        s.append(kq[hd] * jnp.exp(dmat - m_t[hd]))
        inter.append(jnp.exp(m_inter - m_t[hd]))

        b_last = b_row[:, chunk - 1:chunk]
        w_row = b_last - b_row + li_row
        m_new = jnp.maximum(b_last + m_st, jnp.max(w_row, axis=1, keepdims=True))
        stats.append((m_new, jnp.exp(b_last + m_st - m_new), jnp.exp(w_row - m_new)))

    sv = [_dot(v_t[hd].astype(BF16), s[hd].astype(BF16)) for hd in heads]
    vk = [_dot((v_t[hd] * stats[hd][2]).astype(BF16), k[hd]) for hd in heads]
    row8 = lax.broadcasted_iota(jnp.int32, (SUBLANES, chunk), 0)
    ws_rows = jnp.zeros((SUBLANES, chunk), F32)
    for hd in heads:
        ws_rows = jnp.where(row8 == hd, stats[hd][2], ws_rows)
    ws_k = [_dot(ws_rows.astype(BF16), k[hd]) for hd in heads]

    for hd in heads:
        m_new, decay, _ = stats[hd]
        num = sv[hd] + inter[hd] * carry[hd]
        den = jnp.sum(s[hd], axis=0, keepdims=True) + inter[hd] * n_q[hd][hd:hd + 1, :]
        h = num * (1.0 / jnp.maximum(jnp.abs(den), jnp.exp(-m_t[hd])))
        c_ref[hd] = decay * c_ref[hd] + vk[hd]
        n_ref[hd:hd + 1, :] = decay * n_ref[hd:hd + 1, :] + ws_k[hd][hd:hd + 1, :]
        m_ref[hd:hd + 1, :] = jnp.broadcast_to(m_new, (1, LANES))
        hn = h * lax.rsqrt(jnp.sum(h * h, axis=0, keepdims=True) * (1.0 / MLSTM_HEAD_DIM) + NORM_EPS)
        out_ref[0, :, cols[hd]] = _sigmoid(o_ref[0, :, cols[hd]]) * (hn.T * ng_ref[:, cols[hd]])


def _mlstm(qk, v_t, o, gate_rows, gate_cols, conv_w, conv_b, wq, wk, norm_g):
    bsz, seq, width = qk.shape
    chunk = MLSTM_CHUNK
    tok = lambda b, i: (b, i, 0)
    tok_spec = pl.BlockSpec((1, chunk, width), tok)
    return pl.pallas_call(
        _mlstm_kernel,
        grid=(bsz, seq // chunk),
        in_specs=[tok_spec, pl.BlockSpec((1, width, chunk), lambda b, i: (b, 0, i)), tok_spec,
                  pl.BlockSpec((1, GATE_ROWS, chunk), lambda b, i: (b, 0, i)),
                  pl.BlockSpec((1, chunk, GATE_ROWS), tok)] + [
            _const_spec(a.shape) for a in (conv_w, conv_b, wq, wk, norm_g)],
        out_specs=tok_spec,
        out_shape=jax.ShapeDtypeStruct(qk.shape, F32),
        scratch_shapes=[pltpu.VMEM((SUBLANES, width), F32),
                        pltpu.VMEM((MLSTM_HEADS, LANES, LANES), F32),
                        pltpu.VMEM((SUBLANES, LANES), F32),
                        pltpu.VMEM((SUBLANES, LANES), F32)],
        compiler_params=_params(("parallel", "arbitrary")),
        name="mlstm_chunkwise",
    )(qk, v_t, o, gate_rows, gate_cols, conv_w, conv_b, wq, wk, norm_g)


FOX_BIAS_LANES = 6
FOX_VT_ROWS = FOX_HEAD_DIM + 16
FOX_FLAG_NEW_QUERY, FOX_FLAG_DIAGONAL, FOX_FLAG_FINISH = 1, 2, 4


def _fox_kernel(qa_ref, ka_ref, kb_ref, qo_ref, flag_ref, q_ref, qbias_ref, k_ref, kbias_ref,
                vt_ref, o_ref, qx_ref, s_ref, m_ref, acc_ref):
    step = pl.program_id(1)
    flags = flag_ref[step]
    tq, tk = q_ref.shape[1], k_ref.shape[1]
    hd = FOX_HEAD_DIM

    def pair_lanes(pair):
        return slice(pair * LANES, (pair + 1) * LANES)

    def reset_accumulators():
        m_ref[...] = jnp.full_like(m_ref, NEG_BIG)
        acc_ref[...] = jnp.zeros_like(acc_ref)

    @pl.when(step == 0)
    def _():
        s_ref[...] = jnp.full(s_ref.shape, NEG_BIG, F32)
        reset_accumulators()

    @pl.when((flags & FOX_FLAG_NEW_QUERY) != 0)
    def _():
        lane = lax.broadcasted_iota(jnp.int32, (1, LANES), 1)
        for pair in range(FOX_PAIRS):
            q = q_ref[0, :, pair_lanes(pair)] * (hd ** -0.5 * LOG2_E)
            qb = qbias_ref[0, :, pair_lanes(pair)]
            for hh in range(2):
                in_head = (lane >= hh * hd) & (lane < (hh + 1) * hd)
                in_bias = (lane >= hh * FOX_BIAS_LANES) & (lane < (hh + 1) * FOX_BIAS_LANES)
                qx_ref[2 * pair + hh, :, :LANES] = jnp.where(in_head, q, 0.0).astype(BF16)
                qx_ref[2 * pair + hh, :, LANES:] = jnp.where(in_bias, qb, jnp.zeros_like(qb))

    def attend(diagonal):
        kx = [jnp.concatenate([k_ref[0, :, pair_lanes(pair)], kbias_ref[0, :, pair_lanes(pair)]],
                              axis=1) for pair in range(FOX_PAIRS)]
        probs = {}

        def score(hh):
            s = _dot_nt(kx[hh // 2], qx_ref[hh])
            if diagonal:
                visible = (lax.broadcasted_iota(jnp.int32, (tk, tq), 0)
                           <= lax.broadcasted_iota(jnp.int32, (tk, tq), 1))
                s = jnp.where(visible, s, NEG_BIG)
            s_ref[hh] = s

        def softmax(hh):
            s = s_ref[hh]
            m_prev = m_ref[hh]
            m_new = jnp.maximum(m_prev, jnp.max(s, axis=0, keepdims=True))
            m_ref[hh] = m_new
            probs[hh] = (jnp.exp2(m_prev - m_new), jnp.exp2(s - m_new).astype(BF16))

        def value(hh):
            alpha, p = probs.pop(hh)
            acc_ref[hh] = alpha * acc_ref[hh] + _dot(vt_ref[0, hh], p)

        softmax(0)
        for hh in range(FOX_HEADS):
            score(hh)
            if hh + 1 < FOX_HEADS:
                softmax(hh + 1)
            value(hh)

    @pl.when((flags & FOX_FLAG_DIAGONAL) == 0)
    def _():
        attend(False)

    @pl.when((flags & FOX_FLAG_DIAGONAL) != 0)
    def _():
        attend(True)

    @pl.when((flags & FOX_FLAG_FINISH) != 0)
    def _():
        for pair in range(FOX_PAIRS):
            out_t = jnp.concatenate([acc_ref[hh, :hd, :] / acc_ref[hh, hd:hd + 1, :]
                                     for hh in (2 * pair, 2 * pair + 1)], axis=0)
            o_ref[0, :, pair_lanes(pair)] = out_t.T
        reset_accumulators()

    @pl.when(step == 0)
    def _():
        reset_accumulators()


def _fox(q, qb, k, kb, v_t, tq):
    bsz, seq, width = q.shape
    nq = seq // tq
    q_idx = [i for i in range(nq) for _ in range(i + 1)]
    k_idx = [j for i in range(nq) for j in range(i + 1)]
    n_pairs = len(q_idx)
    q_new = q_idx + [q_idx[-1]]
    k_new = k_idx + [k_idx[-1]]
    k_old = [0] + k_idx
    q_old = [0] + q_idx
    flags = []
    for n in range(n_pairs + 1):
        f = 0
        if n < n_pairs and k_idx[n] == 0:
            f |= FOX_FLAG_NEW_QUERY
        if n < n_pairs and k_idx[n] == q_idx[n]:
            f |= FOX_FLAG_DIAGONAL
        if n >= 1 and (n == n_pairs or k_idx[n] == 0):
            f |= FOX_FLAG_FINISH
        flags.append(f)
    tables = [jnp.asarray(t, jnp.int32) for t in (q_new, k_new, k_old, q_old, flags)]
    q_spec = pl.BlockSpec((1, tq, width), lambda b, s, qa, ka, kb_, qo, fl: (b, qa[s], 0))
    k_spec = pl.BlockSpec((1, tq, width), lambda b, s, qa, ka, kb_, qo, fl: (b, ka[s], 0))
    grid_spec = pltpu.PrefetchScalarGridSpec(
        num_scalar_prefetch=len(tables),
        grid=(bsz, n_pairs + 1),
        in_specs=[q_spec, q_spec, k_spec, k_spec,
                  pl.BlockSpec((1, FOX_HEADS, FOX_VT_ROWS, tq),
                               lambda b, s, qa, ka, kb_, qo, fl: (b, 0, 0, kb_[s]))],
        out_specs=pl.BlockSpec((1, tq, width), lambda b, s, qa, ka, kb_, qo, fl: (b, qo[s], 0)),
        scratch_shapes=[pltpu.VMEM((FOX_HEADS, tq, 2 * LANES), BF16),
                        pltpu.VMEM((FOX_HEADS, tq, tq), F32),
                        pltpu.VMEM((FOX_HEADS, 1, tq), F32),
                        pltpu.VMEM((FOX_HEADS, FOX_VT_ROWS, tq), F32)],
    )
    v_heads = v_t.reshape(bsz, FOX_HEADS, FOX_HEAD_DIM, seq)
    ones = jnp.ones((bsz, FOX_HEADS, FOX_VT_ROWS - FOX_HEAD_DIM, seq), v_t.dtype)
    v_ext = jnp.concatenate([v_heads, ones], axis=2)
    return pl.pallas_call(
        _fox_kernel, grid_spec=grid_spec,
        out_shape=jax.ShapeDtypeStruct((bsz, seq, width), F32),
        compiler_params=_params(("parallel", "arbitrary")),
        name="forgetting_attention",
    )(*tables, q, qb, k, kb, v_ext)


def _fox_bias_lanes(split):
    bsz, _, _, seq = split.shape
    lo = 2 * MLSTM_HEADS
    pieces = jnp.transpose(split[:, :, lo:lo + FOX_HEADS, :], (0, 3, 2, 1))
    ones = jnp.ones_like(pieces)

    def lanes(per_head):
        per_pair = per_head.reshape(bsz, seq, FOX_PAIRS, 2 * FOX_BIAS_LANES)
        per_pair = jnp.pad(per_pair, ((0, 0), (0, 0), (0, 0), (0, LANES - 2 * FOX_BIAS_LANES)))
        return per_pair.reshape(bsz, seq, FOX_PAIRS * LANES).astype(BF16)

    return (lanes(jnp.concatenate([pieces, ones], axis=-1)),
            lanes(jnp.concatenate([ones, -pieces], axis=-1)))


def _merge_kernel(x_ref, ys_ref, ym_ref, yf_ref, gate_ref, g1_ref, ws_ref, wm_ref, wf_ref,
                  wo_ref, o_ref):
    d = x_ref.shape[-1]
    merged = (_sigmoid(gate_ref[0, :, :d]) * _dot(ys_ref[0].astype(BF16), ws_ref[...])
              + _sigmoid(gate_ref[0, :, d:2 * d]) * _dot(ym_ref[0].astype(BF16), wm_ref[...])
              + _sigmoid(gate_ref[0, :, 2 * d:]) * _dot(yf_ref[0].astype(BF16), wf_ref[...]))
    o_ref[0] = x_ref[0] + g1_ref[0] * _dot(merged.astype(BF16), wo_ref[...])


def _merge(x, y_s5, y_m, y_f, gates, g1, w_up_s5, w_up_m, w_up_f, w_out, tm):
    bsz, seq, d = x.shape
    tok = lambda b, i: (b, i, 0)
    acts = (x, y_s5, y_m, y_f, gates)
    weights = (w_up_s5, w_up_m, w_up_f, w_out)
    return pl.pallas_call(
        _merge_kernel,
        grid=(bsz, seq // tm),
        in_specs=[pl.BlockSpec((1, tm, a.shape[-1]), tok) for a in acts]
        + [pl.BlockSpec((1, 1, d), lambda b, i: (b, 0, 0))]
        + [_const_spec(w.shape) for w in weights],
        out_specs=pl.BlockSpec((1, tm, d), tok),
        out_shape=jax.ShapeDtypeStruct(x.shape, F32),
        compiler_params=_params(("parallel", "parallel")),
        name="merge_output_projection",
    )(*acts, g1, *weights)


FFN_COLS = 256


def _ffn_kernel(x_ref, sh_ref, sc_ref, g2_ref, gain_ref, w1_ref, w3_ref, w2_ref, fg_ref, o_ref,
                *, final_norm):
    x = x_ref[0]
    h = _rms_modulate(x, gain_ref[...], sc_ref[0], sh_ref[0]).astype(BF16)
    starts = list(range(0, w1_ref.shape[1], FFN_COLS))

    def up(c0):
        return _dot(h, w1_ref[:, c0:c0 + FFN_COLS]), _dot(h, w3_ref[:, c0:c0 + FFN_COLS])

    acc = jnp.zeros(x.shape, F32)
    nxt = up(starts[0])
    for i, c0 in enumerate(starts):
        a, b = nxt
        if i + 1 < len(starts):
            nxt = up(starts[i + 1])
        acc = acc + _dot((_silu(a) * b).astype(BF16), w2_ref[c0:c0 + FFN_COLS, :])
    y = x + g2_ref[0] * acc
    if final_norm:
        y = (y * lax.rsqrt(jnp.mean(y * y, axis=-1, keepdims=True) + NORM_EPS)) * fg_ref[...]
    o_ref[0] = y


def _ffn(x, shift, scale, g2, gain, w1, w3, w2, final_g, final_norm, tm):
    bsz, seq, d = x.shape
    tok = lambda b, i: (b, i, 0)
    vec = pl.BlockSpec((1, 1, d), lambda b, i: (b, 0, 0))
    return pl.pallas_call(
        functools.partial(_ffn_kernel, final_norm=final_norm),
        grid=(bsz, seq // tm),
        in_specs=[pl.BlockSpec((1, tm, d), tok), vec, vec, vec]
        + [_const_spec(a.shape) for a in (gain, w1, w3, w2, final_g)],
        out_specs=pl.BlockSpec((1, tm, d), tok),
        out_shape=jax.ShapeDtypeStruct(x.shape, F32),
        compiler_params=_params(("parallel", "parallel")),
        name="norm_swiglu",
    )(x, shift, scale, g2, gain, w1, w3, w2, final_g)


def _pad_heads(w, heads, head_dim):
    lead = w.shape[:-1]
    w = w.reshape(lead + (heads, head_dim))
    w = jnp.pad(w, [(0, 0)] * len(lead) + [(0, 0), (0, LANES - head_dim)])
    return w.reshape(lead + (heads * LANES,))


def _layer_weights(w_in, b_in):
    sizes = (S5_WIDTH, MLSTM_WIDTH, MLSTM_WIDTH, MLSTM_WIDTH, MLSTM_HEADS, MLSTM_HEADS,
             FOX_WIDTH, FOX_WIDTH, FOX_WIDTH, FOX_HEADS)
    offs = [0]
    for n in sizes:
        offs.append(offs[-1] + n)
    both = jnp.concatenate([w_in, b_in[None, :]], axis=0)
    part = [both[:, offs[i]:offs[i + 1]] for i in range(len(sizes))] + [both[:, offs[-1]:]]
    pad_m = lambda w: _pad_heads(w, MLSTM_HEADS, MLSTM_HEAD_DIM)
    cat = jnp.concatenate([part[0], pad_m(part[1]), pad_m(part[3]),
                           part[6], part[7], part[10]], axis=1)
    small = jnp.concatenate([part[4], part[5], part[9]], axis=1)
    small = jnp.pad(small, ((0, 0), (0, GATE_ROWS - small.shape[1])))
    cat_t = jnp.concatenate([small, part[8], pad_m(part[2])], axis=1).T
    return (cat[:-1].astype(BF16), cat[-1:], cat_t[:, :-1].astype(BF16), cat_t[:, -1:])


def kernel(x, c, mod_w, mod_b, norm1_g, norm2_g, w_in, b_in, s5_a_re, s5_a_im, s5_log_dt, s5_b_re, s5_b_im, s5_c_re, s5_c_im, s5_d, s5_w_glu, mlstm_conv_w, mlstm_conv_b, mlstm_wq, mlstm_wk, mlstm_norm_g, w_up_s5, w_up_mlstm, w_up_fox, w_out, ffn_w1, ffn_w3, ffn_w2, final_g):
    bsz, seq, d = x.shape
    depth = mod_w.shape[0]
    assert seq % MLSTM_CHUNK == 0
    tm = _row_tile(seq, 512)
    tq = _row_tile(seq, 512)
    ts = _row_tile(seq, 512)

    mod = _modulation(c, mod_w, mod_b)
    final_gain = final_g.reshape(1, d)
    for l in range(depth):
        sh1, sc1, g1, sh2, sc2, g2 = [mod[l, :, i * d:(i + 1) * d].reshape(bsz, 1, d)
                                      for i in range(6)]
        w_cat, b_cat, w_t, b_t = _layer_weights(w_in[l], b_in[l])
        (s5_u, m_qk, m_o, f_q, f_k, gates, gate_pre, f_vt, m_vt) = _input_projection(
            x, sh1, sc1, norm1_g[l].reshape(1, d), w_cat, b_cat, w_t, b_t, tm)

        gate_rows, gate_split = _gate_scan(gate_pre)
        gate_cols = jnp.swapaxes(gate_rows, 1, 2)
        f_qb, f_kb = _fox_bias_lanes(gate_split)

        bmat, cmat, astep, apow = _s5_tables(s5_a_re[l], s5_a_im[l], s5_log_dt[l], s5_b_re[l],
                                             s5_b_im[l], s5_c_re[l], s5_c_im[l])
        y_s5 = _s5(s5_u, bmat, cmat, astep, apow, s5_d[l].reshape(1, -1),
                   s5_w_glu[l].astype(BF16), ts)

        pad_hh = ((0, 0), (0, LANES - MLSTM_HEAD_DIM), (0, LANES - MLSTM_HEAD_DIM))
        y_m = _mlstm(m_qk, m_vt, m_o, gate_rows, gate_cols,
                     _pad_heads(mlstm_conv_w[l], MLSTM_HEADS, MLSTM_HEAD_DIM),
                     _pad_heads(mlstm_conv_b[l].reshape(1, -1), MLSTM_HEADS, MLSTM_HEAD_DIM),
                     jnp.pad(mlstm_wq[l], pad_hh).astype(BF16),
                     jnp.pad(mlstm_wk[l], pad_hh).astype(BF16),
                     _pad_heads(mlstm_norm_g[l].reshape(1, -1), MLSTM_HEADS, MLSTM_HEAD_DIM))

        y_f = _fox(f_q, f_qb, f_k, f_kb, f_vt, tq)

        w_up_m = _pad_heads(w_up_mlstm[l].T, MLSTM_HEADS, MLSTM_HEAD_DIM).T
        x = _merge(x, y_s5, y_m, y_f, gates, g1, w_up_s5[l].astype(BF16), w_up_m.astype(BF16),
                   w_up_fox[l].astype(BF16), w_out[l].astype(BF16), tm)
        x = _ffn(x, sh2, sc2, g2, norm2_g[l].reshape(1, d), ffn_w1[l].astype(BF16),
                 ffn_w3[l].astype(BF16), ffn_w2[l].astype(BF16), final_gain,
                 l == depth - 1, tm)
    return x
```

```python
import functools
import math

import jax
import jax.numpy as jnp
from jax import lax
from jax.experimental import pallas as pl
from jax.experimental.pallas import tpu as pltpu

F32 = jnp.float32
BF16 = jnp.bfloat16

LANES = 128
SUBLANES = 8
VMEM_LIMIT_BYTES = 56 * 1024 * 1024

NORM_EPS = 1e-6
S5_GROUPS = 16
S5_GROUP_DIM = 16
S5_STATE = 64
S5_WIDTH = S5_GROUPS * S5_GROUP_DIM
S5_CHANNELS = S5_GROUPS * S5_STATE
MLSTM_HEADS = 4
MLSTM_HEAD_DIM = 96
MLSTM_WIDTH = MLSTM_HEADS * MLSTM_HEAD_DIM
MLSTM_PAD_WIDTH = MLSTM_HEADS * LANES
MLSTM_CHUNK = 128
CONV_TAPS = 4
FOX_HEADS = 6
FOX_HEAD_DIM = 64
FOX_WIDTH = FOX_HEADS * FOX_HEAD_DIM
FOX_PAIRS = FOX_WIDTH // LANES
GATE_ROWS = 16
NEG_BIG = -1e30
LOG2_E = math.log2(math.e)

NT_DIMS = (((1,), (1,)), ((), ()))


def _dot(a, b):
    return jnp.dot(a, b, preferred_element_type=F32)


def _dot_nt(a, b):
    return lax.dot_general(a, b, NT_DIMS, preferred_element_type=F32)


def _sigmoid(x):
    return 1.0 / (1.0 + jnp.exp(-x))


def _silu(x):
    return x * _sigmoid(x)


def _log_sigmoid(x):
    return jnp.minimum(x, 0.0) - jnp.log1p(jnp.exp(-jnp.abs(x)))


def _rms_modulate(x, gain, scale, shift):
    y = x * lax.rsqrt(jnp.mean(x * x, axis=-1, keepdims=True) + NORM_EPS)
    return (y * gain) * (1.0 + scale) + shift


def _params(semantics):
    return pltpu.CompilerParams(dimension_semantics=semantics, vmem_limit_bytes=VMEM_LIMIT_BYTES)


def _const_spec(shape):
    nd = len(shape)
    return pl.BlockSpec(shape, lambda *_: (0,) * nd, pipeline_mode=pl.Buffered(1))


def _row_tile(seq, want):
    t = min(want, seq)
    assert seq % t == 0 and t % SUBLANES == 0
    return t


def _mod_kernel(c_ref, w_ref, b_ref, o_ref):
    cf = _silu(c_ref[...]).astype(BF16)
    o_ref[0] = _dot(cf, w_ref[0].astype(BF16)) + b_ref[0]


def _modulation(c, mod_w, mod_b):
    depth, d, six_d = mod_w.shape
    bsz = c.shape[0]
    n_col = six_d // d
    return pl.pallas_call(
        _mod_kernel,
        grid=(depth, n_col),
        in_specs=[pl.BlockSpec((bsz, d), lambda l, j: (0, 0)),
                  pl.BlockSpec((1, d, d), lambda l, j: (l, 0, j)),
                  pl.BlockSpec((1, 1, d), lambda l, j: (l, 0, j))],
        out_specs=pl.BlockSpec((1, bsz, d), lambda l, j: (l, 0, j)),
        out_shape=jax.ShapeDtypeStruct((depth, bsz, six_d), F32),
        compiler_params=_params(("parallel", "parallel")),
        name="adaln_modulation",
    )(c, mod_w, mod_b.reshape(depth, 1, six_d))


IN_SEGMENTS = ((S5_WIDTH, F32), (MLSTM_PAD_WIDTH, F32), (MLSTM_PAD_WIDTH, F32),
               (FOX_WIDTH, F32), (FOX_WIDTH, BF16))
IN_T_SEGMENTS = ((GATE_ROWS, F32), (FOX_WIDTH, BF16), (MLSTM_PAD_WIDTH, F32))
DOT_COLS = 512


def _inproj_kernel(x_ref, sh_ref, sc_ref, g_ref, w_ref, b_ref, wt_ref, bt_ref, *out_refs):
    h = _rms_modulate(x_ref[0], g_ref[...], sc_ref[0], sh_ref[0]).astype(BF16)
    n_t = len(IN_T_SEGMENTS)
    start = 0
    for o_ref in out_refs[:-n_t]:
        width = o_ref.shape[-1]
        for c0 in range(0, width, DOT_COLS):
            cw = min(DOT_COLS, width - c0)
            z = _dot(h, w_ref[:, start + c0:start + c0 + cw]) + b_ref[:, start + c0:start + c0 + cw]
            o_ref[0, :, c0:c0 + cw] = z.astype(o_ref.dtype)
        start += width
    start = 0
    for o_ref in out_refs[-n_t:]:
        rows = o_ref.shape[1]
        z = _dot_nt(wt_ref[start:start + rows, :], h) + bt_ref[start:start + rows, :]
        o_ref[0] = z.astype(o_ref.dtype)
        start += rows


def _input_projection(x, shift, scale, gain, w_cat, b_cat, w_t, b_t, tm):
    bsz, seq, d = x.shape
    segs = IN_SEGMENTS + ((w_cat.shape[1] - sum(w for w, _ in IN_SEGMENTS), BF16),)
    tok = lambda b, i: (b, i, 0)
    vec = lambda b, i: (b, 0, 0)
    out_shape = [jax.ShapeDtypeStruct((bsz, seq, w), dt) for w, dt in segs]
    out_shape += [jax.ShapeDtypeStruct((bsz, r, seq), dt) for r, dt in IN_T_SEGMENTS]
    out_specs = [pl.BlockSpec((1, tm, w), tok) for w, _ in segs]
    out_specs += [pl.BlockSpec((1, r, tm), lambda b, i: (b, 0, i)) for r, _ in IN_T_SEGMENTS]
    return pl.pallas_call(
        _inproj_kernel,
        grid=(bsz, seq // tm),
        in_specs=[pl.BlockSpec((1, tm, d), tok),
                  pl.BlockSpec((1, 1, d), vec), pl.BlockSpec((1, 1, d), vec),
                  _const_spec(gain.shape), _const_spec(w_cat.shape), _const_spec(b_cat.shape),
                  _const_spec(w_t.shape), _const_spec(b_t.shape)],
        out_specs=out_specs,
        out_shape=out_shape,
        compiler_params=_params(("parallel", "parallel")),
        name="norm_input_projection",
    )(x, shift, scale, gain, w_cat, b_cat, w_t, b_t)


def _bf16_round(x):
    return x.astype(BF16).astype(F32)


def _gate_scan_kernel(x_ref, o_ref, split_ref):
    x = x_ref[0]
    seq = x.shape[1]
    seg = MLSTM_CHUNK
    lane = lax.broadcasted_iota(jnp.int32, x.shape, 1) & (seg - 1)
    v = _log_sigmoid(x)
    k = 1
    while k < seg:
        v = v + jnp.where(lane >= k, pltpu.roll(v, k, 1), 0.0)
        k *= 2
    row = lax.broadcasted_iota(jnp.int32, (GATE_ROWS, seg), 0)
    carry = jnp.zeros((GATE_ROWS, 1), F32)
    for j in range(seq // seg):
        sl = slice(j * seg, (j + 1) * seg)
        local = v[:, sl]
        total = local + carry
        o_ref[0, :, sl] = jnp.where(row < MLSTM_HEADS, x[:, sl],
                                    jnp.where(row < 2 * MLSTM_HEADS, local, total))
        scaled = total * LOG2_E
        hi = _bf16_round(scaled)
        mid = _bf16_round(scaled - hi)
        split_ref[0, 0, :, sl] = hi
        split_ref[0, 1, :, sl] = mid
        split_ref[0, 2, :, sl] = _bf16_round((scaled - hi) - mid)
        carry = carry + local[:, seg - 1:seg]


def _gate_scan(pre):
    bsz, rows, seq = pre.shape
    spec = pl.BlockSpec((1, rows, seq), lambda b: (b, 0, 0))
    return pl.pallas_call(
        _gate_scan_kernel, grid=(bsz,), in_specs=[spec],
        out_specs=[spec, pl.BlockSpec((1, 3, rows, seq), lambda b: (b, 0, 0, 0))],
        out_shape=[jax.ShapeDtypeStruct(pre.shape, F32),
                   jax.ShapeDtypeStruct((bsz, 3, rows, seq), F32)],
        compiler_params=_params(("parallel",)),
        name="gate_scan",
    )(pre)


SCAN_SHIFTS = (1, 2, 4)


def _s5_kernel(u_ref, bmat_ref, cmat_ref, astep_ref, apow_ref, d_ref, wglu_ref, o_ref,
               st_ref, carry_ref):
    nch = S5_CHANNELS

    @pl.when(pl.program_id(1) == 0)
    def _():
        carry_ref[...] = jnp.zeros_like(carry_ref)

    u = u_ref[0]
    st_ref[...] = _dot(u.astype(BF16), bmat_ref[...])
    n_blocks = u.shape[0] // SUBLANES

    def body(i, carry):
        cr, ci = carry
        r0 = pl.multiple_of(i * SUBLANES, SUBLANES)
        xr = st_ref[pl.ds(r0, SUBLANES), :nch]
        xi = st_ref[pl.ds(r0, SUBLANES), nch:]
        for k, shift in enumerate(SCAN_SHIFTS):
            ar, ai = astep_ref[k, :, :nch], astep_ref[k, :, nch:]
            sr, si = pltpu.roll(xr, shift, 0), pltpu.roll(xi, shift, 0)
            xr, xi = xr + (ar * sr - ai * si), xi + (ar * si + ai * sr)
        pr, pi = apow_ref[:, :nch], apow_ref[:, nch:]
        xr, xi = xr + (pr * cr - pi * ci), xi + (pr * ci + pi * cr)
        st_ref[pl.ds(r0, SUBLANES), :nch] = xr
        st_ref[pl.ds(r0, SUBLANES), nch:] = xi
        last = SUBLANES - 1
        return (jnp.broadcast_to(xr[last:, :], xr.shape), jnp.broadcast_to(xi[last:, :], xi.shape))

    cr, ci = lax.fori_loop(0, n_blocks, body, (carry_ref[:, :nch], carry_ref[:, nch:]))
    carry_ref[:, :nch] = cr
    carry_ref[:, nch:] = ci

    y = _dot(st_ref[...].astype(BF16), cmat_ref[...]) + d_ref[...] * u
    y = y * (0.5 * (1.0 + jnp.tanh(math.sqrt(2.0 / math.pi) * (y + 0.044715 * (y * y * y)))))
    o_ref[0] = y * _sigmoid(_dot(y.astype(BF16), wglu_ref[...]))


def _s5(u, bmat, cmat, astep, apow, d_skip, w_glu, tb):
    bsz, seq, width = u.shape
    tok = lambda b, i: (b, i, 0)
    return pl.pallas_call(
        _s5_kernel,
        grid=(bsz, seq // tb),
        in_specs=[pl.BlockSpec((1, tb, width), tok)] + [
            _const_spec(a.shape) for a in (bmat, cmat, astep, apow, d_skip, w_glu)],
        out_specs=pl.BlockSpec((1, tb, width), tok),
        out_shape=jax.ShapeDtypeStruct(u.shape, F32),
        scratch_shapes=[pltpu.VMEM((tb, 2 * S5_CHANNELS), F32),
                        pltpu.VMEM((SUBLANES, 2 * S5_CHANNELS), F32)],
        compiler_params=_params(("parallel", "arbitrary")),
        name="s5_scan",
    )(u, bmat, cmat, astep, apow, d_skip, w_glu)


def _s5_tables(a_re, a_im, log_dt, b_re, b_im, c_re, c_im):
    g, n, p = b_re.shape
    dt = jnp.exp(log_dt)[:, None]
    mag = jnp.exp(dt * a_re)
    ang = dt * a_im
    abar_r, abar_i = mag * jnp.cos(ang), mag * jnp.sin(ang)
    den = a_re * a_re + a_im * a_im
    pr, qi = abar_r - 1.0, abar_i
    coef_r = (pr * a_re + qi * a_im) / den
    coef_i = (qi * a_re - pr * a_im) / den
    bbar_r = coef_r[..., None] * b_re - coef_i[..., None] * b_im
    bbar_i = coef_r[..., None] * b_im + coef_i[..., None] * b_re
    eye = jnp.eye(g, dtype=F32)

    def in_block(bb):
        return jnp.einsum('gnp,gh->gphn', bb, eye).reshape(g * p, g * n)

    def out_block(cc):
        return jnp.einsum('gpn,gh->gnhp', cc, eye).reshape(g * n, g * p)

    bmat = jnp.concatenate([in_block(bbar_r), in_block(bbar_i)], axis=1).astype(BF16)
    cmat = jnp.concatenate([out_block(c_re), -out_block(c_im)], axis=0).astype(BF16)

    def power(e):
        e = jnp.asarray(e, F32)[:, None, None]
        m = jnp.exp(e * (dt * a_re)[None])
        th = e * ang[None]
        return (m * jnp.cos(th)).reshape(-1, g * n), (m * jnp.sin(th)).reshape(-1, g * n)

    rows = jnp.arange(SUBLANES)
    sr, si = power(SCAN_SHIFTS)
    keep = (rows[None, :] >= jnp.asarray(SCAN_SHIFTS)[:, None]).astype(F32)[:, :, None]
    astep = jnp.concatenate([keep * sr[:, None, :], keep * si[:, None, :]], axis=-1)
    wr, wi = power(rows + 1)
    apow = jnp.concatenate([wr, wi], axis=-1)
    return bmat, cmat, astep, apow


def _mlstm_kernel(qk_ref, vt_ref, o_ref, grow_ref, gcol_ref, cw_ref, cb_ref, wq_ref, wk_ref,
                  ng_ref, out_ref, tail_ref, c_ref, n_ref, m_ref):
    chunk = qk_ref.shape[1]
    nh = MLSTM_HEADS

    @pl.when(pl.program_id(1) == 0)
    def _():
        tail_ref[...] = jnp.zeros_like(tail_ref)
        c_ref[...] = jnp.zeros_like(c_ref)
        n_ref[...] = jnp.zeros_like(n_ref)
        m_ref[...] = jnp.zeros_like(m_ref)

    x = qk_ref[0]
    ext = jnp.concatenate([tail_ref[...], x], axis=0)
    conv = cb_ref[...] + cw_ref[CONV_TAPS - 1:CONV_TAPS, :] * x
    for tap in range(CONV_TAPS - 1):
        back = CONV_TAPS - 1 - tap
        conv = conv + cw_ref[tap:tap + 1, :] * ext[SUBLANES - back:SUBLANES - back + chunk, :]
    tail_ref[...] = x[chunk - SUBLANES:, :]
    cx = _silu(conv)

    key_id = lax.broadcasted_iota(jnp.int32, (chunk, chunk), 0)
    query_id = lax.broadcasted_iota(jnp.int32, (chunk, chunk), 1)
    causal = key_id <= query_id
    k_scale = MLSTM_HEAD_DIM ** -0.5
    heads = range(nh)
    cols = [slice(hd * LANES, (hd + 1) * LANES) for hd in heads]
    cxh = [cx[:, cols[hd]].astype(BF16) for hd in heads]
    q = [_dot(cxh[hd], wq_ref[hd]).astype(BF16) for hd in heads]
    k = [(_dot(cxh[hd], wk_ref[hd]) * k_scale).astype(BF16) for hd in heads]
    v_t = [vt_ref[0, cols[hd], :] for hd in heads]
    kq = [_dot_nt(k[hd], q[hd]) for hd in heads]
    carry = [_dot_nt(c_ref[hd].astype(BF16), q[hd]) for hd in heads]
    n_q = [_dot_nt(n_ref[...].astype(BF16), q[hd]) for hd in heads]

    s, inter, m_t, stats = [], [], [], []
    for hd in heads:
        li_row = grow_ref[0, hd:hd + 1, :]
        b_row = grow_ref[0, nh + hd:nh + hd + 1, :]
        d_col = gcol_ref[0, :, hd:hd + 1] - gcol_ref[0, :, nh + hd:nh + hd + 1]
        m_st = m_ref[hd:hd + 1, 0:1]
        dmat = jnp.where(causal, b_row + d_col, -jnp.inf)
        m_inter = b_row + m_st
        m_t.append(jnp.maximum(m_inter, jnp.max(dmat, axis=0, keepdims=True)))
        s.append(kq[hd] * jnp.exp(dmat - m_t[hd]))
        inter.append(jnp.exp(m_inter - m_t[hd]))

        b_last = b_row[:, chunk - 1:chunk]
        w_row = b_last - b_row + li_row
        m_new = jnp.maximum(b_last + m_st, jnp.max(w_row, axis=1, keepdims=True))
        stats.append((m_new, jnp.exp(b_last + m_st - m_new), jnp.exp(w_row - m_new)))

    sv = [_dot(v_t[hd].astype(BF16), s[hd].astype(BF16)) for hd in heads]
    vk = [_dot((v_t[hd] * stats[hd][2]).astype(BF16), k[hd]) for hd in heads]
    row8 = lax.broadcasted_iota(jnp.int32, (SUBLANES, chunk), 0)
    ws_rows = jnp.zeros((SUBLANES, chunk), F32)
    for hd in heads:
        ws_rows = jnp.where(row8 == hd, stats[hd][2], ws_rows)
    ws_k = [_dot(ws_rows.astype(BF16), k[hd]) for hd in heads]

    for hd in heads:
        m_new, decay, _ = stats[hd]
        num = sv[hd] + inter[hd] * carry[hd]
        den = jnp.sum(s[hd], axis=0, keepdims=True) + inter[hd] * n_q[hd][hd:hd + 1, :]
        h = num * (1.0 / jnp.maximum(jnp.abs(den), jnp.exp(-m_t[hd])))
        c_ref[hd] = decay * c_ref[hd] + vk[hd]
        n_ref[hd:hd + 1, :] = decay * n_ref[hd:hd + 1, :] + ws_k[hd][hd:hd + 1, :]
        m_ref[hd:hd + 1, :] = jnp.broadcast_to(m_new, (1, LANES))
        hn = h * lax.rsqrt(jnp.sum(h * h, axis=0, keepdims=True) * (1.0 / MLSTM_HEAD_DIM) + NORM_EPS)
        out_ref[0, :, cols[hd]] = _sigmoid(o_ref[0, :, cols[hd]]) * (hn.T * ng_ref[:, cols[hd]])


def _mlstm(qk, v_t, o, gate_rows, gate_cols, conv_w, conv_b, wq, wk, norm_g):
    bsz, seq, width = qk.shape
    chunk = MLSTM_CHUNK
    tok = lambda b, i: (b, i, 0)
    tok_spec = pl.BlockSpec((1, chunk, width), tok)
    return pl.pallas_call(
        _mlstm_kernel,
        grid=(bsz, seq // chunk),
        in_specs=[tok_spec, pl.BlockSpec((1, width, chunk), lambda b, i: (b, 0, i)), tok_spec,
                  pl.BlockSpec((1, GATE_ROWS, chunk), lambda b, i: (b, 0, i)),
                  pl.BlockSpec((1, chunk, GATE_ROWS), tok)] + [
            _const_spec(a.shape) for a in (conv_w, conv_b, wq, wk, norm_g)],
        out_specs=tok_spec,
        out_shape=jax.ShapeDtypeStruct(qk.shape, F32),
        scratch_shapes=[pltpu.VMEM((SUBLANES, width), F32),
                        pltpu.VMEM((MLSTM_HEADS, LANES, LANES), F32),
                        pltpu.VMEM((SUBLANES, LANES), F32),
                        pltpu.VMEM((SUBLANES, LANES), F32)],
        compiler_params=_params(("parallel", "arbitrary")),
        name="mlstm_chunkwise",
    )(qk, v_t, o, gate_rows, gate_cols, conv_w, conv_b, wq, wk, norm_g)


FOX_BIAS_LANES = 6
FOX_VT_ROWS = FOX_HEAD_DIM + 16
FOX_FLAG_NEW_QUERY, FOX_FLAG_DIAGONAL, FOX_FLAG_FINISH = 1, 2, 4


def _fox_kernel(qa_ref, ka_ref, kb_ref, qo_ref, flag_ref, q_ref, qbias_ref, k_ref, kbias_ref,
                vt_ref, o_ref, qx_ref, s_ref, smax_ref, m_ref, acc_ref):
    step = pl.program_id(1)
    flags = flag_ref[step]
    tq, tk = q_ref.shape[1], k_ref.shape[1]
    hd = FOX_HEAD_DIM

    def pair_lanes(pair):
        return slice(pair * LANES, (pair + 1) * LANES)

    def reset_accumulators():
        m_ref[...] = jnp.full_like(m_ref, NEG_BIG)
        acc_ref[...] = jnp.zeros_like(acc_ref)

    @pl.when(step == 0)
    def _():
        s_ref[...] = jnp.full(s_ref.shape, NEG_BIG, F32)
        smax_ref[...] = jnp.full(smax_ref.shape, NEG_BIG, F32)
        reset_accumulators()

    @pl.when((flags & FOX_FLAG_NEW_QUERY) != 0)
    def _():
        lane = lax.broadcasted_iota(jnp.int32, (1, LANES), 1)
        for pair in range(FOX_PAIRS):
            q = q_ref[0, :, pair_lanes(pair)] * (hd ** -0.5 * LOG2_E)
            qb = qbias_ref[0, :, pair_lanes(pair)]
            for hh in range(2):
                in_head = (lane >= hh * hd) & (lane < (hh + 1) * hd)
                in_bias = (lane >= hh * FOX_BIAS_LANES) & (lane < (hh + 1) * FOX_BIAS_LANES)
                qx_ref[2 * pair + hh, :, :LANES] = jnp.where(in_head, q, 0.0).astype(BF16)
                qx_ref[2 * pair + hh, :, LANES:] = jnp.where(in_bias, qb, jnp.zeros_like(qb))

    def attend(diagonal):
        kx = [jnp.concatenate([k_ref[0, :, pair_lanes(pair)], kbias_ref[0, :, pair_lanes(pair)]],
                              axis=1) for pair in range(FOX_PAIRS)]
        probs = {}

        def score(hh):
            s = _dot_nt(kx[hh // 2], qx_ref[hh])
            if diagonal:
                visible = (lax.broadcasted_iota(jnp.int32, (tk, tq), 0)
                           <= lax.broadcasted_iota(jnp.int32, (tk, tq), 1))
                s = jnp.where(visible, s, NEG_BIG)
            s_ref[hh] = s
            smax_ref[hh] = jnp.max(s, axis=0, keepdims=True)

        def softmax(hh):
            s = s_ref[hh]
            m_prev = m_ref[hh]
            m_new = jnp.maximum(m_prev, smax_ref[hh])
            m_ref[hh] = m_new
            probs[hh] = (jnp.exp2(m_prev - m_new), jnp.exp2(s - m_new).astype(BF16))

        def value(hh):
            alpha, p = probs.pop(hh)
            acc_ref[hh] = alpha * acc_ref[hh] + _dot(vt_ref[0, hh], p)

        softmax(0)
        for hh in range(FOX_HEADS):
            score(hh)
            if hh + 1 < FOX_HEADS:
                softmax(hh + 1)
            value(hh)

    @pl.when((flags & FOX_FLAG_DIAGONAL) == 0)
    def _():
        attend(False)

    @pl.when((flags & FOX_FLAG_DIAGONAL) != 0)
    def _():
        attend(True)

    @pl.when((flags & FOX_FLAG_FINISH) != 0)
    def _():
        for pair in range(FOX_PAIRS):
            out_t = jnp.concatenate([acc_ref[hh, :hd, :] / acc_ref[hh, hd:hd + 1, :]
                                     for hh in (2 * pair, 2 * pair + 1)], axis=0)
            o_ref[0, :, pair_lanes(pair)] = out_t.T
        reset_accumulators()

    @pl.when(step == 0)
    def _():
        reset_accumulators()


def _fox(q, qb, k, kb, v_t, tq):
    bsz, seq, width = q.shape
    nq = seq // tq
    q_idx = [i for i in range(nq) for _ in range(i + 1)]
    k_idx = [j for i in range(nq) for j in range(i + 1)]
    n_pairs = len(q_idx)
    q_new = q_idx + [q_idx[-1]]
    k_new = k_idx + [k_idx[-1]]
    k_old = [0] + k_idx
    q_old = [0] + q_idx
    flags = []
    for n in range(n_pairs + 1):
        f = 0
        if n < n_pairs and k_idx[n] == 0:
            f |= FOX_FLAG_NEW_QUERY
        if n < n_pairs and k_idx[n] == q_idx[n]:
            f |= FOX_FLAG_DIAGONAL
        if n >= 1 and (n == n_pairs or k_idx[n] == 0):
            f |= FOX_FLAG_FINISH
        flags.append(f)
    tables = [jnp.asarray(t, jnp.int32) for t in (q_new, k_new, k_old, q_old, flags)]
    q_spec = pl.BlockSpec((1, tq, width), lambda b, s, qa, ka, kb_, qo, fl: (b, qa[s], 0))
    k_spec = pl.BlockSpec((1, tq, width), lambda b, s, qa, ka, kb_, qo, fl: (b, ka[s], 0))
    grid_spec = pltpu.PrefetchScalarGridSpec(
        num_scalar_prefetch=len(tables),
        grid=(bsz, n_pairs + 1),
        in_specs=[q_spec, q_spec, k_spec, k_spec,
                  pl.BlockSpec((1, FOX_HEADS, FOX_VT_ROWS, tq),
                               lambda b, s, qa, ka, kb_, qo, fl: (b, 0, 0, kb_[s]))],
        out_specs=pl.BlockSpec((1, tq, width), lambda b, s, qa, ka, kb_, qo, fl: (b, qo[s], 0)),
        scratch_shapes=[pltpu.VMEM((FOX_HEADS, tq, 2 * LANES), BF16),
                        pltpu.VMEM((FOX_HEADS, tq, tq), F32),
                        pltpu.VMEM((FOX_HEADS, 1, tq), F32),
                        pltpu.VMEM((FOX_HEADS, 1, tq), F32),
                        pltpu.VMEM((FOX_HEADS, FOX_VT_ROWS, tq), F32)],
    )
    v_heads = v_t.reshape(bsz, FOX_HEADS, FOX_HEAD_DIM, seq)
    ones = jnp.ones((bsz, FOX_HEADS, FOX_VT_ROWS - FOX_HEAD_DIM, seq), v_t.dtype)
    v_ext = jnp.concatenate([v_heads, ones], axis=2)
    return pl.pallas_call(
        _fox_kernel, grid_spec=grid_spec,
        out_shape=jax.ShapeDtypeStruct((bsz, seq, width), F32),
        compiler_params=_params(("parallel", "arbitrary")),
        name="forgetting_attention",
    )(*tables, q, qb, k, kb, v_ext)


def _fox_bias_lanes(split):
    bsz, _, _, seq = split.shape
    lo = 2 * MLSTM_HEADS
    pieces = jnp.transpose(split[:, :, lo:lo + FOX_HEADS, :], (0, 3, 2, 1))
    ones = jnp.ones_like(pieces)

    def lanes(per_head):
        per_pair = per_head.reshape(bsz, seq, FOX_PAIRS, 2 * FOX_BIAS_LANES)
        per_pair = jnp.pad(per_pair, ((0, 0), (0, 0), (0, 0), (0, LANES - 2 * FOX_BIAS_LANES)))
        return per_pair.reshape(bsz, seq, FOX_PAIRS * LANES).astype(BF16)

    return (lanes(jnp.concatenate([pieces, ones], axis=-1)),
            lanes(jnp.concatenate([ones, -pieces], axis=-1)))


FFN_COLS = 256


def _merge_ffn_kernel(x_ref, ys_ref, ym_ref, yf_ref, gate_ref, g1_ref, sh_ref, sc_ref, g2_ref,
                      ws_ref, wm_ref, wf_ref, wo_ref, gain_ref, w1_ref, w3_ref, w2_ref, fg_ref,
                      o_ref, *, final_norm):
    d = x_ref.shape[-1]

    def gate(i):
        return _sigmoid(gate_ref[0, :, i * d:(i + 1) * d].astype(F32))

    merged = (gate(0) * _dot(ys_ref[0].astype(BF16), ws_ref[...])
              + gate(1) * _dot(ym_ref[0].astype(BF16), wm_ref[...])
              + gate(2) * _dot(yf_ref[0].astype(BF16), wf_ref[...]))
    x = x_ref[0] + g1_ref[0] * _dot(merged.astype(BF16), wo_ref[...])

    h = _rms_modulate(x, gain_ref[...], sc_ref[0], sh_ref[0]).astype(BF16)
    starts = list(range(0, w1_ref.shape[1], FFN_COLS))

    def up(c0):
        return _dot(h, w1_ref[:, c0:c0 + FFN_COLS]), _dot(h, w3_ref[:, c0:c0 + FFN_COLS])

    acc = jnp.zeros(x.shape, F32)
    nxt = up(starts[0])
    for i, c0 in enumerate(starts):
        a, b = nxt
        if i + 1 < len(starts):
            nxt = up(starts[i + 1])
        acc = acc + _dot((_silu(a) * b).astype(BF16), w2_ref[c0:c0 + FFN_COLS, :])
    y = x + g2_ref[0] * acc
    if final_norm:
        y = (y * lax.rsqrt(jnp.mean(y * y, axis=-1, keepdims=True) + NORM_EPS)) * fg_ref[...]
    o_ref[0] = y


def _merge_ffn(acts, mod_vectors, weights, final_norm, tm):
    bsz, seq, d = acts[0].shape
    tok = lambda b, i: (b, i, 0)
    vec = pl.BlockSpec((1, 1, d), lambda b, i: (b, 0, 0))
    return pl.pallas_call(
        functools.partial(_merge_ffn_kernel, final_norm=final_norm),
        grid=(bsz, seq // tm),
        in_specs=[pl.BlockSpec((1, tm, a.shape[-1]), tok) for a in acts]
        + [vec] * len(mod_vectors) + [_const_spec(w.shape) for w in weights],
        out_specs=pl.BlockSpec((1, tm, d), tok),
        out_shape=jax.ShapeDtypeStruct(acts[0].shape, F32),
        compiler_params=_params(("parallel", "parallel")),
        name="merge_swiglu",
    )(*acts, *mod_vectors, *weights)


def _pad_heads(w, heads, head_dim):
    lead = w.shape[:-1]
    w = w.reshape(lead + (heads, head_dim))
    w = jnp.pad(w, [(0, 0)] * len(lead) + [(0, 0), (0, LANES - head_dim)])
    return w.reshape(lead + (heads * LANES,))


def _layer_weights(w_in, b_in):
    sizes = (S5_WIDTH, MLSTM_WIDTH, MLSTM_WIDTH, MLSTM_WIDTH, MLSTM_HEADS, MLSTM_HEADS,
             FOX_WIDTH, FOX_WIDTH, FOX_WIDTH, FOX_HEADS)
    offs = [0]
    for n in sizes:
        offs.append(offs[-1] + n)
    both = jnp.concatenate([w_in, b_in[None, :]], axis=0)
    part = [both[:, offs[i]:offs[i + 1]] for i in range(len(sizes))] + [both[:, offs[-1]:]]
    pad_m = lambda w: _pad_heads(w, MLSTM_HEADS, MLSTM_HEAD_DIM)
    cat = jnp.concatenate([part[0], pad_m(part[1]), pad_m(part[3]),
                           part[6], part[7], part[10]], axis=1)
    small = jnp.concatenate([part[4], part[5], part[9]], axis=1)
    small = jnp.pad(small, ((0, 0), (0, GATE_ROWS - small.shape[1])))
    cat_t = jnp.concatenate([small, part[8], pad_m(part[2])], axis=1).T
    return (cat[:-1].astype(BF16), cat[-1:], cat_t[:, :-1].astype(BF16), cat_t[:, -1:])


def kernel(x, c, mod_w, mod_b, norm1_g, norm2_g, w_in, b_in, s5_a_re, s5_a_im, s5_log_dt, s5_b_re, s5_b_im, s5_c_re, s5_c_im, s5_d, s5_w_glu, mlstm_conv_w, mlstm_conv_b, mlstm_wq, mlstm_wk, mlstm_norm_g, w_up_s5, w_up_mlstm, w_up_fox, w_out, ffn_w1, ffn_w3, ffn_w2, final_g):
    bsz, seq, d = x.shape
    depth = mod_w.shape[0]
    assert seq % MLSTM_CHUNK == 0
    tm = _row_tile(seq, 512)
    tq = _row_tile(seq, 512)
    ts = _row_tile(seq, 512)

    mod = _modulation(c, mod_w, mod_b)
    final_gain = final_g.reshape(1, d)
    for l in range(depth):
        sh1, sc1, g1, sh2, sc2, g2 = [mod[l, :, i * d:(i + 1) * d].reshape(bsz, 1, d)
                                      for i in range(6)]
        w_cat, b_cat, w_t, b_t = _layer_weights(w_in[l], b_in[l])
        (s5_u, m_qk, m_o, f_q, f_k, gates, gate_pre, f_vt, m_vt) = _input_projection(
            x, sh1, sc1, norm1_g[l].reshape(1, d), w_cat, b_cat, w_t, b_t, tm)

        gate_rows, gate_split = _gate_scan(gate_pre)
        gate_cols = jnp.swapaxes(gate_rows, 1, 2)
        f_qb, f_kb = _fox_bias_lanes(gate_split)

        bmat, cmat, astep, apow = _s5_tables(s5_a_re[l], s5_a_im[l], s5_log_dt[l], s5_b_re[l],
                                             s5_b_im[l], s5_c_re[l], s5_c_im[l])
        y_s5 = _s5(s5_u, bmat, cmat, astep, apow, s5_d[l].reshape(1, -1),
                   s5_w_glu[l].astype(BF16), ts)

        pad_hh = ((0, 0), (0, LANES - MLSTM_HEAD_DIM), (0, LANES - MLSTM_HEAD_DIM))
        y_m = _mlstm(m_qk, m_vt, m_o, gate_rows, gate_cols,
                     _pad_heads(mlstm_conv_w[l], MLSTM_HEADS, MLSTM_HEAD_DIM),
                     _pad_heads(mlstm_conv_b[l].reshape(1, -1), MLSTM_HEADS, MLSTM_HEAD_DIM),
                     jnp.pad(mlstm_wq[l], pad_hh).astype(BF16),
                     jnp.pad(mlstm_wk[l], pad_hh).astype(BF16),
                     _pad_heads(mlstm_norm_g[l].reshape(1, -1), MLSTM_HEADS, MLSTM_HEAD_DIM))

        y_f = _fox(f_q, f_qb, f_k, f_kb, f_vt, tq)

        w_up_m = _pad_heads(w_up_mlstm[l].T, MLSTM_HEADS, MLSTM_HEAD_DIM).T
        weights = (w_up_s5[l].astype(BF16), w_up_m.astype(BF16), w_up_fox[l].astype(BF16),
                   w_out[l].astype(BF16), norm2_g[l].reshape(1, d), ffn_w1[l].astype(BF16),
                   ffn_w3[l].astype(BF16), ffn_w2[l].astype(BF16), final_gain)
        x = _merge_ffn((x, y_s5, y_m, y_f, gates), (g1, sh2, sc2, g2), weights,
                       l == depth - 1, tm)
    return x
```

```python
import functools
import math

import jax
import jax.numpy as jnp
from jax import lax
from jax.experimental import pallas as pl
from jax.experimental.pallas import tpu as pltpu

F32 = jnp.float32
BF16 = jnp.bfloat16

LANES = 128
SUBLANES = 8
VMEM_LIMIT_BYTES = 56 * 1024 * 1024

NORM_EPS = 1e-6
S5_GROUPS = 16
S5_GROUP_DIM = 16
S5_STATE = 64
S5_WIDTH = S5_GROUPS * S5_GROUP_DIM
S5_CHANNELS = S5_GROUPS * S5_STATE
MLSTM_HEADS = 4
MLSTM_HEAD_DIM = 96
MLSTM_WIDTH = MLSTM_HEADS * MLSTM_HEAD_DIM
MLSTM_PAD_WIDTH = MLSTM_HEADS * LANES
MLSTM_CHUNK = 128
CONV_TAPS = 4
FOX_HEADS = 6
FOX_HEAD_DIM = 64
FOX_WIDTH = FOX_HEADS * FOX_HEAD_DIM
FOX_PAIRS = FOX_WIDTH // LANES
GATE_ROWS = 16
NEG_BIG = -1e30
LOG2_E = math.log2(math.e)

NT_DIMS = (((1,), (1,)), ((), ()))


def _dot(a, b):
    return jnp.dot(a, b, preferred_element_type=F32)


def _dot_nt(a, b):
    return lax.dot_general(a, b, NT_DIMS, preferred_element_type=F32)


def _sigmoid(x):
    return 1.0 / (1.0 + jnp.exp(-x))


def _silu(x):
    return x * _sigmoid(x)


def _log_sigmoid(x):
    return jnp.minimum(x, 0.0) - jnp.log1p(jnp.exp(-jnp.abs(x)))


def _rms_modulate(x, gain, scale, shift):
    y = x * lax.rsqrt(jnp.mean(x * x, axis=-1, keepdims=True) + NORM_EPS)
    return (y * gain) * (1.0 + scale) + shift


def _params(semantics):
    return pltpu.CompilerParams(dimension_semantics=semantics, vmem_limit_bytes=VMEM_LIMIT_BYTES)


def _const_spec(shape):
    nd = len(shape)
    return pl.BlockSpec(shape, lambda *_: (0,) * nd, pipeline_mode=pl.Buffered(1))


def _row_tile(seq, want):
    t = min(want, seq)
    assert seq % t == 0 and t % SUBLANES == 0
    return t


def _mod_kernel(c_ref, w_ref, b_ref, o_ref):
    cf = _silu(c_ref[...]).astype(BF16)
    o_ref[0] = _dot(cf, w_ref[0].astype(BF16)) + b_ref[0]


def _modulation(c, mod_w, mod_b):
    depth, d, six_d = mod_w.shape
    bsz = c.shape[0]
    n_col = six_d // d
    return pl.pallas_call(
        _mod_kernel,
        grid=(depth, n_col),
        in_specs=[pl.BlockSpec((bsz, d), lambda l, j: (0, 0)),
                  pl.BlockSpec((1, d, d), lambda l, j: (l, 0, j)),
                  pl.BlockSpec((1, 1, d), lambda l, j: (l, 0, j))],
        out_specs=pl.BlockSpec((1, bsz, d), lambda l, j: (l, 0, j)),
        out_shape=jax.ShapeDtypeStruct((depth, bsz, six_d), F32),
        compiler_params=_params(("parallel", "parallel")),
        name="adaln_modulation",
    )(c, mod_w, mod_b.reshape(depth, 1, six_d))


IN_SEGMENTS = ((S5_WIDTH, F32), (MLSTM_PAD_WIDTH, F32), (MLSTM_PAD_WIDTH, F32),
               (FOX_WIDTH, F32), (FOX_WIDTH, BF16))
IN_T_SEGMENTS = ((GATE_ROWS, F32), (FOX_WIDTH, BF16), (MLSTM_PAD_WIDTH, F32))
DOT_COLS = 512


def _inproj_kernel(x_ref, sh_ref, sc_ref, g_ref, w_ref, b_ref, wt_ref, bt_ref, *out_refs):
    h = _rms_modulate(x_ref[0], g_ref[...], sc_ref[0], sh_ref[0]).astype(BF16)
    n_t = len(IN_T_SEGMENTS)
    start = 0
    for o_ref in out_refs[:-n_t]:
        width = o_ref.shape[-1]
        for c0 in range(0, width, DOT_COLS):
            cw = min(DOT_COLS, width - c0)
            z = _dot(h, w_ref[:, start + c0:start + c0 + cw]) + b_ref[:, start + c0:start + c0 + cw]
            o_ref[0, :, c0:c0 + cw] = z.astype(o_ref.dtype)
        start += width
    start = 0
    for o_ref in out_refs[-n_t:]:
        rows = o_ref.shape[1]
        z = _dot_nt(wt_ref[start:start + rows, :], h) + bt_ref[start:start + rows, :]
        o_ref[0] = z.astype(o_ref.dtype)
        start += rows


def _input_projection(x, shift, scale, gain, w_cat, b_cat, w_t, b_t, tm):
    bsz, seq, d = x.shape
    segs = IN_SEGMENTS + ((w_cat.shape[1] - sum(w for w, _ in IN_SEGMENTS), BF16),)
    tok = lambda b, i: (b, i, 0)
    vec = lambda b, i: (b, 0, 0)
    out_shape = [jax.ShapeDtypeStruct((bsz, seq, w), dt) for w, dt in segs]
    out_shape += [jax.ShapeDtypeStruct((bsz, r, seq), dt) for r, dt in IN_T_SEGMENTS]
    out_specs = [pl.BlockSpec((1, tm, w), tok) for w, _ in segs]
    out_specs += [pl.BlockSpec((1, r, tm), lambda b, i: (b, 0, i)) for r, _ in IN_T_SEGMENTS]
    return pl.pallas_call(
        _inproj_kernel,
        grid=(bsz, seq // tm),
        in_specs=[pl.BlockSpec((1, tm, d), tok),
                  pl.BlockSpec((1, 1, d), vec), pl.BlockSpec((1, 1, d), vec),
                  _const_spec(gain.shape), _const_spec(w_cat.shape), _const_spec(b_cat.shape),
                  _const_spec(w_t.shape), _const_spec(b_t.shape)],
        out_specs=out_specs,
        out_shape=out_shape,
        compiler_params=_params(("parallel", "parallel")),
        name="norm_input_projection",
    )(x, shift, scale, gain, w_cat, b_cat, w_t, b_t)


def _bf16_round(x):
    return x.astype(BF16).astype(F32)


def _gate_scan_kernel(x_ref, o_ref, split_ref):
    x = x_ref[0]
    seq = x.shape[1]
    seg = MLSTM_CHUNK
    lane = lax.broadcasted_iota(jnp.int32, x.shape, 1) & (seg - 1)
    v = _log_sigmoid(x)
    k = 1
    while k < seg:
        v = v + jnp.where(lane >= k, pltpu.roll(v, k, 1), 0.0)
        k *= 2
    row = lax.broadcasted_iota(jnp.int32, (GATE_ROWS, seg), 0)
    carry = jnp.zeros((GATE_ROWS, 1), F32)
    for j in range(seq // seg):
        sl = slice(j * seg, (j + 1) * seg)
        local = v[:, sl]
        total = local + carry
        o_ref[0, :, sl] = jnp.where(row < MLSTM_HEADS, x[:, sl],
                                    jnp.where(row < 2 * MLSTM_HEADS, local, total))
        scaled = total * LOG2_E
        hi = _bf16_round(scaled)
        mid = _bf16_round(scaled - hi)
        split_ref[0, 0, :, sl] = hi
        split_ref[0, 1, :, sl] = mid
        split_ref[0, 2, :, sl] = _bf16_round((scaled - hi) - mid)
        carry = carry + local[:, seg - 1:seg]


def _gate_scan(pre):
    bsz, rows, seq = pre.shape
    spec = pl.BlockSpec((1, rows, seq), lambda b: (b, 0, 0))
    return pl.pallas_call(
        _gate_scan_kernel, grid=(bsz,), in_specs=[spec],
        out_specs=[spec, pl.BlockSpec((1, 3, rows, seq), lambda b: (b, 0, 0, 0))],
        out_shape=[jax.ShapeDtypeStruct(pre.shape, F32),
                   jax.ShapeDtypeStruct((bsz, 3, rows, seq), F32)],
        compiler_params=_params(("parallel",)),
        name="gate_scan",
    )(pre)


SCAN_SHIFTS = (1, 2, 4)


def _s5_kernel(u_ref, bmat_ref, cmat_ref, astep_ref, apow_ref, d_ref, wglu_ref, o_ref,
               st_ref, carry_ref):
    nch = S5_CHANNELS

    @pl.when(pl.program_id(1) == 0)
    def _():
        carry_ref[...] = jnp.zeros_like(carry_ref)

    u = u_ref[0]
    st_ref[...] = _dot(u.astype(BF16), bmat_ref[...])
    n_blocks = u.shape[0] // SUBLANES

    def body(i, carry):
        cr, ci = carry
        r0 = pl.multiple_of(i * SUBLANES, SUBLANES)
        xr = st_ref[pl.ds(r0, SUBLANES), :nch]
        xi = st_ref[pl.ds(r0, SUBLANES), nch:]
        for k, shift in enumerate(SCAN_SHIFTS):
            ar, ai = astep_ref[k, :, :nch], astep_ref[k, :, nch:]
            sr, si = pltpu.roll(xr, shift, 0), pltpu.roll(xi, shift, 0)
            xr, xi = xr + (ar * sr - ai * si), xi + (ar * si + ai * sr)
        pr, pi = apow_ref[:, :nch], apow_ref[:, nch:]
        xr, xi = xr + (pr * cr - pi * ci), xi + (pr * ci + pi * cr)
        st_ref[pl.ds(r0, SUBLANES), :nch] = xr
        st_ref[pl.ds(r0, SUBLANES), nch:] = xi
        last = SUBLANES - 1
        return (jnp.broadcast_to(xr[last:, :], xr.shape), jnp.broadcast_to(xi[last:, :], xi.shape))

    cr, ci = lax.fori_loop(0, n_blocks, body, (carry_ref[:, :nch], carry_ref[:, nch:]))
    carry_ref[:, :nch] = cr
    carry_ref[:, nch:] = ci

    y = _dot(st_ref[...].astype(BF16), cmat_ref[...]) + d_ref[...] * u
    y = y * (0.5 * (1.0 + jnp.tanh(math.sqrt(2.0 / math.pi) * (y + 0.044715 * (y * y * y)))))
    o_ref[0] = y * _sigmoid(_dot(y.astype(BF16), wglu_ref[...]))


def _s5(u, bmat, cmat, astep, apow, d_skip, w_glu, tb):
    bsz, seq, width = u.shape
    tok = lambda b, i: (b, i, 0)
    return pl.pallas_call(
        _s5_kernel,
        grid=(bsz, seq // tb),
        in_specs=[pl.BlockSpec((1, tb, width), tok)] + [
            _const_spec(a.shape) for a in (bmat, cmat, astep, apow, d_skip, w_glu)],
        out_specs=pl.BlockSpec((1, tb, width), tok),
        out_shape=jax.ShapeDtypeStruct(u.shape, F32),
        scratch_shapes=[pltpu.VMEM((tb, 2 * S5_CHANNELS), F32),
                        pltpu.VMEM((SUBLANES, 2 * S5_CHANNELS), F32)],
        compiler_params=_params(("parallel", "arbitrary")),
        name="s5_scan",
    )(u, bmat, cmat, astep, apow, d_skip, w_glu)


def _s5_tables(a_re, a_im, log_dt, b_re, b_im, c_re, c_im):
    g, n, p = b_re.shape
    dt = jnp.exp(log_dt)[:, None]
    mag = jnp.exp(dt * a_re)
    ang = dt * a_im
    abar_r, abar_i = mag * jnp.cos(ang), mag * jnp.sin(ang)
    den = a_re * a_re + a_im * a_im
    pr, qi = abar_r - 1.0, abar_i
    coef_r = (pr * a_re + qi * a_im) / den
    coef_i = (qi * a_re - pr * a_im) / den
    bbar_r = coef_r[..., None] * b_re - coef_i[..., None] * b_im
    bbar_i = coef_r[..., None] * b_im + coef_i[..., None] * b_re
    eye = jnp.eye(g, dtype=F32)

    def in_block(bb):
        return jnp.einsum('gnp,gh->gphn', bb, eye).reshape(g * p, g * n)

    def out_block(cc):
        return jnp.einsum('gpn,gh->gnhp', cc, eye).reshape(g * n, g * p)

    bmat = jnp.concatenate([in_block(bbar_r), in_block(bbar_i)], axis=1).astype(BF16)
    cmat = jnp.concatenate([out_block(c_re), -out_block(c_im)], axis=0).astype(BF16)

    def power(e):
        e = jnp.asarray(e, F32)[:, None, None]
        m = jnp.exp(e * (dt * a_re)[None])
        th = e * ang[None]
        return (m * jnp.cos(th)).reshape(-1, g * n), (m * jnp.sin(th)).reshape(-1, g * n)

    rows = jnp.arange(SUBLANES)
    sr, si = power(SCAN_SHIFTS)
    keep = (rows[None, :] >= jnp.asarray(SCAN_SHIFTS)[:, None]).astype(F32)[:, :, None]
    astep = jnp.concatenate([keep * sr[:, None, :], keep * si[:, None, :]], axis=-1)
    wr, wi = power(rows + 1)
    apow = jnp.concatenate([wr, wi], axis=-1)
    return bmat, cmat, astep, apow


def _mlstm_kernel(qk_ref, vt_ref, o_ref, grow_ref, gcol_ref, cw_ref, cb_ref, wq_ref, wk_ref,
                  ng_ref, out_ref, tail_ref, c_ref, n_ref, m_ref):
    chunk = qk_ref.shape[1]
    nh = MLSTM_HEADS

    @pl.when(pl.program_id(1) == 0)
    def _():
        tail_ref[...] = jnp.zeros_like(tail_ref)
        c_ref[...] = jnp.zeros_like(c_ref)
        n_ref[...] = jnp.zeros_like(n_ref)
        m_ref[...] = jnp.zeros_like(m_ref)

    x = qk_ref[0]
    ext = jnp.concatenate([tail_ref[...], x], axis=0)
    conv = cb_ref[...] + cw_ref[CONV_TAPS - 1:CONV_TAPS, :] * x
    for tap in range(CONV_TAPS - 1):
        back = CONV_TAPS - 1 - tap
        conv = conv + cw_ref[tap:tap + 1, :] * ext[SUBLANES - back:SUBLANES - back + chunk, :]
    tail_ref[...] = x[chunk - SUBLANES:, :]
    cx = _silu(conv)

    key_id = lax.broadcasted_iota(jnp.int32, (chunk, chunk), 0)
    query_id = lax.broadcasted_iota(jnp.int32, (chunk, chunk), 1)
    causal = key_id <= query_id
    k_scale = MLSTM_HEAD_DIM ** -0.5
    heads = range(nh)
    cols = [slice(hd * LANES, (hd + 1) * LANES) for hd in heads]
    cxh = [cx[:, cols[hd]].astype(BF16) for hd in heads]
    q = [_dot(cxh[hd], wq_ref[hd]).astype(BF16) for hd in heads]
    k = [(_dot(cxh[hd], wk_ref[hd]) * k_scale).astype(BF16) for hd in heads]
    v_t = [vt_ref[0, cols[hd], :] for hd in heads]
    kq = [_dot_nt(k[hd], q[hd]) for hd in heads]
    carry = [_dot_nt(c_ref[hd].astype(BF16), q[hd]) for hd in heads]
    n_q = [_dot_nt(n_ref[...].astype(BF16), q[hd]) for hd in heads]

    s, inter, m_t, stats = [], [], [], []
    for hd in heads:
        li_row = grow_ref[0, hd:hd + 1, :]
        b_row = grow_ref[0, nh + hd:nh + hd + 1, :]
        d_col = gcol_ref[0, :, hd:hd + 1] - gcol_ref[0, :, nh + hd:nh + hd + 1]
        m_st = m_ref[hd:hd + 1, 0:1]
        dmat = jnp.where(causal, b_row + d_col, -jnp.inf)
        m_inter = b_row + m_st
        m_t.append(jnp.maximum(m_inter, jnp.max(dmat, axis=0, keepdims=True)))
        s.append(kq[hd] * jnp.exp(dmat - m_t[hd]))
        inter.append(jnp.exp(m_inter - m_t[hd]))

        b_last = b_row[:, chunk - 1:chunk]
        w_row = b_last - b_row + li_row
        m_new = jnp.maximum(b_last + m_st, jnp.max(w_row, axis=1, keepdims=True))
        stats.append((m_new, jnp.exp(b_last + m_st - m_new), jnp.exp(w_row - m_new)))

    sv = [_dot(v_t[hd].astype(BF16), s[hd].astype(BF16)) for hd in heads]
    vk = [_dot((v_t[hd] * stats[hd][2]).astype(BF16), k[hd]) for hd in heads]
    row8 = lax.broadcasted_iota(jnp.int32, (SUBLANES, chunk), 0)
    ws_rows = jnp.zeros((SUBLANES, chunk), F32)
    for hd in heads:
        ws_rows = jnp.where(row8 == hd, stats[hd][2], ws_rows)
    ws_k = [_dot(ws_rows.astype(BF16), k[hd]) for hd in heads]

    for hd in heads:
        m_new, decay, _ = stats[hd]
        num = sv[hd] + inter[hd] * carry[hd]
        den = jnp.sum(s[hd], axis=0, keepdims=True) + inter[hd] * n_q[hd][hd:hd + 1, :]
        h = num * (1.0 / jnp.maximum(jnp.abs(den), jnp.exp(-m_t[hd])))
        c_ref[hd] = decay * c_ref[hd] + vk[hd]
        n_ref[hd:hd + 1, :] = decay * n_ref[hd:hd + 1, :] + ws_k[hd][hd:hd + 1, :]
        m_ref[hd:hd + 1, :] = jnp.broadcast_to(m_new, (1, LANES))
        hn = h * lax.rsqrt(jnp.sum(h * h, axis=0, keepdims=True) * (1.0 / MLSTM_HEAD_DIM) + NORM_EPS)
        out_ref[0, :, cols[hd]] = _sigmoid(o_ref[0, :, cols[hd]]) * (hn.T * ng_ref[:, cols[hd]])


def _mlstm(qk, v_t, o, gate_rows, gate_cols, conv_w, conv_b, wq, wk, norm_g):
    bsz, seq, width = qk.shape
    chunk = MLSTM_CHUNK
    tok = lambda b, i: (b, i, 0)
    tok_spec = pl.BlockSpec((1, chunk, width), tok)
    return pl.pallas_call(
        _mlstm_kernel,
        grid=(bsz, seq // chunk),
        in_specs=[tok_spec, pl.BlockSpec((1, width, chunk), lambda b, i: (b, 0, i)), tok_spec,
                  pl.BlockSpec((1, GATE_ROWS, chunk), lambda b, i: (b, 0, i)),
                  pl.BlockSpec((1, chunk, GATE_ROWS), tok)] + [
            _const_spec(a.shape) for a in (conv_w, conv_b, wq, wk, norm_g)],
        out_specs=tok_spec,
        out_shape=jax.ShapeDtypeStruct(qk.shape, F32),
        scratch_shapes=[pltpu.VMEM((SUBLANES, width), F32),
                        pltpu.VMEM((MLSTM_HEADS, LANES, LANES), F32),
                        pltpu.VMEM((SUBLANES, LANES), F32),
                        pltpu.VMEM((SUBLANES, LANES), F32)],
        compiler_params=_params(("parallel", "arbitrary")),
        name="mlstm_chunkwise",
    )(qk, v_t, o, gate_rows, gate_cols, conv_w, conv_b, wq, wk, norm_g)


FOX_BIAS_LANES = 6
FOX_VT_ROWS = FOX_HEAD_DIM + 16
FOX_FLAG_NEW_QUERY, FOX_FLAG_DIAGONAL, FOX_FLAG_FINISH = 1, 2, 4


def _fox_kernel(qa_ref, ka_ref, kb_ref, qo_ref, flag_ref, q_ref, qbias_ref, k_ref, kbias_ref,
                vt_ref, o_ref, qx_ref, s_ref, smax_ref, m_ref, acc_ref):
    step = pl.program_id(1)
    flags = flag_ref[step]
    tq, tk = q_ref.shape[1], k_ref.shape[1]
    hd = FOX_HEAD_DIM

    def pair_lanes(pair):
        return slice(pair * LANES, (pair + 1) * LANES)

    def reset_accumulators():
        m_ref[...] = jnp.full_like(m_ref, NEG_BIG)
        acc_ref[...] = jnp.zeros_like(acc_ref)

    @pl.when(step == 0)
    def _():
        s_ref[...] = jnp.full(s_ref.shape, NEG_BIG, F32)
        smax_ref[...] = jnp.full(smax_ref.shape, NEG_BIG, F32)
        reset_accumulators()

    @pl.when((flags & FOX_FLAG_NEW_QUERY) != 0)
    def _():
        lane = lax.broadcasted_iota(jnp.int32, (1, LANES), 1)
        for pair in range(FOX_PAIRS):
            q = q_ref[0, :, pair_lanes(pair)] * (hd ** -0.5 * LOG2_E)
            qb = qbias_ref[0, :, pair_lanes(pair)]
            for hh in range(2):
                in_head = (lane >= hh * hd) & (lane < (hh + 1) * hd)
                in_bias = (lane >= hh * FOX_BIAS_LANES) & (lane < (hh + 1) * FOX_BIAS_LANES)
                qx_ref[2 * pair + hh, :, :LANES] = jnp.where(in_head, q, 0.0).astype(BF16)
                qx_ref[2 * pair + hh, :, LANES:] = jnp.where(in_bias, qb, jnp.zeros_like(qb))

    def attend(diagonal):
        kx = [jnp.concatenate([k_ref[0, :, pair_lanes(pair)], kbias_ref[0, :, pair_lanes(pair)]],
                              axis=1) for pair in range(FOX_PAIRS)]
        probs = {}

        def score(hh):
            s = _dot_nt(kx[hh // 2], qx_ref[hh])
            if diagonal:
                visible = (lax.broadcasted_iota(jnp.int32, (tk, tq), 0)
                           <= lax.broadcasted_iota(jnp.int32, (tk, tq), 1))
                s = jnp.where(visible, s, NEG_BIG)
            s_ref[hh] = s
            smax_ref[hh] = jnp.max(s, axis=0, keepdims=True)

        def softmax(hh):
            s = s_ref[hh]
            m_prev = m_ref[hh]
            m_new = jnp.maximum(m_prev, smax_ref[hh])
            m_ref[hh] = m_new
            probs[hh] = (jnp.exp2(m_prev - m_new), jnp.exp2(s - m_new).astype(BF16))

        def value(hh):
            alpha, p = probs.pop(hh)
            acc_ref[hh] = alpha * acc_ref[hh] + _dot(vt_ref[0, hh], p)

        softmax(0)
        for hh in range(FOX_HEADS):
            score(hh)
            if hh + 1 < FOX_HEADS:
                softmax(hh + 1)
            value(hh)

    @pl.when((flags & FOX_FLAG_DIAGONAL) == 0)
    def _():
        attend(False)

    @pl.when((flags & FOX_FLAG_DIAGONAL) != 0)
    def _():
        attend(True)

    @pl.when((flags & FOX_FLAG_FINISH) != 0)
    def _():
        for pair in range(FOX_PAIRS):
            out_t = jnp.concatenate([acc_ref[hh, :hd, :] / acc_ref[hh, hd:hd + 1, :]
                                     for hh in (2 * pair, 2 * pair + 1)], axis=0)
            o_ref[0, :, pair_lanes(pair)] = out_t.T
        reset_accumulators()

    @pl.when(step == 0)
    def _():
        reset_accumulators()


def _fox(q, qb, k, kb, v_t, tq):
    bsz, seq, width = q.shape
    nq = seq // tq
    q_idx = [i for i in range(nq) for _ in range(i + 1)]
    k_idx = [j for i in range(nq) for j in range(i + 1)]
    n_pairs = len(q_idx)
    q_new = q_idx + [q_idx[-1]]
    k_new = k_idx + [k_idx[-1]]
    k_old = [0] + k_idx
    q_old = [0] + q_idx
    flags = []
    for n in range(n_pairs + 1):
        f = 0
        if n < n_pairs and k_idx[n] == 0:
            f |= FOX_FLAG_NEW_QUERY
        if n < n_pairs and k_idx[n] == q_idx[n]:
            f |= FOX_FLAG_DIAGONAL
        if n >= 1 and (n == n_pairs or k_idx[n] == 0):
            f |= FOX_FLAG_FINISH
        flags.append(f)
    tables = [jnp.asarray(t, jnp.int32) for t in (q_new, k_new, k_old, q_old, flags)]
    q_spec = pl.BlockSpec((1, tq, width), lambda b, s, qa, ka, kb_, qo, fl: (b, qa[s], 0))
    k_spec = pl.BlockSpec((1, tq, width), lambda b, s, qa, ka, kb_, qo, fl: (b, ka[s], 0))
    grid_spec = pltpu.PrefetchScalarGridSpec(
        num_scalar_prefetch=len(tables),
        grid=(bsz, n_pairs + 1),
        in_specs=[q_spec, q_spec, k_spec, k_spec,
                  pl.BlockSpec((1, FOX_HEADS, FOX_VT_ROWS, tq),
                               lambda b, s, qa, ka, kb_, qo, fl: (b, 0, 0, kb_[s]))],
        out_specs=pl.BlockSpec((1, tq, width), lambda b, s, qa, ka, kb_, qo, fl: (b, qo[s], 0)),
        scratch_shapes=[pltpu.VMEM((FOX_HEADS, tq, 2 * LANES), BF16),
                        pltpu.VMEM((FOX_HEADS, tq, tq), F32),
                        pltpu.VMEM((FOX_HEADS, 1, tq), F32),
                        pltpu.VMEM((FOX_HEADS, 1, tq), F32),
                        pltpu.VMEM((FOX_HEADS, FOX_VT_ROWS, tq), F32)],
    )
    v_heads = v_t.reshape(bsz, FOX_HEADS, FOX_HEAD_DIM, seq)
    ones = jnp.ones((bsz, FOX_HEADS, FOX_VT_ROWS - FOX_HEAD_DIM, seq), v_t.dtype)
    v_ext = jnp.concatenate([v_heads, ones], axis=2)
    return pl.pallas_call(
        _fox_kernel, grid_spec=grid_spec,
        out_shape=jax.ShapeDtypeStruct((bsz, seq, width), F32),
        compiler_params=_params(("parallel", "arbitrary")),
        name="forgetting_attention",
    )(*tables, q, qb, k, kb, v_ext)


def _fox_bias_lanes(split):
    bsz, _, _, seq = split.shape
    lo = 2 * MLSTM_HEADS
    pieces = jnp.transpose(split[:, :, lo:lo + FOX_HEADS, :], (0, 3, 2, 1))
    ones = jnp.ones_like(pieces)

    def lanes(per_head):
        per_pair = per_head.reshape(bsz, seq, FOX_PAIRS, 2 * FOX_BIAS_LANES)
        per_pair = jnp.pad(per_pair, ((0, 0), (0, 0), (0, 0), (0, LANES - 2 * FOX_BIAS_LANES)))
        return per_pair.reshape(bsz, seq, FOX_PAIRS * LANES).astype(BF16)

    return (lanes(jnp.concatenate([pieces, ones], axis=-1)),
            lanes(jnp.concatenate([ones, -pieces], axis=-1)))


FFN_COLS = 256


def _merge_ffn_kernel(x_ref, ys_ref, ym_ref, yf_ref, gate_ref, g1_ref, sh_ref, sc_ref, g2_ref,
                      ws_ref, wm_ref, wf_ref, wo_ref, gain_ref, w1_ref, w3_ref, w2_ref, fg_ref,
                      o_ref, *, final_norm):
    d = x_ref.shape[-1]

    def gate(i):
        return _sigmoid(gate_ref[0, :, i * d:(i + 1) * d].astype(F32))

    merged = (gate(0) * _dot(ys_ref[0].astype(BF16), ws_ref[...])
              + gate(1) * _dot(ym_ref[0].astype(BF16), wm_ref[...])
              + gate(2) * _dot(yf_ref[0].astype(BF16), wf_ref[...]))
    x = x_ref[0] + g1_ref[0] * _dot(merged.astype(BF16), wo_ref[...])

    h = _rms_modulate(x, gain_ref[...], sc_ref[0], sh_ref[0]).astype(BF16)
    starts = list(range(0, w1_ref.shape[1], FFN_COLS))

    def up(c0):
        return _dot(h, w1_ref[:, c0:c0 + FFN_COLS]), _dot(h, w3_ref[:, c0:c0 + FFN_COLS])

    acc = jnp.zeros(x.shape, F32)
    nxt = up(starts[0])
    for i, c0 in enumerate(starts):
        a, b = nxt
        if i + 1 < len(starts):
            nxt = up(starts[i + 1])
        acc = acc + _dot((_silu(a) * b).astype(BF16), w2_ref[c0:c0 + FFN_COLS, :])
    y = x + g2_ref[0] * acc
    if final_norm:
        y = (y * lax.rsqrt(jnp.mean(y * y, axis=-1, keepdims=True) + NORM_EPS)) * fg_ref[...]
    o_ref[0] = y


def _merge_ffn(acts, mod_vectors, weights, final_norm, tm):
    bsz, seq, d = acts[0].shape
    tok = lambda b, i: (b, i, 0)
    vec = pl.BlockSpec((1, 1, d), lambda b, i: (b, 0, 0))
    return pl.pallas_call(
        functools.partial(_merge_ffn_kernel, final_norm=final_norm),
        grid=(bsz, seq // tm),
        in_specs=[pl.BlockSpec((1, tm, a.shape[-1]), tok) for a in acts]
        + [vec] * len(mod_vectors) + [_const_spec(w.shape) for w in weights],
        out_specs=pl.BlockSpec((1, tm, d), tok),
        out_shape=jax.ShapeDtypeStruct(acts[0].shape, F32),
        compiler_params=_params(("parallel", "parallel")),
        name="merge_swiglu",
    )(*acts, *mod_vectors, *weights)


def _pad_heads(w, heads, head_dim):
    lead = w.shape[:-1]
    w = w.reshape(lead + (heads, head_dim))
    w = jnp.pad(w, [(0, 0)] * len(lead) + [(0, 0), (0, LANES - head_dim)])
    return w.reshape(lead + (heads * LANES,))


def _layer_weights(w_in, b_in):
    sizes = (S5_WIDTH, MLSTM_WIDTH, MLSTM_WIDTH, MLSTM_WIDTH, MLSTM_HEADS, MLSTM_HEADS,
             FOX_WIDTH, FOX_WIDTH, FOX_WIDTH, FOX_HEADS)
    offs = [0]
    for n in sizes:
        offs.append(offs[-1] + n)
    both = jnp.concatenate([w_in, b_in[None, :]], axis=0)
    part = [both[:, offs[i]:offs[i + 1]] for i in range(len(sizes))] + [both[:, offs[-1]:]]
    pad_m = lambda w: _pad_heads(w, MLSTM_HEADS, MLSTM_HEAD_DIM)
    cat = jnp.concatenate([part[0], pad_m(part[1]), pad_m(part[3]),
                           part[6], part[7], part[10]], axis=1)
    small = jnp.concatenate([part[4], part[5], part[9]], axis=1)
    small = jnp.pad(small, ((0, 0), (0, GATE_ROWS - small.shape[1])))
    cat_t = jnp.concatenate([small, part[8], pad_m(part[2])], axis=1).T
    return (cat[:-1].astype(BF16), cat[-1:], cat_t[:, :-1].astype(BF16), cat_t[:, -1:])


def kernel(x, c, mod_w, mod_b, norm1_g, norm2_g, w_in, b_in, s5_a_re, s5_a_im, s5_log_dt, s5_b_re, s5_b_im, s5_c_re, s5_c_im, s5_d, s5_w_glu, mlstm_conv_w, mlstm_conv_b, mlstm_wq, mlstm_wk, mlstm_norm_g, w_up_s5, w_up_mlstm, w_up_fox, w_out, ffn_w1, ffn_w3, ffn_w2, final_g):
    bsz, seq, d = x.shape
    depth = mod_w.shape[0]
    assert seq % MLSTM_CHUNK == 0
    tm = _row_tile(seq, 512)
    tq = _row_tile(seq, 1024)
    ts = _row_tile(seq, 512)

    mod = _modulation(c, mod_w, mod_b)
    final_gain = final_g.reshape(1, d)
    for l in range(depth):
        sh1, sc1, g1, sh2, sc2, g2 = [mod[l, :, i * d:(i + 1) * d].reshape(bsz, 1, d)
                                      for i in range(6)]
        w_cat, b_cat, w_t, b_t = _layer_weights(w_in[l], b_in[l])
        (s5_u, m_qk, m_o, f_q, f_k, gates, gate_pre, f_vt, m_vt) = _input_projection(
            x, sh1, sc1, norm1_g[l].reshape(1, d), w_cat, b_cat, w_t, b_t, tm)

        gate_rows, gate_split = _gate_scan(gate_pre)
        gate_cols = jnp.swapaxes(gate_rows, 1, 2)
        f_qb, f_kb = _fox_bias_lanes(gate_split)

        bmat, cmat, astep, apow = _s5_tables(s5_a_re[l], s5_a_im[l], s5_log_dt[l], s5_b_re[l],
                                             s5_b_im[l], s5_c_re[l], s5_c_im[l])
        y_s5 = _s5(s5_u, bmat, cmat, astep, apow, s5_d[l].reshape(1, -1),
                   s5_w_glu[l].astype(BF16), ts)

        pad_hh = ((0, 0), (0, LANES - MLSTM_HEAD_DIM), (0, LANES - MLSTM_HEAD_DIM))
        y_m = _mlstm(m_qk, m_vt, m_o, gate_rows, gate_cols,
                     _pad_heads(mlstm_conv_w[l], MLSTM_HEADS, MLSTM_HEAD_DIM),
                     _pad_heads(mlstm_conv_b[l].reshape(1, -1), MLSTM_HEADS, MLSTM_HEAD_DIM),
                     jnp.pad(mlstm_wq[l], pad_hh).astype(BF16),
                     jnp.pad(mlstm_wk[l], pad_hh).astype(BF16),
                     _pad_heads(mlstm_norm_g[l].reshape(1, -1), MLSTM_HEADS, MLSTM_HEAD_DIM))

        y_f = _fox(f_q, f_qb, f_k, f_kb, f_vt, tq)

        w_up_m = _pad_heads(w_up_mlstm[l].T, MLSTM_HEADS, MLSTM_HEAD_DIM).T
        weights = (w_up_s5[l].astype(BF16), w_up_m.astype(BF16), w_up_fox[l].astype(BF16),
                   w_out[l].astype(BF16), norm2_g[l].reshape(1, d), ffn_w1[l].astype(BF16),
                   ffn_w3[l].astype(BF16), ffn_w2[l].astype(BF16), final_gain)
        x = _merge_ffn((x, y_s5, y_m, y_f, gates), (g1, sh2, sc2, g2), weights,
                       l == depth - 1, tm)
    return x
```

```python
import functools
import math

import jax
import jax.numpy as jnp
from jax import lax
from jax.experimental import pallas as pl
from jax.experimental.pallas import tpu as pltpu

F32 = jnp.float32
BF16 = jnp.bfloat16

LANES = 128
SUBLANES = 8
VMEM_LIMIT_BYTES = 56 * 1024 * 1024

NORM_EPS = 1e-6
S5_GROUPS = 16
S5_GROUP_DIM = 16
S5_STATE = 64
S5_WIDTH = S5_GROUPS * S5_GROUP_DIM
S5_CHANNELS = S5_GROUPS * S5_STATE
MLSTM_HEADS = 4
MLSTM_HEAD_DIM = 96
MLSTM_WIDTH = MLSTM_HEADS * MLSTM_HEAD_DIM
MLSTM_PAD_WIDTH = MLSTM_HEADS * LANES
MLSTM_CHUNK = 128
CONV_TAPS = 4
FOX_HEADS = 6
FOX_HEAD_DIM = 64
FOX_WIDTH = FOX_HEADS * FOX_HEAD_DIM
FOX_PAIRS = FOX_WIDTH // LANES
GATE_ROWS = 16
NEG_BIG = -1e30
LOG2_E = math.log2(math.e)

NT_DIMS = (((1,), (1,)), ((), ()))


def _dot(a, b):
    return jnp.dot(a, b, preferred_element_type=F32)


def _dot_nt(a, b):
    return lax.dot_general(a, b, NT_DIMS, preferred_element_type=F32)


def _sigmoid(x):
    return 1.0 / (1.0 + jnp.exp(-x))


def _silu(x):
    return x * _sigmoid(x)


def _log_sigmoid(x):
    return jnp.minimum(x, 0.0) - jnp.log1p(jnp.exp(-jnp.abs(x)))


def _rms_modulate(x, gain, scale, shift):
    y = x * lax.rsqrt(jnp.mean(x * x, axis=-1, keepdims=True) + NORM_EPS)
    return (y * gain) * (1.0 + scale) + shift


def _params(semantics):
    return pltpu.CompilerParams(dimension_semantics=semantics, vmem_limit_bytes=VMEM_LIMIT_BYTES)


def _const_spec(shape):
    nd = len(shape)
    return pl.BlockSpec(shape, lambda *_: (0,) * nd, pipeline_mode=pl.Buffered(1))


def _row_tile(seq, want):
    t = min(want, seq)
    assert seq % t == 0 and t % SUBLANES == 0
    return t


def _mod_kernel(c_ref, w_ref, b_ref, o_ref):
    cf = _silu(c_ref[...]).astype(BF16)
    o_ref[0] = _dot(cf, w_ref[0].astype(BF16)) + b_ref[0]


def _modulation(c, mod_w, mod_b):
    depth, d, six_d = mod_w.shape
    bsz = c.shape[0]
    n_col = six_d // d
    return pl.pallas_call(
        _mod_kernel,
        grid=(depth, n_col),
        in_specs=[pl.BlockSpec((bsz, d), lambda l, j: (0, 0)),
                  pl.BlockSpec((1, d, d), lambda l, j: (l, 0, j)),
                  pl.BlockSpec((1, 1, d), lambda l, j: (l, 0, j))],
        out_specs=pl.BlockSpec((1, bsz, d), lambda l, j: (l, 0, j)),
        out_shape=jax.ShapeDtypeStruct((depth, bsz, six_d), F32),
        compiler_params=_params(("parallel", "parallel")),
        name="adaln_modulation",
    )(c, mod_w, mod_b.reshape(depth, 1, six_d))


IN_SEGMENTS = ((S5_WIDTH, F32), (MLSTM_PAD_WIDTH, F32), (MLSTM_PAD_WIDTH, F32),
               (FOX_WIDTH, F32), (FOX_WIDTH, BF16))
IN_T_SEGMENTS = ((GATE_ROWS, F32), (FOX_WIDTH, BF16), (MLSTM_PAD_WIDTH, F32))
DOT_COLS = 512


def _inproj_kernel(x_ref, sh_ref, sc_ref, g_ref, w_ref, b_ref, wt_ref, bt_ref, *out_refs):
    h = _rms_modulate(x_ref[0], g_ref[...], sc_ref[0], sh_ref[0]).astype(BF16)
    n_t = len(IN_T_SEGMENTS)
    start = 0
    for o_ref in out_refs[:-n_t]:
        width = o_ref.shape[-1]
        for c0 in range(0, width, DOT_COLS):
            cw = min(DOT_COLS, width - c0)
            z = _dot(h, w_ref[:, start + c0:start + c0 + cw]) + b_ref[:, start + c0:start + c0 + cw]
            o_ref[0, :, c0:c0 + cw] = z.astype(o_ref.dtype)
        start += width
    start = 0
    for o_ref in out_refs[-n_t:]:
        rows = o_ref.shape[1]
        z = _dot_nt(wt_ref[start:start + rows, :], h) + bt_ref[start:start + rows, :]
        o_ref[0] = z.astype(o_ref.dtype)
        start += rows


def _input_projection(x, shift, scale, gain, w_cat, b_cat, w_t, b_t, tm):
    bsz, seq, d = x.shape
    segs = IN_SEGMENTS + ((w_cat.shape[1] - sum(w for w, _ in IN_SEGMENTS), BF16),)
    tok = lambda b, i: (b, i, 0)
    vec = lambda b, i: (b, 0, 0)
    out_shape = [jax.ShapeDtypeStruct((bsz, seq, w), dt) for w, dt in segs]
    out_shape += [jax.ShapeDtypeStruct((bsz, r, seq), dt) for r, dt in IN_T_SEGMENTS]
    out_specs = [pl.BlockSpec((1, tm, w), tok) for w, _ in segs]
    out_specs += [pl.BlockSpec((1, r, tm), lambda b, i: (b, 0, i)) for r, _ in IN_T_SEGMENTS]
    return pl.pallas_call(
        _inproj_kernel,
        grid=(bsz, seq // tm),
        in_specs=[pl.BlockSpec((1, tm, d), tok),
                  pl.BlockSpec((1, 1, d), vec), pl.BlockSpec((1, 1, d), vec),
                  _const_spec(gain.shape), _const_spec(w_cat.shape), _const_spec(b_cat.shape),
                  _const_spec(w_t.shape), _const_spec(b_t.shape)],
        out_specs=out_specs,
        out_shape=out_shape,
        compiler_params=_params(("parallel", "parallel")),
        name="norm_input_projection",
    )(x, shift, scale, gain, w_cat, b_cat, w_t, b_t)


def _bf16_round(x):
    return x.astype(BF16).astype(F32)


def _gate_scan_kernel(x_ref, o_ref, split_ref):
    x = x_ref[0]
    seq = x.shape[1]
    seg = MLSTM_CHUNK
    lane = lax.broadcasted_iota(jnp.int32, x.shape, 1) & (seg - 1)
    v = _log_sigmoid(x)
    k = 1
    while k < seg:
        v = v + jnp.where(lane >= k, pltpu.roll(v, k, 1), 0.0)
        k *= 2
    row = lax.broadcasted_iota(jnp.int32, (GATE_ROWS, seg), 0)
    carry = jnp.zeros((GATE_ROWS, 1), F32)
    for j in range(seq // seg):
        sl = slice(j * seg, (j + 1) * seg)
        local = v[:, sl]
        total = local + carry
        o_ref[0, :, sl] = jnp.where(row < MLSTM_HEADS, x[:, sl],
                                    jnp.where(row < 2 * MLSTM_HEADS, local, total))
        scaled = total * LOG2_E
        hi = _bf16_round(scaled)
        mid = _bf16_round(scaled - hi)
        split_ref[0, 0, :, sl] = hi
        split_ref[0, 1, :, sl] = mid
        split_ref[0, 2, :, sl] = _bf16_round((scaled - hi) - mid)
        carry = carry + local[:, seg - 1:seg]


def _gate_scan(pre):
    bsz, rows, seq = pre.shape
    spec = pl.BlockSpec((1, rows, seq), lambda b: (b, 0, 0))
    return pl.pallas_call(
        _gate_scan_kernel, grid=(bsz,), in_specs=[spec],
        out_specs=[spec, pl.BlockSpec((1, 3, rows, seq), lambda b: (b, 0, 0, 0))],
        out_shape=[jax.ShapeDtypeStruct(pre.shape, F32),
                   jax.ShapeDtypeStruct((bsz, 3, rows, seq), F32)],
        compiler_params=_params(("parallel",)),
        name="gate_scan",
    )(pre)


SCAN_SHIFTS = (1, 2, 4)
S5_BLOCK = 32
S5_STATE_LANES = 2 * S5_STATE
S5_ROWS_PER_STEP = 128


def _s5_kernel(u_ref, k_ref, e_ref, c_ref, astep_ref, apow_ref, o_ref, st_ref, carry_ref):
    ng, n_rows = u_ref.shape[1], u_ref.shape[2]
    sl = S5_STATE_LANES

    @pl.when(pl.program_id(1) == 0)
    def _():
        carry_ref[...] = jnp.zeros_like(carry_ref)

    u = [u_ref[0, g].astype(BF16) for g in range(ng)]
    for g in range(ng):
        st_ref[:, g * sl:(g + 1) * sl] = _dot(u[g], e_ref[g])

    def swap(x):
        return jnp.concatenate([pltpu.roll(x[:, g * sl:(g + 1) * sl], S5_STATE, 1)
                                for g in range(ng)], axis=1)

    def body(i, carry):
        r0 = pl.multiple_of(i * SUBLANES, SUBLANES)
        x = st_ref[pl.ds(r0, SUBLANES), :]
        for k, shift in enumerate(SCAN_SHIFTS):
            sh = pltpu.roll(x, shift, 0)
            x = x + astep_ref[k, 0] * sh + astep_ref[k, 1] * swap(sh)
        x = x + apow_ref[0] * carry + apow_ref[1] * swap(carry)
        row = lax.broadcasted_iota(jnp.int32, x.shape, 0)
        st_ref[pl.ds(r0, SUBLANES), :] = jnp.where(row == 0, carry, pltpu.roll(x, 1, 0))
        return jnp.broadcast_to(x[SUBLANES - 1:, :], x.shape)

    carry_ref[...] = lax.fori_loop(0, n_rows // SUBLANES, body, carry_ref[...])

    for g in range(ng):
        h_in = st_ref[:, g * sl:(g + 1) * sl].astype(BF16)
        o_ref[0, g] = _dot(u[g], k_ref[g]) + _dot(h_in, c_ref[g])


def _s5(u, tables):
    bsz, seq, width = u.shape
    blk, g, p = S5_BLOCK, S5_GROUPS, S5_GROUP_DIM
    n_blk = seq // blk
    rows = min(S5_ROWS_PER_STEP, n_blk)
    assert seq % blk == 0 and n_blk % rows == 0 and rows % SUBLANES == 0
    ug = u.reshape(bsz, n_blk, blk, g, p).transpose(0, 3, 1, 2, 4).reshape(bsz, g, n_blk, blk * p)
    spec = pl.BlockSpec((1, g, rows, blk * p), lambda b, i: (b, 0, i, 0))
    y = pl.pallas_call(
        _s5_kernel,
        grid=(bsz, n_blk // rows),
        in_specs=[spec] + [_const_spec(a.shape) for a in tables],
        out_specs=spec,
        out_shape=jax.ShapeDtypeStruct(ug.shape, F32),
        scratch_shapes=[pltpu.VMEM((rows, g * S5_STATE_LANES), F32),
                        pltpu.VMEM((SUBLANES, g * S5_STATE_LANES), F32)],
        compiler_params=_params(("parallel", "arbitrary")),
        name="s5_blocked_scan",
    )(ug, *tables)
    return y.reshape(bsz, g, n_blk, blk, p).transpose(0, 2, 3, 1, 4).reshape(bsz, seq, width)


def _s5_tables(a_re, a_im, log_dt, b_re, b_im, c_re, c_im):
    g, n, p = b_re.shape
    blk = S5_BLOCK
    exact = lax.Precision.HIGHEST
    dt = jnp.exp(log_dt)[:, None]
    lam_r, lam_i = dt * a_re, dt * a_im
    abar_r, abar_i = jnp.exp(lam_r) * jnp.cos(lam_i), jnp.exp(lam_r) * jnp.sin(lam_i)
    den = a_re * a_re + a_im * a_im
    pr, qi = abar_r - 1.0, abar_i
    coef_r = (pr * a_re + qi * a_im) / den
    coef_i = (qi * a_re - pr * a_im) / den
    bbar_r = coef_r[..., None] * b_re - coef_i[..., None] * b_im
    bbar_i = coef_r[..., None] * b_im + coef_i[..., None] * b_re

    def power(e):
        e = jnp.asarray(e, F32)[:, None, None]
        m = jnp.exp(e * lam_r[None])
        return m * jnp.cos(e * lam_i[None]), m * jnp.sin(e * lam_i[None])

    steps = jnp.arange(blk)
    pw_r, pw_i = power(steps)
    ab_r = pw_r[..., None] * bbar_r[None] - pw_i[..., None] * bbar_i[None]
    ab_i = pw_r[..., None] * bbar_i[None] + pw_i[..., None] * bbar_r[None]
    taps = (jnp.einsum('gqn,dgnp->dgqp', c_re, ab_r, precision=exact)
            - jnp.einsum('gqn,dgnp->dgqp', c_im, ab_i, precision=exact))
    lag = steps[None, :] - steps[:, None]
    toe = jnp.where((lag >= 0)[:, :, None, None, None], taps[jnp.clip(lag, 0)], 0.0)
    k_mat = jnp.transpose(toe, (2, 0, 4, 1, 3)).reshape(g, blk * p, blk * p)

    e_r = jnp.transpose(ab_r[::-1], (1, 0, 3, 2)).reshape(g, blk * p, n)
    e_i = jnp.transpose(ab_i[::-1], (1, 0, 3, 2)).reshape(g, blk * p, n)
    e_mat = jnp.concatenate([e_r, e_i], axis=-1)

    pr1, pi1 = power(steps + 1)
    ca_r = c_re[None] * pr1[:, :, None, :] - c_im[None] * pi1[:, :, None, :]
    ca_i = c_re[None] * pi1[:, :, None, :] + c_im[None] * pr1[:, :, None, :]
    c_mat = jnp.concatenate([jnp.transpose(ca_r, (1, 3, 0, 2)).reshape(g, n, blk * p),
                             -jnp.transpose(ca_i, (1, 3, 0, 2)).reshape(g, n, blk * p)], axis=1)

    rows = jnp.arange(SUBLANES)

    def lanes(e):
        r_, i_ = power(jnp.asarray(e) * blk)
        return (jnp.concatenate([r_, r_], axis=-1).reshape(len(e), g * 2 * n),
                jnp.concatenate([-i_, i_], axis=-1).reshape(len(e), g * 2 * n))

    keep = (rows[None, :] >= jnp.asarray(SCAN_SHIFTS)[:, None]).astype(F32)[:, :, None]
    s_same, s_cross = lanes(SCAN_SHIFTS)
    astep = jnp.stack([keep * s_same[:, None, :], keep * s_cross[:, None, :]], axis=1)
    apow = jnp.stack(lanes(rows + 1), axis=0)
    return k_mat.astype(BF16), e_mat.astype(BF16), c_mat.astype(BF16), astep, apow


def _mlstm_kernel(qk_ref, vt_ref, o_ref, grow_ref, gcol_ref, cw_ref, cb_ref, wq_ref, wk_ref,
                  ng_ref, out_ref, tail_ref, c_ref, n_ref, m_ref):
    chunk = qk_ref.shape[1]
    nh = MLSTM_HEADS

    @pl.when(pl.program_id(1) == 0)
    def _():
        tail_ref[...] = jnp.zeros_like(tail_ref)
        c_ref[...] = jnp.zeros_like(c_ref)
        n_ref[...] = jnp.zeros_like(n_ref)
        m_ref[...] = jnp.zeros_like(m_ref)

    x = qk_ref[0]
    ext = jnp.concatenate([tail_ref[...], x], axis=0)
    conv = cb_ref[...] + cw_ref[CONV_TAPS - 1:CONV_TAPS, :] * x
    for tap in range(CONV_TAPS - 1):
        back = CONV_TAPS - 1 - tap
        conv = conv + cw_ref[tap:tap + 1, :] * ext[SUBLANES - back:SUBLANES - back + chunk, :]
    tail_ref[...] = x[chunk - SUBLANES:, :]
    cx = _silu(conv)

    key_id = lax.broadcasted_iota(jnp.int32, (chunk, chunk), 0)
    query_id = lax.broadcasted_iota(jnp.int32, (chunk, chunk), 1)
    causal = key_id <= query_id
    k_scale = MLSTM_HEAD_DIM ** -0.5
    heads = range(nh)
    cols = [slice(hd * LANES, (hd + 1) * LANES) for hd in heads]
    cxh = [cx[:, cols[hd]].astype(BF16) for hd in heads]
    q = [_dot(cxh[hd], wq_ref[hd]).astype(BF16) for hd in heads]
    k = [(_dot(cxh[hd], wk_ref[hd]) * k_scale).astype(BF16) for hd in heads]
    v_t = [vt_ref[0, cols[hd], :] for hd in heads]
    kq = [_dot_nt(k[hd], q[hd]) for hd in heads]
    carry = [_dot_nt(c_ref[hd].astype(BF16), q[hd]) for hd in heads]
    n_q = [_dot_nt(n_ref[...].astype(BF16), q[hd]) for hd in heads]

    s, inter, m_t, stats = [], [], [], []
    for hd in heads:
        li_row = grow_ref[0, hd:hd + 1, :]
        b_row = grow_ref[0, nh + hd:nh + hd + 1, :]
        d_col = gcol_ref[0, :, hd:hd + 1] - gcol_ref[0, :, nh + hd:nh + hd + 1]
        m_st = m_ref[hd:hd + 1, 0:1]
        dmat = jnp.where(causal, b_row + d_col, -jnp.inf)
        m_inter = b_row + m_st
        m_t.append(jnp.maximum(m_inter, jnp.max(dmat, axis=0, keepdims=True)))
        s.append(kq[hd] * jnp.exp(dmat - m_t[hd]))
        inter.append(jnp.exp(m_inter - m_t[hd]))

        b_last = b_row[:, chunk - 1:chunk]
        w_row = b_last - b_row + li_row
        m_new = jnp.maximum(b_last + m_st, jnp.max(w_row, axis=1, keepdims=True))
        stats.append((m_new, jnp.exp(b_last + m_st - m_new), jnp.exp(w_row - m_new)))

    sv = [_dot(v_t[hd].astype(BF16), s[hd].astype(BF16)) for hd in heads]
    vk = [_dot((v_t[hd] * stats[hd][2]).astype(BF16), k[hd]) for hd in heads]
    row8 = lax.broadcasted_iota(jnp.int32, (SUBLANES, chunk), 0)
    ws_rows = jnp.zeros((SUBLANES, chunk), F32)
    for hd in heads:
        ws_rows = jnp.where(row8 == hd, stats[hd][2], ws_rows)
    ws_k = [_dot(ws_rows.astype(BF16), k[hd]) for hd in heads]

    for hd in heads:
        m_new, decay, _ = stats[hd]
        num = sv[hd] + inter[hd] * carry[hd]
        den = jnp.sum(s[hd], axis=0, keepdims=True) + inter[hd] * n_q[hd][hd:hd + 1, :]
        h = num * (1.0 / jnp.maximum(jnp.abs(den), jnp.exp(-m_t[hd])))
        c_ref[hd] = decay * c_ref[hd] + vk[hd]
        n_ref[hd:hd + 1, :] = decay * n_ref[hd:hd + 1, :] + ws_k[hd][hd:hd + 1, :]
        m_ref[hd:hd + 1, :] = jnp.broadcast_to(m_new, (1, LANES))
        hn = h * lax.rsqrt(jnp.sum(h * h, axis=0, keepdims=True) * (1.0 / MLSTM_HEAD_DIM) + NORM_EPS)
        out_ref[0, :, cols[hd]] = _sigmoid(o_ref[0, :, cols[hd]]) * (hn.T * ng_ref[:, cols[hd]])


def _mlstm(qk, v_t, o, gate_rows, gate_cols, conv_w, conv_b, wq, wk, norm_g):
    bsz, seq, width = qk.shape
    chunk = MLSTM_CHUNK
    tok = lambda b, i: (b, i, 0)
    tok_spec = pl.BlockSpec((1, chunk, width), tok)
    return pl.pallas_call(
        _mlstm_kernel,
        grid=(bsz, seq // chunk),
        in_specs=[tok_spec, pl.BlockSpec((1, width, chunk), lambda b, i: (b, 0, i)), tok_spec,
                  pl.BlockSpec((1, GATE_ROWS, chunk), lambda b, i: (b, 0, i)),
                  pl.BlockSpec((1, chunk, GATE_ROWS), tok)] + [
            _const_spec(a.shape) for a in (conv_w, conv_b, wq, wk, norm_g)],
        out_specs=tok_spec,
        out_shape=jax.ShapeDtypeStruct(qk.shape, F32),
        scratch_shapes=[pltpu.VMEM((SUBLANES, width), F32),
                        pltpu.VMEM((MLSTM_HEADS, LANES, LANES), F32),
                        pltpu.VMEM((SUBLANES, LANES), F32),
                        pltpu.VMEM((SUBLANES, LANES), F32)],
        compiler_params=_params(("parallel", "arbitrary")),
        name="mlstm_chunkwise",
    )(qk, v_t, o, gate_rows, gate_cols, conv_w, conv_b, wq, wk, norm_g)


FOX_BIAS_LANES = 6
FOX_VT_ROWS = FOX_HEAD_DIM + 16
FOX_FLAG_NEW_QUERY, FOX_FLAG_DIAGONAL, FOX_FLAG_FINISH = 1, 2, 4


def _fox_kernel(qa_ref, ka_ref, kb_ref, qo_ref, flag_ref, q_ref, qbias_ref, k_ref, kbias_ref,
                vt_ref, o_ref, qx_ref, s_ref, smax_ref, m_ref, acc_ref):
    step = pl.program_id(1)
    flags = flag_ref[step]
    tq, tk = q_ref.shape[1], k_ref.shape[1]
    hd = FOX_HEAD_DIM

    def pair_lanes(pair):
        return slice(pair * LANES, (pair + 1) * LANES)

    def reset_accumulators():
        m_ref[...] = jnp.full_like(m_ref, NEG_BIG)
        acc_ref[...] = jnp.zeros_like(acc_ref)

    @pl.when(step == 0)
    def _():
        s_ref[...] = jnp.full(s_ref.shape, NEG_BIG, F32)
        smax_ref[...] = jnp.full(smax_ref.shape, NEG_BIG, F32)
        reset_accumulators()

    @pl.when((flags & FOX_FLAG_NEW_QUERY) != 0)
    def _():
        lane = lax.broadcasted_iota(jnp.int32, (1, LANES), 1)
        for pair in range(FOX_PAIRS):
            q = q_ref[0, :, pair_lanes(pair)] * (hd ** -0.5 * LOG2_E)
            qb = qbias_ref[0, :, pair_lanes(pair)]
            for hh in range(2):
                in_head = (lane >= hh * hd) & (lane < (hh + 1) * hd)
                in_bias = (lane >= hh * FOX_BIAS_LANES) & (lane < (hh + 1) * FOX_BIAS_LANES)
                qx_ref[2 * pair + hh, :, :LANES] = jnp.where(in_head, q, 0.0).astype(BF16)
                qx_ref[2 * pair + hh, :, LANES:] = jnp.where(in_bias, qb, jnp.zeros_like(qb))

    def attend(diagonal):
        kx = [jnp.concatenate([k_ref[0, :, pair_lanes(pair)], kbias_ref[0, :, pair_lanes(pair)]],
                              axis=1) for pair in range(FOX_PAIRS)]
        probs = {}

        def score(hh):
            s = _dot_nt(kx[hh // 2], qx_ref[hh])
            if diagonal:
                visible = (lax.broadcasted_iota(jnp.int32, (tk, tq), 0)
                           <= lax.broadcasted_iota(jnp.int32, (tk, tq), 1))
                s = jnp.where(visible, s, NEG_BIG)
            s_ref[hh] = s
            smax_ref[hh] = jnp.max(s, axis=0, keepdims=True)

        def softmax(hh):
            s = s_ref[hh]
            m_prev = m_ref[hh]
            m_new = jnp.maximum(m_prev, smax_ref[hh])
            m_ref[hh] = m_new
            probs[hh] = (jnp.exp2(m_prev - m_new), jnp.exp2(s - m_new).astype(BF16))

        def value(hh):
            alpha, p = probs.pop(hh)
            acc_ref[hh] = alpha * acc_ref[hh] + _dot(vt_ref[0, hh], p)

        softmax(0)
        for hh in range(FOX_HEADS):
            score(hh)
            if hh + 1 < FOX_HEADS:
                softmax(hh + 1)
            value(hh)

    @pl.when((flags & FOX_FLAG_DIAGONAL) == 0)
    def _():
        attend(False)

    @pl.when((flags & FOX_FLAG_DIAGONAL) != 0)
    def _():
        attend(True)

    @pl.when((flags & FOX_FLAG_FINISH) != 0)
    def _():
        for pair in range(FOX_PAIRS):
            out_t = jnp.concatenate([acc_ref[hh, :hd, :] / acc_ref[hh, hd:hd + 1, :]
                                     for hh in (2 * pair, 2 * pair + 1)], axis=0)
            o_ref[0, :, pair_lanes(pair)] = out_t.T
        reset_accumulators()

    @pl.when(step == 0)
    def _():
        reset_accumulators()


def _fox(q, qb, k, kb, v_t, tq):
    bsz, seq, width = q.shape
    nq = seq // tq
    q_idx = [i for i in range(nq) for _ in range(i + 1)]
    k_idx = [j for i in range(nq) for j in range(i + 1)]
    n_pairs = len(q_idx)
    q_new = q_idx + [q_idx[-1]]
    k_new = k_idx + [k_idx[-1]]
    k_old = [0] + k_idx
    q_old = [0] + q_idx
    flags = []
    for n in range(n_pairs + 1):
        f = 0
        if n < n_pairs and k_idx[n] == 0:
            f |= FOX_FLAG_NEW_QUERY
        if n < n_pairs and k_idx[n] == q_idx[n]:
            f |= FOX_FLAG_DIAGONAL
        if n >= 1 and (n == n_pairs or k_idx[n] == 0):
            f |= FOX_FLAG_FINISH
        flags.append(f)
    tables = [jnp.asarray(t, jnp.int32) for t in (q_new, k_new, k_old, q_old, flags)]
    q_spec = pl.BlockSpec((1, tq, width), lambda b, s, qa, ka, kb_, qo, fl: (b, qa[s], 0))
    k_spec = pl.BlockSpec((1, tq, width), lambda b, s, qa, ka, kb_, qo, fl: (b, ka[s], 0))
    grid_spec = pltpu.PrefetchScalarGridSpec(
        num_scalar_prefetch=len(tables),
        grid=(bsz, n_pairs + 1),
        in_specs=[q_spec, q_spec, k_spec, k_spec,
                  pl.BlockSpec((1, FOX_HEADS, FOX_VT_ROWS, tq),
                               lambda b, s, qa, ka, kb_, qo, fl: (b, 0, 0, kb_[s]))],
        out_specs=pl.BlockSpec((1, tq, width), lambda b, s, qa, ka, kb_, qo, fl: (b, qo[s], 0)),
        scratch_shapes=[pltpu.VMEM((FOX_HEADS, tq, 2 * LANES), BF16),
                        pltpu.VMEM((FOX_HEADS, tq, tq), F32),
                        pltpu.VMEM((FOX_HEADS, 1, tq), F32),
                        pltpu.VMEM((FOX_HEADS, 1, tq), F32),
                        pltpu.VMEM((FOX_HEADS, FOX_VT_ROWS, tq), F32)],
    )
    v_heads = v_t.reshape(bsz, FOX_HEADS, FOX_HEAD_DIM, seq)
    ones = jnp.ones((bsz, FOX_HEADS, FOX_VT_ROWS - FOX_HEAD_DIM, seq), v_t.dtype)
    v_ext = jnp.concatenate([v_heads, ones], axis=2)
    return pl.pallas_call(
        _fox_kernel, grid_spec=grid_spec,
        out_shape=jax.ShapeDtypeStruct((bsz, seq, width), F32),
        compiler_params=_params(("parallel", "arbitrary")),
        name="forgetting_attention",
    )(*tables, q, qb, k, kb, v_ext)


def _fox_bias_lanes(split):
    bsz, _, _, seq = split.shape
    lo = 2 * MLSTM_HEADS
    pieces = jnp.transpose(split[:, :, lo:lo + FOX_HEADS, :], (0, 3, 2, 1))
    ones = jnp.ones_like(pieces)

    def lanes(per_head):
        per_pair = per_head.reshape(bsz, seq, FOX_PAIRS, 2 * FOX_BIAS_LANES)
        per_pair = jnp.pad(per_pair, ((0, 0), (0, 0), (0, 0), (0, LANES - 2 * FOX_BIAS_LANES)))
        return per_pair.reshape(bsz, seq, FOX_PAIRS * LANES).astype(BF16)

    return (lanes(jnp.concatenate([pieces, ones], axis=-1)),
            lanes(jnp.concatenate([ones, -pieces], axis=-1)))


FFN_COLS = 256


def _merge_ffn_kernel(x_ref, us_ref, ss_ref, ym_ref, yf_ref, gate_ref, g1_ref, sh_ref, sc_ref,
                      g2_ref, ds_ref, wglu_ref, ws_ref, wm_ref, wf_ref, wo_ref, gain_ref, w1_ref,
                      w3_ref, w2_ref, fg_ref, o_ref, *, final_norm):
    d = x_ref.shape[-1]

    def gate(i):
        return _sigmoid(gate_ref[0, :, i * d:(i + 1) * d].astype(F32))

    ys = ss_ref[0] + ds_ref[...] * us_ref[0]
    ys = ys * (0.5 * (1.0 + jnp.tanh(math.sqrt(2.0 / math.pi) * (ys + 0.044715 * (ys * ys * ys)))))
    ys = ys * _sigmoid(_dot(ys.astype(BF16), wglu_ref[...]))

    merged = (gate(0) * _dot(ys.astype(BF16), ws_ref[...])
              + gate(1) * _dot(ym_ref[0].astype(BF16), wm_ref[...])
              + gate(2) * _dot(yf_ref[0].astype(BF16), wf_ref[...]))
    x = x_ref[0] + g1_ref[0] * _dot(merged.astype(BF16), wo_ref[...])

    h = _rms_modulate(x, gain_ref[...], sc_ref[0], sh_ref[0]).astype(BF16)
    starts = list(range(0, w1_ref.shape[1], FFN_COLS))

    def up(c0):
        return _dot(h, w1_ref[:, c0:c0 + FFN_COLS]), _dot(h, w3_ref[:, c0:c0 + FFN_COLS])

    acc = jnp.zeros(x.shape, F32)
    nxt = up(starts[0])
    for i, c0 in enumerate(starts):
        a, b = nxt
        if i + 1 < len(starts):
            nxt = up(starts[i + 1])
        acc = acc + _dot((_silu(a) * b).astype(BF16), w2_ref[c0:c0 + FFN_COLS, :])
    y = x + g2_ref[0] * acc
    if final_norm:
        y = (y * lax.rsqrt(jnp.mean(y * y, axis=-1, keepdims=True) + NORM_EPS)) * fg_ref[...]
    o_ref[0] = y


def _merge_ffn(acts, mod_vectors, weights, final_norm, tm):
    bsz, seq, d = acts[0].shape
    tok = lambda b, i: (b, i, 0)
    vec = pl.BlockSpec((1, 1, d), lambda b, i: (b, 0, 0))
    return pl.pallas_call(
        functools.partial(_merge_ffn_kernel, final_norm=final_norm),
        grid=(bsz, seq // tm),
        in_specs=[pl.BlockSpec((1, tm, a.shape[-1]), tok) for a in acts]
        + [vec] * len(mod_vectors) + [_const_spec(w.shape) for w in weights],
        out_specs=pl.BlockSpec((1, tm, d), tok),
        out_shape=jax.ShapeDtypeStruct(acts[0].shape, F32),
        compiler_params=_params(("parallel", "parallel")),
        name="merge_swiglu",
    )(*acts, *mod_vectors, *weights)


def _pad_heads(w, heads, head_dim):
    lead = w.shape[:-1]
    w = w.reshape(lead + (heads, head_dim))
    w = jnp.pad(w, [(0, 0)] * len(lead) + [(0, 0), (0, LANES - head_dim)])
    return w.reshape(lead + (heads * LANES,))


def _layer_weights(w_in, b_in):
    sizes = (S5_WIDTH, MLSTM_WIDTH, MLSTM_WIDTH, MLSTM_WIDTH, MLSTM_HEADS, MLSTM_HEADS,
             FOX_WIDTH, FOX_WIDTH, FOX_WIDTH, FOX_HEADS)
    offs = [0]
    for n in sizes:
        offs.append(offs[-1] + n)
    both = jnp.concatenate([w_in, b_in[None, :]], axis=0)
    part = [both[:, offs[i]:offs[i + 1]] for i in range(len(sizes))] + [both[:, offs[-1]:]]
    pad_m = lambda w: _pad_heads(w, MLSTM_HEADS, MLSTM_HEAD_DIM)
    cat = jnp.concatenate([part[0], pad_m(part[1]), pad_m(part[3]),
                           part[6], part[7], part[10]], axis=1)
    small = jnp.concatenate([part[4], part[5], part[9]], axis=1)
    small = jnp.pad(small, ((0, 0), (0, GATE_ROWS - small.shape[1])))
    cat_t = jnp.concatenate([small, part[8], pad_m(part[2])], axis=1).T
    return (cat[:-1].astype(BF16), cat[-1:], cat_t[:, :-1].astype(BF16), cat_t[:, -1:])


def kernel(x, c, mod_w, mod_b, norm1_g, norm2_g, w_in, b_in, s5_a_re, s5_a_im, s5_log_dt, s5_b_re, s5_b_im, s5_c_re, s5_c_im, s5_d, s5_w_glu, mlstm_conv_w, mlstm_conv_b, mlstm_wq, mlstm_wk, mlstm_norm_g, w_up_s5, w_up_mlstm, w_up_fox, w_out, ffn_w1, ffn_w3, ffn_w2, final_g):
    bsz, seq, d = x.shape
    depth = mod_w.shape[0]
    assert seq % MLSTM_CHUNK == 0
    tm = _row_tile(seq, 512)
    tq = _row_tile(seq, 1024)

    mod = _modulation(c, mod_w, mod_b)
    final_gain = final_g.reshape(1, d)
    for l in range(depth):
        sh1, sc1, g1, sh2, sc2, g2 = [mod[l, :, i * d:(i + 1) * d].reshape(bsz, 1, d)
                                      for i in range(6)]
        w_cat, b_cat, w_t, b_t = _layer_weights(w_in[l], b_in[l])
        (s5_u, m_qk, m_o, f_q, f_k, gates, gate_pre, f_vt, m_vt) = _input_projection(
            x, sh1, sc1, norm1_g[l].reshape(1, d), w_cat, b_cat, w_t, b_t, tm)

        gate_rows, gate_split = _gate_scan(gate_pre)
        gate_cols = jnp.swapaxes(gate_rows, 1, 2)
        f_qb, f_kb = _fox_bias_lanes(gate_split)

        s5_ssm = _s5(s5_u, _s5_tables(s5_a_re[l], s5_a_im[l], s5_log_dt[l], s5_b_re[l],
                                      s5_b_im[l], s5_c_re[l], s5_c_im[l]))

        pad_hh = ((0, 0), (0, LANES - MLSTM_HEAD_DIM), (0, LANES - MLSTM_HEAD_DIM))
        y_m = _mlstm(m_qk, m_vt, m_o, gate_rows, gate_cols,
                     _pad_heads(mlstm_conv_w[l], MLSTM_HEADS, MLSTM_HEAD_DIM),
                     _pad_heads(mlstm_conv_b[l].reshape(1, -1), MLSTM_HEADS, MLSTM_HEAD_DIM),
                     jnp.pad(mlstm_wq[l], pad_hh).astype(BF16),
                     jnp.pad(mlstm_wk[l], pad_hh).astype(BF16),
                     _pad_heads(mlstm_norm_g[l].reshape(1, -1), MLSTM_HEADS, MLSTM_HEAD_DIM))

        y_f = _fox(f_q, f_qb, f_k, f_kb, f_vt, tq)

        w_up_m = _pad_heads(w_up_mlstm[l].T, MLSTM_HEADS, MLSTM_HEAD_DIM).T
        weights = (s5_d[l].reshape(1, -1), s5_w_glu[l].astype(BF16),
                   w_up_s5[l].astype(BF16), w_up_m.astype(BF16), w_up_fox[l].astype(BF16),
                   w_out[l].astype(BF16), norm2_g[l].reshape(1, d), ffn_w1[l].astype(BF16),
                   ffn_w3[l].astype(BF16), ffn_w2[l].astype(BF16), final_gain)
        x = _merge_ffn((x, s5_u, s5_ssm, y_m, y_f, gates), (g1, sh2, sc2, g2), weights,
                       l == depth - 1, tm)
    return x
```

```python
import functools
import math

import jax
import jax.numpy as jnp
from jax import lax
from jax.experimental import pallas as pl
from jax.experimental.pallas import tpu as pltpu

F32 = jnp.float32
BF16 = jnp.bfloat16

LANES = 128
SUBLANES = 8
VMEM_LIMIT_BYTES = 56 * 1024 * 1024

NORM_EPS = 1e-6
S5_GROUPS = 16
S5_GROUP_DIM = 16
S5_STATE = 64
S5_WIDTH = S5_GROUPS * S5_GROUP_DIM
S5_CHANNELS = S5_GROUPS * S5_STATE
MLSTM_HEADS = 4
MLSTM_HEAD_DIM = 96
MLSTM_WIDTH = MLSTM_HEADS * MLSTM_HEAD_DIM
MLSTM_PAD_WIDTH = MLSTM_HEADS * LANES
MLSTM_CHUNK = 128
CONV_TAPS = 4
FOX_HEADS = 6
FOX_HEAD_DIM = 64
FOX_WIDTH = FOX_HEADS * FOX_HEAD_DIM
FOX_PAIRS = FOX_WIDTH // LANES
GATE_ROWS = 16
NEG_BIG = -1e30
LOG2_E = math.log2(math.e)

NT_DIMS = (((1,), (1,)), ((), ()))


def _dot(a, b):
    return jnp.dot(a, b, preferred_element_type=F32)


def _dot_nt(a, b):
    return lax.dot_general(a, b, NT_DIMS, preferred_element_type=F32)


def _sigmoid(x):
    return 1.0 / (1.0 + jnp.exp(-x))


def _silu(x):
    return x * _sigmoid(x)


def _log_sigmoid(x):
    return jnp.minimum(x, 0.0) - jnp.log1p(jnp.exp(-jnp.abs(x)))


def _rms_modulate(x, gain, scale, shift):
    y = x * lax.rsqrt(jnp.mean(x * x, axis=-1, keepdims=True) + NORM_EPS)
    return (y * gain) * (1.0 + scale) + shift


def _params(semantics):
    return pltpu.CompilerParams(dimension_semantics=semantics, vmem_limit_bytes=VMEM_LIMIT_BYTES)


def _const_spec(shape):
    nd = len(shape)
    return pl.BlockSpec(shape, lambda *_: (0,) * nd, pipeline_mode=pl.Buffered(1))


def _row_tile(seq, want):
    t = min(want, seq)
    assert seq % t == 0 and t % SUBLANES == 0
    return t


def _mod_kernel(c_ref, w_ref, b_ref, o_ref):
    cf = _silu(c_ref[...]).astype(BF16)
    o_ref[0] = _dot(cf, w_ref[0].astype(BF16)) + b_ref[0]


def _modulation(c, mod_w, mod_b):
    depth, d, six_d = mod_w.shape
    bsz = c.shape[0]
    n_col = six_d // d
    return pl.pallas_call(
        _mod_kernel,
        grid=(depth, n_col),
        in_specs=[pl.BlockSpec((bsz, d), lambda l, j: (0, 0)),
                  pl.BlockSpec((1, d, d), lambda l, j: (l, 0, j)),
                  pl.BlockSpec((1, 1, d), lambda l, j: (l, 0, j))],
        out_specs=pl.BlockSpec((1, bsz, d), lambda l, j: (l, 0, j)),
        out_shape=jax.ShapeDtypeStruct((depth, bsz, six_d), F32),
        compiler_params=_params(("parallel", "parallel")),
        name="adaln_modulation",
    )(c, mod_w, mod_b.reshape(depth, 1, six_d))


IN_SEGMENTS = ((S5_WIDTH, F32), (MLSTM_PAD_WIDTH, F32), (MLSTM_PAD_WIDTH, F32),
               (FOX_WIDTH, F32), (FOX_WIDTH, BF16))
IN_T_SEGMENTS = ((GATE_ROWS, F32), (FOX_WIDTH, BF16), (MLSTM_PAD_WIDTH, F32))
IN_T_FOX_VALUES = 1
FOX_VT_ROWS = FOX_HEAD_DIM + 16
DOT_COLS = 512


def _inproj_kernel(x_ref, sh_ref, sc_ref, g_ref, w_ref, b_ref, wt_ref, bt_ref, *out_refs):
    h = _rms_modulate(x_ref[0], g_ref[...], sc_ref[0], sh_ref[0]).astype(BF16)
    n_t = len(IN_T_SEGMENTS)
    start = 0
    for o_ref in out_refs[:-n_t]:
        width = o_ref.shape[-1]
        for c0 in range(0, width, DOT_COLS):
            cw = min(DOT_COLS, width - c0)
            z = _dot(h, w_ref[:, start + c0:start + c0 + cw]) + b_ref[:, start + c0:start + c0 + cw]
            o_ref[0, :, c0:c0 + cw] = z.astype(o_ref.dtype)
        start += width
    start = 0
    for (rows, _), o_ref in zip(IN_T_SEGMENTS, out_refs[-n_t:]):
        z = _dot_nt(wt_ref[start:start + rows, :], h) + bt_ref[start:start + rows, :]
        if o_ref.ndim == 3:
            o_ref[0] = z.astype(o_ref.dtype)
        else:
            for hd in range(FOX_HEADS):
                o_ref[0, hd, :FOX_HEAD_DIM, :] = z[hd * FOX_HEAD_DIM:(hd + 1) * FOX_HEAD_DIM].astype(
                    o_ref.dtype)
                o_ref[0, hd, FOX_HEAD_DIM:, :] = jnp.ones(
                    (FOX_VT_ROWS - FOX_HEAD_DIM, z.shape[1]), o_ref.dtype)
        start += rows


def _input_projection(x, shift, scale, gain, w_cat, b_cat, w_t, b_t, tm):
    bsz, seq, d = x.shape
    segs = IN_SEGMENTS + ((w_cat.shape[1] - sum(w for w, _ in IN_SEGMENTS), BF16),)
    tok = lambda b, i: (b, i, 0)
    vec = lambda b, i: (b, 0, 0)
    out_shape = [jax.ShapeDtypeStruct((bsz, seq, w), dt) for w, dt in segs]
    out_specs = [pl.BlockSpec((1, tm, w), tok) for w, _ in segs]
    for idx, (r, dt) in enumerate(IN_T_SEGMENTS):
        if idx == IN_T_FOX_VALUES:
            out_shape.append(jax.ShapeDtypeStruct((bsz, FOX_HEADS, FOX_VT_ROWS, seq), dt))
            out_specs.append(pl.BlockSpec((1, FOX_HEADS, FOX_VT_ROWS, tm),
                                          lambda b, i: (b, 0, 0, i)))
        else:
            out_shape.append(jax.ShapeDtypeStruct((bsz, r, seq), dt))
            out_specs.append(pl.BlockSpec((1, r, tm), lambda b, i: (b, 0, i)))
    return pl.pallas_call(
        _inproj_kernel,
        grid=(bsz, seq // tm),
        in_specs=[pl.BlockSpec((1, tm, d), tok),
                  pl.BlockSpec((1, 1, d), vec), pl.BlockSpec((1, 1, d), vec),
                  _const_spec(gain.shape), _const_spec(w_cat.shape), _const_spec(b_cat.shape),
                  _const_spec(w_t.shape), _const_spec(b_t.shape)],
        out_specs=out_specs,
        out_shape=out_shape,
        compiler_params=_params(("parallel", "parallel")),
        name="norm_input_projection",
    )(x, shift, scale, gain, w_cat, b_cat, w_t, b_t)


def _bf16_round(x):
    return x.astype(BF16).astype(F32)


def _gate_scan_kernel(x_ref, o_ref, split_ref):
    x = x_ref[0]
    seq = x.shape[1]
    seg = MLSTM_CHUNK
    lane = lax.broadcasted_iota(jnp.int32, x.shape, 1) & (seg - 1)
    v = _log_sigmoid(x)
    k = 1
    while k < seg:
        v = v + jnp.where(lane >= k, pltpu.roll(v, k, 1), 0.0)
        k *= 2
    row = lax.broadcasted_iota(jnp.int32, (GATE_ROWS, seg), 0)
    carry = jnp.zeros((GATE_ROWS, 1), F32)
    for j in range(seq // seg):
        sl = slice(j * seg, (j + 1) * seg)
        local = v[:, sl]
        total = local + carry
        o_ref[0, :, sl] = jnp.where(row < MLSTM_HEADS, x[:, sl],
                                    jnp.where(row < 2 * MLSTM_HEADS, local, total))
        scaled = total * LOG2_E
        hi = _bf16_round(scaled)
        mid = _bf16_round(scaled - hi)
        split_ref[0, 0, :, sl] = hi
        split_ref[0, 1, :, sl] = mid
        split_ref[0, 2, :, sl] = _bf16_round((scaled - hi) - mid)
        carry = carry + local[:, seg - 1:seg]


def _gate_scan(pre):
    bsz, rows, seq = pre.shape
    spec = pl.BlockSpec((1, rows, seq), lambda b: (b, 0, 0))
    return pl.pallas_call(
        _gate_scan_kernel, grid=(bsz,), in_specs=[spec],
        out_specs=[spec, pl.BlockSpec((1, 3, rows, seq), lambda b: (b, 0, 0, 0))],
        out_shape=[jax.ShapeDtypeStruct(pre.shape, F32),
                   jax.ShapeDtypeStruct((bsz, 3, rows, seq), F32)],
        compiler_params=_params(("parallel",)),
        name="gate_scan",
    )(pre)


SCAN_SHIFTS = (1, 2, 4)


def _s5_kernel(u_ref, bmat_ref, cmat_ref, astep_ref, apow_ref, d_ref, wglu_ref, o_ref,
               st_ref, carry_ref):
    nch = S5_CHANNELS

    @pl.when(pl.program_id(1) == 0)
    def _():
        carry_ref[...] = jnp.zeros_like(carry_ref)

    u = u_ref[0]
    st_ref[...] = _dot(u.astype(BF16), bmat_ref[...])
    n_blocks = u.shape[0] // SUBLANES

    def body(i, carry):
        cr, ci = carry
        r0 = pl.multiple_of(i * SUBLANES, SUBLANES)
        xr = st_ref[pl.ds(r0, SUBLANES), :nch]
        xi = st_ref[pl.ds(r0, SUBLANES), nch:]
        for k, shift in enumerate(SCAN_SHIFTS):
            ar, ai = astep_ref[k, :, :nch], astep_ref[k, :, nch:]
            sr, si = pltpu.roll(xr, shift, 0), pltpu.roll(xi, shift, 0)
            xr, xi = xr + (ar * sr - ai * si), xi + (ar * si + ai * sr)
        pr, pi = apow_ref[:, :nch], apow_ref[:, nch:]
        xr, xi = xr + (pr * cr - pi * ci), xi + (pr * ci + pi * cr)
        st_ref[pl.ds(r0, SUBLANES), :nch] = xr
        st_ref[pl.ds(r0, SUBLANES), nch:] = xi
        last = SUBLANES - 1
        return (jnp.broadcast_to(xr[last:, :], xr.shape), jnp.broadcast_to(xi[last:, :], xi.shape))

    cr, ci = lax.fori_loop(0, n_blocks, body, (carry_ref[:, :nch], carry_ref[:, nch:]))
    carry_ref[:, :nch] = cr
    carry_ref[:, nch:] = ci

    y = _dot(st_ref[...].astype(BF16), cmat_ref[...]) + d_ref[...] * u
    y = y * (0.5 * (1.0 + jnp.tanh(math.sqrt(2.0 / math.pi) * (y + 0.044715 * (y * y * y)))))
    o_ref[0] = y * _sigmoid(_dot(y.astype(BF16), wglu_ref[...]))


def _s5(u, bmat, cmat, astep, apow, d_skip, w_glu, tb):
    bsz, seq, width = u.shape
    tok = lambda b, i: (b, i, 0)
    return pl.pallas_call(
        _s5_kernel,
        grid=(bsz, seq // tb),
        in_specs=[pl.BlockSpec((1, tb, width), tok)] + [
            _const_spec(a.shape) for a in (bmat, cmat, astep, apow, d_skip, w_glu)],
        out_specs=pl.BlockSpec((1, tb, width), tok),
        out_shape=jax.ShapeDtypeStruct(u.shape, F32),
        scratch_shapes=[pltpu.VMEM((tb, 2 * S5_CHANNELS), F32),
                        pltpu.VMEM((SUBLANES, 2 * S5_CHANNELS), F32)],
        compiler_params=_params(("parallel", "arbitrary")),
        name="s5_scan",
    )(u, bmat, cmat, astep, apow, d_skip, w_glu)


def _s5_tables(a_re, a_im, log_dt, b_re, b_im, c_re, c_im):
    g, n, p = b_re.shape
    dt = jnp.exp(log_dt)[:, None]
    mag = jnp.exp(dt * a_re)
    ang = dt * a_im
    abar_r, abar_i = mag * jnp.cos(ang), mag * jnp.sin(ang)
    den = a_re * a_re + a_im * a_im
    pr, qi = abar_r - 1.0, abar_i
    coef_r = (pr * a_re + qi * a_im) / den
    coef_i = (qi * a_re - pr * a_im) / den
    bbar_r = coef_r[..., None] * b_re - coef_i[..., None] * b_im
    bbar_i = coef_r[..., None] * b_im + coef_i[..., None] * b_re
    eye = jnp.eye(g, dtype=F32)

    def in_block(bb):
        return jnp.einsum('gnp,gh->gphn', bb, eye).reshape(g * p, g * n)

    def out_block(cc):
        return jnp.einsum('gpn,gh->gnhp', cc, eye).reshape(g * n, g * p)

    bmat = jnp.concatenate([in_block(bbar_r), in_block(bbar_i)], axis=1).astype(BF16)
    cmat = jnp.concatenate([out_block(c_re), -out_block(c_im)], axis=0).astype(BF16)

    def power(e):
        e = jnp.asarray(e, F32)[:, None, None]
        m = jnp.exp(e * (dt * a_re)[None])
        th = e * ang[None]
        return (m * jnp.cos(th)).reshape(-1, g * n), (m * jnp.sin(th)).reshape(-1, g * n)

    rows = jnp.arange(SUBLANES)
    sr, si = power(SCAN_SHIFTS)
    keep = (rows[None, :] >= jnp.asarray(SCAN_SHIFTS)[:, None]).astype(F32)[:, :, None]
    astep = jnp.concatenate([keep * sr[:, None, :], keep * si[:, None, :]], axis=-1)
    wr, wi = power(rows + 1)
    apow = jnp.concatenate([wr, wi], axis=-1)
    return bmat, cmat, astep, apow


def _mlstm_kernel(qk_ref, vt_ref, o_ref, grow_ref, gcol_ref, cw_ref, cb_ref, wq_ref, wk_ref,
                  ng_ref, out_ref, tail_ref, c_ref, n_ref, m_ref):
    chunk = qk_ref.shape[1]
    nh = MLSTM_HEADS

    @pl.when(pl.program_id(1) == 0)
    def _():
        tail_ref[...] = jnp.zeros_like(tail_ref)
        c_ref[...] = jnp.zeros_like(c_ref)
        n_ref[...] = jnp.zeros_like(n_ref)
        m_ref[...] = jnp.zeros_like(m_ref)

    x = qk_ref[0]
    ext = jnp.concatenate([tail_ref[...], x], axis=0)
    conv = cb_ref[...] + cw_ref[CONV_TAPS - 1:CONV_TAPS, :] * x
    for tap in range(CONV_TAPS - 1):
        back = CONV_TAPS - 1 - tap
        conv = conv + cw_ref[tap:tap + 1, :] * ext[SUBLANES - back:SUBLANES - back + chunk, :]
    tail_ref[...] = x[chunk - SUBLANES:, :]
    cx = _silu(conv)

    key_id = lax.broadcasted_iota(jnp.int32, (chunk, chunk), 0)
    query_id = lax.broadcasted_iota(jnp.int32, (chunk, chunk), 1)
    causal = key_id <= query_id
    k_scale = MLSTM_HEAD_DIM ** -0.5
    heads = range(nh)
    cols = [slice(hd * LANES, (hd + 1) * LANES) for hd in heads]
    cxh = [cx[:, cols[hd]].astype(BF16) for hd in heads]
    q = [_dot(cxh[hd], wq_ref[hd]).astype(BF16) for hd in heads]
    k = [(_dot(cxh[hd], wk_ref[hd]) * k_scale).astype(BF16) for hd in heads]
    v_t = [vt_ref[0, cols[hd], :] for hd in heads]
    kq = [_dot_nt(k[hd], q[hd]) for hd in heads]
    carry = [_dot_nt(c_ref[hd].astype(BF16), q[hd]) for hd in heads]
    n_q = [_dot_nt(n_ref[...].astype(BF16), q[hd]) for hd in heads]

    s, inter, m_t, stats = [], [], [], []
    for hd in heads:
        li_row = grow_ref[0, hd:hd + 1, :]
        b_row = grow_ref[0, nh + hd:nh + hd + 1, :]
        d_col = gcol_ref[0, :, hd:hd + 1] - gcol_ref[0, :, nh + hd:nh + hd + 1]
        m_st = m_ref[hd:hd + 1, 0:1]
        dmat = jnp.where(causal, b_row + d_col, -jnp.inf)
        m_inter = b_row + m_st
        m_t.append(jnp.maximum(m_inter, jnp.max(dmat, axis=0, keepdims=True)))
        s.append(kq[hd] * jnp.exp(dmat - m_t[hd]))
        inter.append(jnp.exp(m_inter - m_t[hd]))

        b_last = b_row[:, chunk - 1:chunk]
        w_row = b_last - b_row + li_row
        m_new = jnp.maximum(b_last + m_st, jnp.max(w_row, axis=1, keepdims=True))
        stats.append((m_new, jnp.exp(b_last + m_st - m_new), jnp.exp(w_row - m_new)))

    sv = [_dot(v_t[hd].astype(BF16), s[hd].astype(BF16)) for hd in heads]
    vk = [_dot((v_t[hd] * stats[hd][2]).astype(BF16), k[hd]) for hd in heads]
    row8 = lax.broadcasted_iota(jnp.int32, (SUBLANES, chunk), 0)
    ws_rows = jnp.zeros((SUBLANES, chunk), F32)
    for hd in heads:
        ws_rows = jnp.where(row8 == hd, stats[hd][2], ws_rows)
    ws_k = [_dot(ws_rows.astype(BF16), k[hd]) for hd in heads]

    for hd in heads:
        m_new, decay, _ = stats[hd]
        num = sv[hd] + inter[hd] * carry[hd]
        den = jnp.sum(s[hd], axis=0, keepdims=True) + inter[hd] * n_q[hd][hd:hd + 1, :]
        h = num * (1.0 / jnp.maximum(jnp.abs(den), jnp.exp(-m_t[hd])))
        c_ref[hd] = decay * c_ref[hd] + vk[hd]
        n_ref[hd:hd + 1, :] = decay * n_ref[hd:hd + 1, :] + ws_k[hd][hd:hd + 1, :]
        m_ref[hd:hd + 1, :] = jnp.broadcast_to(m_new, (1, LANES))
        hn = h * lax.rsqrt(jnp.sum(h * h, axis=0, keepdims=True) * (1.0 / MLSTM_HEAD_DIM) + NORM_EPS)
        out_ref[0, :, cols[hd]] = _sigmoid(o_ref[0, :, cols[hd]]) * (hn.T * ng_ref[:, cols[hd]])


def _mlstm(qk, v_t, o, gate_rows, gate_cols, conv_w, conv_b, wq, wk, norm_g):
    bsz, seq, width = qk.shape
    chunk = MLSTM_CHUNK
    tok = lambda b, i: (b, i, 0)
    tok_spec = pl.BlockSpec((1, chunk, width), tok)
    return pl.pallas_call(
        _mlstm_kernel,
        grid=(bsz, seq // chunk),
        in_specs=[tok_spec, pl.BlockSpec((1, width, chunk), lambda b, i: (b, 0, i)), tok_spec,
                  pl.BlockSpec((1, GATE_ROWS, chunk), lambda b, i: (b, 0, i)),
                  pl.BlockSpec((1, chunk, GATE_ROWS), tok)] + [
            _const_spec(a.shape) for a in (conv_w, conv_b, wq, wk, norm_g)],
        out_specs=tok_spec,
        out_shape=jax.ShapeDtypeStruct(qk.shape, F32),
        scratch_shapes=[pltpu.VMEM((SUBLANES, width), F32),
                        pltpu.VMEM((MLSTM_HEADS, LANES, LANES), F32),
                        pltpu.VMEM((SUBLANES, LANES), F32),
                        pltpu.VMEM((SUBLANES, LANES), F32)],
        compiler_params=_params(("parallel", "arbitrary")),
        name="mlstm_chunkwise",
    )(qk, v_t, o, gate_rows, gate_cols, conv_w, conv_b, wq, wk, norm_g)


FOX_BIAS_LANES = 6
FOX_FLAG_NEW_QUERY, FOX_FLAG_DIAGONAL, FOX_FLAG_FINISH = 1, 2, 4


def _fox_kernel(qa_ref, ka_ref, kb_ref, qo_ref, flag_ref, q_ref, qbias_ref, k_ref, kbias_ref,
                vt_ref, o_ref, qx_ref, s_ref, smax_ref, m_ref, acc_ref):
    step = pl.program_id(1)
    flags = flag_ref[step]
    tq, tk = q_ref.shape[1], k_ref.shape[1]
    hd = FOX_HEAD_DIM

    def pair_lanes(pair):
        return slice(pair * LANES, (pair + 1) * LANES)

    def reset_accumulators():
        m_ref[...] = jnp.full_like(m_ref, NEG_BIG)
        acc_ref[...] = jnp.zeros_like(acc_ref)

    @pl.when(step == 0)
    def _():
        s_ref[...] = jnp.full(s_ref.shape, NEG_BIG, F32)
        smax_ref[...] = jnp.full(smax_ref.shape, NEG_BIG, F32)
        reset_accumulators()

    @pl.when((flags & FOX_FLAG_NEW_QUERY) != 0)
    def _():
        lane = lax.broadcasted_iota(jnp.int32, (1, LANES), 1)
        for pair in range(FOX_PAIRS):
            q = q_ref[0, :, pair_lanes(pair)] * (hd ** -0.5 * LOG2_E)
            qb = qbias_ref[0, :, pair_lanes(pair)]
            for hh in range(2):
                in_head = (lane >= hh * hd) & (lane < (hh + 1) * hd)
                in_bias = (lane >= hh * FOX_BIAS_LANES) & (lane < (hh + 1) * FOX_BIAS_LANES)
                qx_ref[2 * pair + hh, :, :LANES] = jnp.where(in_head, q, 0.0).astype(BF16)
                qx_ref[2 * pair + hh, :, LANES:] = jnp.where(in_bias, qb, jnp.zeros_like(qb))

    def attend(diagonal):
        kx = [jnp.concatenate([k_ref[0, :, pair_lanes(pair)], kbias_ref[0, :, pair_lanes(pair)]],
                              axis=1) for pair in range(FOX_PAIRS)]
        probs = {}

        def score(hh):
            s = _dot_nt(kx[hh // 2], qx_ref[hh])
            if diagonal:
                visible = (lax.broadcasted_iota(jnp.int32, (tk, tq), 0)
                           <= lax.broadcasted_iota(jnp.int32, (tk, tq), 1))
                s = jnp.where(visible, s, NEG_BIG)
            s_ref[hh] = s
            smax_ref[hh] = jnp.max(s, axis=0, keepdims=True)

        def softmax(hh):
            s = s_ref[hh]
            m_prev = m_ref[hh]
            m_new = jnp.maximum(m_prev, smax_ref[hh])
            m_ref[hh] = m_new
            probs[hh] = (jnp.exp2(m_prev - m_new), jnp.exp2(s - m_new).astype(BF16))

        def value(hh):
            alpha, p = probs.pop(hh)
            acc_ref[hh] = alpha * acc_ref[hh] + _dot(vt_ref[0, hh], p)

        softmax(0)
        for hh in range(FOX_HEADS):
            score(hh)
            if hh + 1 < FOX_HEADS:
                softmax(hh + 1)
            value(hh)

    @pl.when((flags & FOX_FLAG_DIAGONAL) == 0)
    def _():
        attend(False)

    @pl.when((flags & FOX_FLAG_DIAGONAL) != 0)
    def _():
        attend(True)

    @pl.when((flags & FOX_FLAG_FINISH) != 0)
    def _():
        for pair in range(FOX_PAIRS):
            out_t = jnp.concatenate([acc_ref[hh, :hd, :] / acc_ref[hh, hd:hd + 1, :]
                                     for hh in (2 * pair, 2 * pair + 1)], axis=0)
            o_ref[0, :, pair_lanes(pair)] = out_t.T
        reset_accumulators()

    @pl.when(step == 0)
    def _():
        reset_accumulators()


def _fox(q, qb, k, kb, v_t, tq):
    bsz, seq, width = q.shape
    nq = seq // tq
    q_idx = [i for i in range(nq) for _ in range(i + 1)]
    k_idx = [j for i in range(nq) for j in range(i + 1)]
    n_pairs = len(q_idx)
    q_new = q_idx + [q_idx[-1]]
    k_new = k_idx + [k_idx[-1]]
    k_old = [0] + k_idx
    q_old = [0] + q_idx
    flags = []
    for n in range(n_pairs + 1):
        f = 0
        if n < n_pairs and k_idx[n] == 0:
            f |= FOX_FLAG_NEW_QUERY
        if n < n_pairs and k_idx[n] == q_idx[n]:
            f |= FOX_FLAG_DIAGONAL
        if n >= 1 and (n == n_pairs or k_idx[n] == 0):
            f |= FOX_FLAG_FINISH
        flags.append(f)
    tables = [jnp.asarray(t, jnp.int32) for t in (q_new, k_new, k_old, q_old, flags)]
    q_spec = pl.BlockSpec((1, tq, width), lambda b, s, qa, ka, kb_, qo, fl: (b, qa[s], 0))
    k_spec = pl.BlockSpec((1, tq, width), lambda b, s, qa, ka, kb_, qo, fl: (b, ka[s], 0))
    grid_spec = pltpu.PrefetchScalarGridSpec(
        num_scalar_prefetch=len(tables),
        grid=(bsz, n_pairs + 1),
        in_specs=[q_spec, q_spec, k_spec, k_spec,
                  pl.BlockSpec((1, FOX_HEADS, FOX_VT_ROWS, tq),
                               lambda b, s, qa, ka, kb_, qo, fl: (b, 0, 0, kb_[s]))],
        out_specs=pl.BlockSpec((1, tq, width), lambda b, s, qa, ka, kb_, qo, fl: (b, qo[s], 0)),
        scratch_shapes=[pltpu.VMEM((FOX_HEADS, tq, 2 * LANES), BF16),
                        pltpu.VMEM((FOX_HEADS, tq, tq), F32),
                        pltpu.VMEM((FOX_HEADS, 1, tq), F32),
                        pltpu.VMEM((FOX_HEADS, 1, tq), F32),
                        pltpu.VMEM((FOX_HEADS, FOX_VT_ROWS, tq), F32)],
    )
    return pl.pallas_call(
        _fox_kernel, grid_spec=grid_spec,
        out_shape=jax.ShapeDtypeStruct((bsz, seq, width), F32),
        compiler_params=_params(("parallel", "arbitrary")),
        name="forgetting_attention",
    )(*tables, q, qb, k, kb, v_t)


def _fox_bias_lanes(split):
    bsz, _, _, seq = split.shape
    lo = 2 * MLSTM_HEADS
    pieces = jnp.swapaxes(split[:, :, lo:lo + FOX_HEADS, :], 1, 2).astype(BF16)
    ones = jnp.ones_like(pieces)

    def lanes(per_head):
        per_pair = per_head.reshape(bsz, FOX_PAIRS, 2 * FOX_BIAS_LANES, seq)
        per_pair = jnp.pad(per_pair, ((0, 0), (0, 0), (0, LANES - 2 * FOX_BIAS_LANES), (0, 0)))
        return jnp.swapaxes(per_pair.reshape(bsz, FOX_PAIRS * LANES, seq), 1, 2)

    return (lanes(jnp.concatenate([pieces, ones], axis=2)),
            lanes(jnp.concatenate([ones, -pieces], axis=2)))


FFN_COLS = 256


def _merge_ffn_kernel(x_ref, ys_ref, ym_ref, yf_ref, gate_ref, g1_ref, sh_ref, sc_ref, g2_ref,
                      ws_ref, wm_ref, wf_ref, wo_ref, gain_ref, w1_ref, w3_ref, w2_ref, fg_ref,
                      o_ref, *, final_norm):
    d = x_ref.shape[-1]

    def gate(i):
        return _sigmoid(gate_ref[0, :, i * d:(i + 1) * d].astype(F32))

    merged = (gate(0) * _dot(ys_ref[0].astype(BF16), ws_ref[...])
              + gate(1) * _dot(ym_ref[0].astype(BF16), wm_ref[...])
              + gate(2) * _dot(yf_ref[0].astype(BF16), wf_ref[...]))
    x = x_ref[0] + g1_ref[0] * _dot(merged.astype(BF16), wo_ref[...])

    h = _rms_modulate(x, gain_ref[...], sc_ref[0], sh_ref[0]).astype(BF16)
    starts = list(range(0, w1_ref.shape[1], FFN_COLS))

    def up(c0):
        return _dot(h, w1_ref[:, c0:c0 + FFN_COLS]), _dot(h, w3_ref[:, c0:c0 + FFN_COLS])

    acc = jnp.zeros(x.shape, F32)
    nxt = up(starts[0])
    for i, c0 in enumerate(starts):
        a, b = nxt
        if i + 1 < len(starts):
            nxt = up(starts[i + 1])
        acc = acc + _dot((_silu(a) * b).astype(BF16), w2_ref[c0:c0 + FFN_COLS, :])
    y = x + g2_ref[0] * acc
    if final_norm:
        y = (y * lax.rsqrt(jnp.mean(y * y, axis=-1, keepdims=True) + NORM_EPS)) * fg_ref[...]
    o_ref[0] = y


def _merge_ffn(acts, mod_vectors, weights, final_norm, tm):
    bsz, seq, d = acts[0].shape
    tok = lambda b, i: (b, i, 0)
    vec = pl.BlockSpec((1, 1, d), lambda b, i: (b, 0, 0))
    return pl.pallas_call(
        functools.partial(_merge_ffn_kernel, final_norm=final_norm),
        grid=(bsz, seq // tm),
        in_specs=[pl.BlockSpec((1, tm, a.shape[-1]), tok) for a in acts]
        + [vec] * len(mod_vectors) + [_const_spec(w.shape) for w in weights],
        out_specs=pl.BlockSpec((1, tm, d), tok),
        out_shape=jax.ShapeDtypeStruct(acts[0].shape, F32),
        compiler_params=_params(("parallel", "parallel")),
        name="merge_swiglu",
    )(*acts, *mod_vectors, *weights)


def _pad_heads(w, heads, head_dim):
    lead = w.shape[:-1]
    w = w.reshape(lead + (heads, head_dim))
    w = jnp.pad(w, [(0, 0)] * len(lead) + [(0, 0), (0, LANES - head_dim)])
    return w.reshape(lead + (heads * LANES,))


def _layer_weights(w_in, b_in):
    sizes = (S5_WIDTH, MLSTM_WIDTH, MLSTM_WIDTH, MLSTM_WIDTH, MLSTM_HEADS, MLSTM_HEADS,
             FOX_WIDTH, FOX_WIDTH, FOX_WIDTH, FOX_HEADS)
    offs = [0]
    for n in sizes:
        offs.append(offs[-1] + n)
    both = jnp.concatenate([w_in, b_in[None, :]], axis=0)
    part = [both[:, offs[i]:offs[i + 1]] for i in range(len(sizes))] + [both[:, offs[-1]:]]
    pad_m = lambda w: _pad_heads(w, MLSTM_HEADS, MLSTM_HEAD_DIM)
    cat = jnp.concatenate([part[0], pad_m(part[1]), pad_m(part[3]),
                           part[6], part[7], part[10]], axis=1)
    small = jnp.concatenate([part[4], part[5], part[9]], axis=1)
    small = jnp.pad(small, ((0, 0), (0, GATE_ROWS - small.shape[1])))
    cat_t = jnp.concatenate([small, part[8], pad_m(part[2])], axis=1).T
    return (cat[:-1].astype(BF16), cat[-1:], cat_t[:, :-1].astype(BF16), cat_t[:, -1:])


def kernel(x, c, mod_w, mod_b, norm1_g, norm2_g, w_in, b_in, s5_a_re, s5_a_im, s5_log_dt, s5_b_re, s5_b_im, s5_c_re, s5_c_im, s5_d, s5_w_glu, mlstm_conv_w, mlstm_conv_b, mlstm_wq, mlstm_wk, mlstm_norm_g, w_up_s5, w_up_mlstm, w_up_fox, w_out, ffn_w1, ffn_w3, ffn_w2, final_g):
    bsz, seq, d = x.shape
    depth = mod_w.shape[0]
    assert seq % MLSTM_CHUNK == 0
    tm = _row_tile(seq, 512)
    tq = _row_tile(seq, 1024)
    ts = _row_tile(seq, 512)

    mod = _modulation(c, mod_w, mod_b)
    final_gain = final_g.reshape(1, d)
    for l in range(depth):
        sh1, sc1, g1, sh2, sc2, g2 = [mod[l, :, i * d:(i + 1) * d].reshape(bsz, 1, d)
                                      for i in range(6)]
        w_cat, b_cat, w_t, b_t = _layer_weights(w_in[l], b_in[l])
        (s5_u, m_qk, m_o, f_q, f_k, gates, gate_pre, f_vt, m_vt) = _input_projection(
            x, sh1, sc1, norm1_g[l].reshape(1, d), w_cat, b_cat, w_t, b_t, tm)

        gate_rows, gate_split = _gate_scan(gate_pre)
        gate_cols = jnp.swapaxes(gate_rows, 1, 2)
        f_qb, f_kb = _fox_bias_lanes(gate_split)

        bmat, cmat, astep, apow = _s5_tables(s5_a_re[l], s5_a_im[l], s5_log_dt[l], s5_b_re[l],
                                             s5_b_im[l], s5_c_re[l], s5_c_im[l])
        y_s5 = _s5(s5_u, bmat, cmat, astep, apow, s5_d[l].reshape(1, -1),
                   s5_w_glu[l].astype(BF16), ts)

        pad_hh = ((0, 0), (0, LANES - MLSTM_HEAD_DIM), (0, LANES - MLSTM_HEAD_DIM))
        y_m = _mlstm(m_qk, m_vt, m_o, gate_rows, gate_cols,
                     _pad_heads(mlstm_conv_w[l], MLSTM_HEADS, MLSTM_HEAD_DIM),
                     _pad_heads(mlstm_conv_b[l].reshape(1, -1), MLSTM_HEADS, MLSTM_HEAD_DIM),
                     jnp.pad(mlstm_wq[l], pad_hh).astype(BF16),
                     jnp.pad(mlstm_wk[l], pad_hh).astype(BF16),
                     _pad_heads(mlstm_norm_g[l].reshape(1, -1), MLSTM_HEADS, MLSTM_HEAD_DIM))

        y_f = _fox(f_q, f_qb, f_k, f_kb, f_vt, tq)

        w_up_m = _pad_heads(w_up_mlstm[l].T, MLSTM_HEADS, MLSTM_HEAD_DIM).T
        weights = (w_up_s5[l].astype(BF16), w_up_m.astype(BF16), w_up_fox[l].astype(BF16),
                   w_out[l].astype(BF16), norm2_g[l].reshape(1, d), ffn_w1[l].astype(BF16),
                   ffn_w3[l].astype(BF16), ffn_w2[l].astype(BF16), final_gain)
        x = _merge_ffn((x, y_s5, y_m, y_f, gates), (g1, sh2, sc2, g2), weights,
                       l == depth - 1, tm)
    return x
```

```python
import functools
import math

import jax
import jax.numpy as jnp
from jax import lax
from jax.experimental import pallas as pl
from jax.experimental.pallas import tpu as pltpu

F32 = jnp.float32
BF16 = jnp.bfloat16

LANES = 128
SUBLANES = 8
VMEM_LIMIT_BYTES = 56 * 1024 * 1024

NORM_EPS = 1e-6
S5_GROUPS = 16
S5_GROUP_DIM = 16
S5_STATE = 64
S5_WIDTH = S5_GROUPS * S5_GROUP_DIM
S5_CHANNELS = S5_GROUPS * S5_STATE
MLSTM_HEADS = 4
MLSTM_HEAD_DIM = 96
MLSTM_WIDTH = MLSTM_HEADS * MLSTM_HEAD_DIM
MLSTM_PAD_WIDTH = MLSTM_HEADS * LANES
MLSTM_CHUNK = 128
CONV_TAPS = 4
FOX_HEADS = 6
FOX_HEAD_DIM = 64
FOX_WIDTH = FOX_HEADS * FOX_HEAD_DIM
FOX_PAIRS = FOX_WIDTH // LANES
GATE_ROWS = 16
NEG_BIG = -1e30
LOG2_E = math.log2(math.e)

NT_DIMS = (((1,), (1,)), ((), ()))


def _dot(a, b):
    return jnp.dot(a, b, preferred_element_type=F32)


def _dot_nt(a, b):
    return lax.dot_general(a, b, NT_DIMS, preferred_element_type=F32)


def _sigmoid(x):
    return 1.0 / (1.0 + jnp.exp(-x))


def _silu(x):
    return x * _sigmoid(x)


def _log_sigmoid(x):
    return jnp.minimum(x, 0.0) - jnp.log1p(jnp.exp(-jnp.abs(x)))


def _rms_modulate(x, gain, scale, shift):
    y = x * lax.rsqrt(jnp.mean(x * x, axis=-1, keepdims=True) + NORM_EPS)
    return (y * gain) * (1.0 + scale) + shift


def _params(semantics):
    return pltpu.CompilerParams(dimension_semantics=semantics, vmem_limit_bytes=VMEM_LIMIT_BYTES)


def _const_spec(shape):
    nd = len(shape)
    return pl.BlockSpec(shape, lambda *_: (0,) * nd, pipeline_mode=pl.Buffered(1))


def _row_tile(seq, want):
    t = min(want, seq)
    assert seq % t == 0 and t % SUBLANES == 0
    return t


def _mod_kernel(c_ref, w_ref, b_ref, o_ref):
    cf = _silu(c_ref[...]).astype(BF16)
    o_ref[0] = _dot(cf, w_ref[0].astype(BF16)) + b_ref[0]


def _modulation(c, mod_w, mod_b):
    depth, d, six_d = mod_w.shape
    bsz = c.shape[0]
    n_col = six_d // d
    return pl.pallas_call(
        _mod_kernel,
        grid=(depth, n_col),
        in_specs=[pl.BlockSpec((bsz, d), lambda l, j: (0, 0)),
                  pl.BlockSpec((1, d, d), lambda l, j: (l, 0, j)),
                  pl.BlockSpec((1, 1, d), lambda l, j: (l, 0, j))],
        out_specs=pl.BlockSpec((1, bsz, d), lambda l, j: (l, 0, j)),
        out_shape=jax.ShapeDtypeStruct((depth, bsz, six_d), F32),
        compiler_params=_params(("parallel", "parallel")),
        name="adaln_modulation",
    )(c, mod_w, mod_b.reshape(depth, 1, six_d))


IN_SEGMENTS = ((S5_WIDTH, F32), (MLSTM_PAD_WIDTH, F32), (MLSTM_PAD_WIDTH, F32),
               (FOX_WIDTH, F32), (FOX_WIDTH, BF16))
IN_T_SEGMENTS = ((GATE_ROWS, F32), (FOX_WIDTH, BF16), (MLSTM_PAD_WIDTH, F32))
IN_T_FOX_VALUES = 1
FOX_VT_ROWS = FOX_HEAD_DIM + 16
DOT_COLS = 512


def _inproj_kernel(x_ref, sh_ref, sc_ref, g_ref, w_ref, b_ref, wt_ref, bt_ref, *out_refs):
    h = _rms_modulate(x_ref[0], g_ref[...], sc_ref[0], sh_ref[0]).astype(BF16)
    n_t = len(IN_T_SEGMENTS)
    start = 0
    for o_ref in out_refs[:-n_t]:
        width = o_ref.shape[-1]
        for c0 in range(0, width, DOT_COLS):
            cw = min(DOT_COLS, width - c0)
            z = _dot(h, w_ref[:, start + c0:start + c0 + cw]) + b_ref[:, start + c0:start + c0 + cw]
            o_ref[0, :, c0:c0 + cw] = z.astype(o_ref.dtype)
        start += width
    start = 0
    for (rows, _), o_ref in zip(IN_T_SEGMENTS, out_refs[-n_t:]):
        z = _dot_nt(wt_ref[start:start + rows, :], h) + bt_ref[start:start + rows, :]
        if o_ref.ndim == 3:
            o_ref[0] = z.astype(o_ref.dtype)
        else:
            for hd in range(FOX_HEADS):
                o_ref[0, hd, :FOX_HEAD_DIM, :] = z[hd * FOX_HEAD_DIM:(hd + 1) * FOX_HEAD_DIM].astype(
                    o_ref.dtype)
                o_ref[0, hd, FOX_HEAD_DIM:, :] = jnp.ones(
                    (FOX_VT_ROWS - FOX_HEAD_DIM, z.shape[1]), o_ref.dtype)
        start += rows


def _input_projection(x, shift, scale, gain, w_cat, b_cat, w_t, b_t, tm):
    bsz, seq, d = x.shape
    segs = IN_SEGMENTS + ((w_cat.shape[1] - sum(w for w, _ in IN_SEGMENTS), BF16),)
    tok = lambda b, i: (b, i, 0)
    vec = lambda b, i: (b, 0, 0)
    out_shape = [jax.ShapeDtypeStruct((bsz, seq, w), dt) for w, dt in segs]
    out_specs = [pl.BlockSpec((1, tm, w), tok) for w, _ in segs]
    for idx, (r, dt) in enumerate(IN_T_SEGMENTS):
        if idx == IN_T_FOX_VALUES:
            out_shape.append(jax.ShapeDtypeStruct((bsz, FOX_HEADS, FOX_VT_ROWS, seq), dt))
            out_specs.append(pl.BlockSpec((1, FOX_HEADS, FOX_VT_ROWS, tm),
                                          lambda b, i: (b, 0, 0, i)))
        else:
            out_shape.append(jax.ShapeDtypeStruct((bsz, r, seq), dt))
            out_specs.append(pl.BlockSpec((1, r, tm), lambda b, i: (b, 0, i)))
    return pl.pallas_call(
        _inproj_kernel,
        grid=(bsz, seq // tm),
        in_specs=[pl.BlockSpec((1, tm, d), tok),
                  pl.BlockSpec((1, 1, d), vec), pl.BlockSpec((1, 1, d), vec),
                  _const_spec(gain.shape), _const_spec(w_cat.shape), _const_spec(b_cat.shape),
                  _const_spec(w_t.shape), _const_spec(b_t.shape)],
        out_specs=out_specs,
        out_shape=out_shape,
        compiler_params=_params(("parallel", "parallel")),
        name="norm_input_projection",
    )(x, shift, scale, gain, w_cat, b_cat, w_t, b_t)


def _bf16_round(x):
    return x.astype(BF16).astype(F32)


def _gate_scan_kernel(x_ref, o_ref, split_ref):
    x = x_ref[0]
    seq = x.shape[1]
    seg = MLSTM_CHUNK
    lane = lax.broadcasted_iota(jnp.int32, x.shape, 1) & (seg - 1)
    v = _log_sigmoid(x)
    k = 1
    while k < seg:
        v = v + jnp.where(lane >= k, pltpu.roll(v, k, 1), 0.0)
        k *= 2
    row = lax.broadcasted_iota(jnp.int32, (GATE_ROWS, seg), 0)
    carry = jnp.zeros((GATE_ROWS, 1), F32)
    for j in range(seq // seg):
        sl = slice(j * seg, (j + 1) * seg)
        local = v[:, sl]
        total = local + carry
        o_ref[0, :, sl] = jnp.where(row < MLSTM_HEADS, x[:, sl],
                                    jnp.where(row < 2 * MLSTM_HEADS, local, total))
        scaled = total * LOG2_E
        hi = _bf16_round(scaled)
        mid = _bf16_round(scaled - hi)
        split_ref[0, 0, :, sl] = hi
        split_ref[0, 1, :, sl] = mid
        split_ref[0, 2, :, sl] = _bf16_round((scaled - hi) - mid)
        carry = carry + local[:, seg - 1:seg]


def _gate_scan(pre):
    bsz, rows, seq = pre.shape
    spec = pl.BlockSpec((1, rows, seq), lambda b: (b, 0, 0))
    return pl.pallas_call(
        _gate_scan_kernel, grid=(bsz,), in_specs=[spec],
        out_specs=[spec, pl.BlockSpec((1, 3, rows, seq), lambda b: (b, 0, 0, 0))],
        out_shape=[jax.ShapeDtypeStruct(pre.shape, F32),
                   jax.ShapeDtypeStruct((bsz, 3, rows, seq), F32)],
        compiler_params=_params(("parallel",)),
        name="gate_scan",
    )(pre)


SCAN_SHIFTS = (1, 2, 4)


def _s5_kernel(u_ref, bmat_ref, cmat_ref, astep_ref, apow_ref, d_ref, wglu_ref, o_ref,
               st_ref, carry_ref):
    nch = S5_CHANNELS

    @pl.when(pl.program_id(1) == 0)
    def _():
        carry_ref[...] = jnp.zeros_like(carry_ref)

    u = u_ref[0]
    st_ref[...] = _dot(u.astype(BF16), bmat_ref[...])
    n_blocks = u.shape[0] // SUBLANES

    def body(i, carry):
        cr, ci = carry
        r0 = pl.multiple_of(i * SUBLANES, SUBLANES)
        xr = st_ref[pl.ds(r0, SUBLANES), :nch]
        xi = st_ref[pl.ds(r0, SUBLANES), nch:]
        for k, shift in enumerate(SCAN_SHIFTS):
            ar, ai = astep_ref[k, :, :nch], astep_ref[k, :, nch:]
            sr, si = pltpu.roll(xr, shift, 0), pltpu.roll(xi, shift, 0)
            xr, xi = xr + (ar * sr - ai * si), xi + (ar * si + ai * sr)
        pr, pi = apow_ref[:, :nch], apow_ref[:, nch:]
        xr, xi = xr + (pr * cr - pi * ci), xi + (pr * ci + pi * cr)
        st_ref[pl.ds(r0, SUBLANES), :nch] = xr
        st_ref[pl.ds(r0, SUBLANES), nch:] = xi
        last = SUBLANES - 1
        return (jnp.broadcast_to(xr[last:, :], xr.shape), jnp.broadcast_to(xi[last:, :], xi.shape))

    cr, ci = lax.fori_loop(0, n_blocks, body, (carry_ref[:, :nch], carry_ref[:, nch:]))
    carry_ref[:, :nch] = cr
    carry_ref[:, nch:] = ci

    y = _dot(st_ref[...].astype(BF16), cmat_ref[...]) + d_ref[...] * u
    y = y * (0.5 * (1.0 + jnp.tanh(math.sqrt(2.0 / math.pi) * (y + 0.044715 * (y * y * y)))))
    o_ref[0] = y * _sigmoid(_dot(y.astype(BF16), wglu_ref[...]))


def _s5(u, bmat, cmat, astep, apow, d_skip, w_glu, tb):
    bsz, seq, width = u.shape
    tok = lambda b, i: (b, i, 0)
    return pl.pallas_call(
        _s5_kernel,
        grid=(bsz, seq // tb),
        in_specs=[pl.BlockSpec((1, tb, width), tok)] + [
            _const_spec(a.shape) for a in (bmat, cmat, astep, apow, d_skip, w_glu)],
        out_specs=pl.BlockSpec((1, tb, width), tok),
        out_shape=jax.ShapeDtypeStruct(u.shape, F32),
        scratch_shapes=[pltpu.VMEM((tb, 2 * S5_CHANNELS), F32),
                        pltpu.VMEM((SUBLANES, 2 * S5_CHANNELS), F32)],
        compiler_params=_params(("parallel", "arbitrary")),
        name="s5_scan",
    )(u, bmat, cmat, astep, apow, d_skip, w_glu)


def _s5_tables(a_re, a_im, log_dt, b_re, b_im, c_re, c_im):
    g, n, p = b_re.shape
    dt = jnp.exp(log_dt)[:, None]
    mag = jnp.exp(dt * a_re)
    ang = dt * a_im
    abar_r, abar_i = mag * jnp.cos(ang), mag * jnp.sin(ang)
    den = a_re * a_re + a_im * a_im
    pr, qi = abar_r - 1.0, abar_i
    coef_r = (pr * a_re + qi * a_im) / den
    coef_i = (qi * a_re - pr * a_im) / den
    bbar_r = coef_r[..., None] * b_re - coef_i[..., None] * b_im
    bbar_i = coef_r[..., None] * b_im + coef_i[..., None] * b_re
    eye = jnp.eye(g, dtype=F32)

    def in_block(bb):
        return jnp.einsum('gnp,gh->gphn', bb, eye).reshape(g * p, g * n)

    def out_block(cc):
        return jnp.einsum('gpn,gh->gnhp', cc, eye).reshape(g * n, g * p)

    bmat = jnp.concatenate([in_block(bbar_r), in_block(bbar_i)], axis=1).astype(BF16)
    cmat = jnp.concatenate([out_block(c_re), -out_block(c_im)], axis=0).astype(BF16)

    def power(e):
        e = jnp.asarray(e, F32)[:, None, None]
        m = jnp.exp(e * (dt * a_re)[None])
        th = e * ang[None]
        return (m * jnp.cos(th)).reshape(-1, g * n), (m * jnp.sin(th)).reshape(-1, g * n)

    rows = jnp.arange(SUBLANES)
    sr, si = power(SCAN_SHIFTS)
    keep = (rows[None, :] >= jnp.asarray(SCAN_SHIFTS)[:, None]).astype(F32)[:, :, None]
    astep = jnp.concatenate([keep * sr[:, None, :], keep * si[:, None, :]], axis=-1)
    wr, wi = power(rows + 1)
    apow = jnp.concatenate([wr, wi], axis=-1)
    return bmat, cmat, astep, apow


def _mlstm_kernel(qk_ref, vt_ref, o_ref, grow_ref, gcol_ref, cw_ref, cb_ref, wq_ref, wk_ref,
                  ng_ref, out_ref, tail_ref, c_ref, n_ref, m_ref):
    chunk = qk_ref.shape[1]
    nh = MLSTM_HEADS

    @pl.when(pl.program_id(1) == 0)
    def _():
        tail_ref[...] = jnp.zeros_like(tail_ref)
        c_ref[...] = jnp.zeros_like(c_ref)
        n_ref[...] = jnp.zeros_like(n_ref)
        m_ref[...] = jnp.zeros_like(m_ref)

    x = qk_ref[0]
    ext = jnp.concatenate([tail_ref[...], x], axis=0)
    conv = cb_ref[...] + cw_ref[CONV_TAPS - 1:CONV_TAPS, :] * x
    for tap in range(CONV_TAPS - 1):
        back = CONV_TAPS - 1 - tap
        conv = conv + cw_ref[tap:tap + 1, :] * ext[SUBLANES - back:SUBLANES - back + chunk, :]
    tail_ref[...] = x[chunk - SUBLANES:, :]
    cx = _silu(conv)

    key_id = lax.broadcasted_iota(jnp.int32, (chunk, chunk), 0)
    query_id = lax.broadcasted_iota(jnp.int32, (chunk, chunk), 1)
    causal = key_id <= query_id
    k_scale = MLSTM_HEAD_DIM ** -0.5
    heads = range(nh)
    cols = [slice(hd * LANES, (hd + 1) * LANES) for hd in heads]
    cxh = [cx[:, cols[hd]].astype(BF16) for hd in heads]
    q = [_dot(cxh[hd], wq_ref[hd]).astype(BF16) for hd in heads]
    k = [(_dot(cxh[hd], wk_ref[hd]) * k_scale).astype(BF16) for hd in heads]
    v_t = [vt_ref[0, cols[hd], :] for hd in heads]
    kq = [_dot_nt(k[hd], q[hd]) for hd in heads]
    carry = [_dot_nt(c_ref[hd].astype(BF16), q[hd]) for hd in heads]
    n_q = [_dot_nt(n_ref[...].astype(BF16), q[hd]) for hd in heads]

    s, inter, m_t, stats = [], [], [], []
    for hd in heads:
        li_row = grow_ref[0, hd:hd + 1, :]
        b_row = grow_ref[0, nh + hd:nh + hd + 1, :]
        d_col = gcol_ref[0, :, hd:hd + 1] - gcol_ref[0, :, nh + hd:nh + hd + 1]
        m_st = m_ref[hd:hd + 1, 0:1]
        dmat = jnp.where(causal, b_row + d_col, -jnp.inf)
        m_inter = b_row + m_st
        m_t.append(jnp.maximum(m_inter, jnp.max(dmat, axis=0, keepdims=True)))
        s.append(kq[hd] * jnp.exp(dmat - m_t[hd]))
        inter.append(jnp.exp(m_inter - m_t[hd]))

        b_last = b_row[:, chunk - 1:chunk]
        w_row = b_last - b_row + li_row
        m_new = jnp.maximum(b_last + m_st, jnp.max(w_row, axis=1, keepdims=True))
        stats.append((m_new, jnp.exp(b_last + m_st - m_new), jnp.exp(w_row - m_new)))

    sv = [_dot(v_t[hd].astype(BF16), s[hd].astype(BF16)) for hd in heads]
    vk = [_dot((v_t[hd] * stats[hd][2]).astype(BF16), k[hd]) for hd in heads]
    row8 = lax.broadcasted_iota(jnp.int32, (SUBLANES, chunk), 0)
    ws_rows = jnp.zeros((SUBLANES, chunk), F32)
    for hd in heads:
        ws_rows = jnp.where(row8 == hd, stats[hd][2], ws_rows)
    ws_k = [_dot(ws_rows.astype(BF16), k[hd]) for hd in heads]

    for hd in heads:
        m_new, decay, _ = stats[hd]
        num = sv[hd] + inter[hd] * carry[hd]
        den = jnp.sum(s[hd], axis=0, keepdims=True) + inter[hd] * n_q[hd][hd:hd + 1, :]
        h = num * (1.0 / jnp.maximum(jnp.abs(den), jnp.exp(-m_t[hd])))
        c_ref[hd] = decay * c_ref[hd] + vk[hd]
        n_ref[hd:hd + 1, :] = decay * n_ref[hd:hd + 1, :] + ws_k[hd][hd:hd + 1, :]
        m_ref[hd:hd + 1, :] = jnp.broadcast_to(m_new, (1, LANES))
        hn = h * lax.rsqrt(jnp.sum(h * h, axis=0, keepdims=True) * (1.0 / MLSTM_HEAD_DIM) + NORM_EPS)
        out_ref[0, :, cols[hd]] = _sigmoid(o_ref[0, :, cols[hd]]) * (hn.T * ng_ref[:, cols[hd]])


def _mlstm(qk, v_t, o, gate_rows, gate_cols, conv_w, conv_b, wq, wk, norm_g):
    bsz, seq, width = qk.shape
    chunk = MLSTM_CHUNK
    tok = lambda b, i: (b, i, 0)
    tok_spec = pl.BlockSpec((1, chunk, width), tok)
    return pl.pallas_call(
        _mlstm_kernel,
        grid=(bsz, seq // chunk),
        in_specs=[tok_spec, pl.BlockSpec((1, width, chunk), lambda b, i: (b, 0, i)), tok_spec,
                  pl.BlockSpec((1, GATE_ROWS, chunk), lambda b, i: (b, 0, i)),
                  pl.BlockSpec((1, chunk, GATE_ROWS), tok)] + [
            _const_spec(a.shape) for a in (conv_w, conv_b, wq, wk, norm_g)],
        out_specs=tok_spec,
        out_shape=jax.ShapeDtypeStruct(qk.shape, F32),
        scratch_shapes=[pltpu.VMEM((SUBLANES, width), F32),
                        pltpu.VMEM((MLSTM_HEADS, LANES, LANES), F32),
                        pltpu.VMEM((SUBLANES, LANES), F32),
                        pltpu.VMEM((SUBLANES, LANES), F32)],
        compiler_params=_params(("parallel", "arbitrary")),
        name="mlstm_chunkwise",
    )(qk, v_t, o, gate_rows, gate_cols, conv_w, conv_b, wq, wk, norm_g)


FOX_BIAS_LANES = 6
FOX_FLAG_NEW_QUERY, FOX_FLAG_DIAGONAL, FOX_FLAG_FINISH = 1, 2, 4
FOX_KEY_CHUNK = 256


def _fox_kernel(qa_ref, ka_ref, kb_ref, qo_ref, flag_ref, q_ref, qbias_ref, k_ref, kbias_ref,
                vt_ref, o_ref, qx_ref, s_ref, smax_ref, m_ref, acc_ref):
    step = pl.program_id(1)
    flags = flag_ref[step]
    tq, tk = q_ref.shape[1], k_ref.shape[1]
    hd = FOX_HEAD_DIM

    def pair_lanes(pair):
        return slice(pair * LANES, (pair + 1) * LANES)

    def reset_accumulators():
        m_ref[...] = jnp.full_like(m_ref, NEG_BIG)
        acc_ref[...] = jnp.zeros_like(acc_ref)

    @pl.when(step == 0)
    def _():
        s_ref[...] = jnp.full(s_ref.shape, NEG_BIG, F32)
        smax_ref[...] = jnp.full(smax_ref.shape, NEG_BIG, F32)
        reset_accumulators()

    @pl.when((flags & FOX_FLAG_NEW_QUERY) != 0)
    def _():
        lane = lax.broadcasted_iota(jnp.int32, (1, LANES), 1)
        for pair in range(FOX_PAIRS):
            q = q_ref[0, :, pair_lanes(pair)] * (hd ** -0.5 * LOG2_E)
            qb = qbias_ref[0, :, pair_lanes(pair)]
            for hh in range(2):
                in_head = (lane >= hh * hd) & (lane < (hh + 1) * hd)
                in_bias = (lane >= hh * FOX_BIAS_LANES) & (lane < (hh + 1) * FOX_BIAS_LANES)
                qx_ref[2 * pair + hh, :, :LANES] = jnp.where(in_head, q, 0.0).astype(BF16)
                qx_ref[2 * pair + hh, :, LANES:] = jnp.where(in_bias, qb, jnp.zeros_like(qb))

    def attend(diagonal):
        kx = [jnp.concatenate([k_ref[0, :, pair_lanes(pair)], kbias_ref[0, :, pair_lanes(pair)]],
                              axis=1) for pair in range(FOX_PAIRS)]
        kc = min(tk, FOX_KEY_CHUNK)
        key_chunks = range(0, tk, kc)

        def score(hh, c0, smax):
            s = _dot_nt(kx[hh // 2][c0:c0 + kc], qx_ref[hh])
            if diagonal:
                visible = (lax.broadcasted_iota(jnp.int32, (kc, tq), 0) + c0
                           <= lax.broadcasted_iota(jnp.int32, (kc, tq), 1))
                s = jnp.where(visible, s, NEG_BIG)
            s_ref[hh, c0:c0 + kc, :] = s
            cmax = jnp.max(s, axis=0, keepdims=True)
            return cmax if smax is None else jnp.maximum(smax, cmax)

        def rescale(hh):
            m_prev = m_ref[hh]
            m_new = jnp.maximum(m_prev, smax_ref[hh])
            m_ref[hh] = m_new
            return m_new, jnp.exp2(m_prev - m_new) * acc_ref[hh]

        def value(hh, c0, m_new, acc):
            p = jnp.exp2(s_ref[hh, c0:c0 + kc, :] - m_new).astype(BF16)
            return acc + _dot(vt_ref[0, hh, :, c0:c0 + kc], p)

        m_new, acc = rescale(0)
        for c0 in key_chunks:
            acc = value(0, c0, m_new, acc)
        acc_ref[0] = acc
        for hh in range(FOX_HEADS):
            nxt = hh + 1
            if nxt < FOX_HEADS:
                m_new, acc = rescale(nxt)
            smax = None
            for c0 in key_chunks:
                smax = score(hh, c0, smax)
                if nxt < FOX_HEADS:
                    acc = value(nxt, c0, m_new, acc)
            smax_ref[hh] = smax
            if nxt < FOX_HEADS:
                acc_ref[nxt] = acc

    @pl.when((flags & FOX_FLAG_DIAGONAL) == 0)
    def _():
        attend(False)

    @pl.when((flags & FOX_FLAG_DIAGONAL) != 0)
    def _():
        attend(True)

    @pl.when((flags & FOX_FLAG_FINISH) != 0)
    def _():
        for pair in range(FOX_PAIRS):
            out_t = jnp.concatenate([acc_ref[hh, :hd, :] / acc_ref[hh, hd:hd + 1, :]
                                     for hh in (2 * pair, 2 * pair + 1)], axis=0)
            o_ref[0, :, pair_lanes(pair)] = out_t.T
        reset_accumulators()

    @pl.when(step == 0)
    def _():
        reset_accumulators()


def _fox(q, qb, k, kb, v_t, tq):
    bsz, seq, width = q.shape
    nq = seq // tq
    q_idx = [i for i in range(nq) for _ in range(i + 1)]
    k_idx = [j for i in range(nq) for j in range(i + 1)]
    n_pairs = len(q_idx)
    q_new = q_idx + [q_idx[-1]]
    k_new = k_idx + [k_idx[-1]]
    k_old = [0] + k_idx
    q_old = [0] + q_idx
    flags = []
    for n in range(n_pairs + 1):
        f = 0
        if n < n_pairs and k_idx[n] == 0:
            f |= FOX_FLAG_NEW_QUERY
        if n < n_pairs and k_idx[n] == q_idx[n]:
            f |= FOX_FLAG_DIAGONAL
        if n >= 1 and (n == n_pairs or k_idx[n] == 0):
            f |= FOX_FLAG_FINISH
        flags.append(f)
    tables = [jnp.asarray(t, jnp.int32) for t in (q_new, k_new, k_old, q_old, flags)]
    q_spec = pl.BlockSpec((1, tq, width), lambda b, s, qa, ka, kb_, qo, fl: (b, qa[s], 0))
    k_spec = pl.BlockSpec((1, tq, width), lambda b, s, qa, ka, kb_, qo, fl: (b, ka[s], 0))
    grid_spec = pltpu.PrefetchScalarGridSpec(
        num_scalar_prefetch=len(tables),
        grid=(bsz, n_pairs + 1),
        in_specs=[q_spec, q_spec, k_spec, k_spec,
                  pl.BlockSpec((1, FOX_HEADS, FOX_VT_ROWS, tq),
                               lambda b, s, qa, ka, kb_, qo, fl: (b, 0, 0, kb_[s]))],
        out_specs=pl.BlockSpec((1, tq, width), lambda b, s, qa, ka, kb_, qo, fl: (b, qo[s], 0)),
        scratch_shapes=[pltpu.VMEM((FOX_HEADS, tq, 2 * LANES), BF16),
                        pltpu.VMEM((FOX_HEADS, tq, tq), F32),
                        pltpu.VMEM((FOX_HEADS, 1, tq), F32),
                        pltpu.VMEM((FOX_HEADS, 1, tq), F32),
                        pltpu.VMEM((FOX_HEADS, FOX_VT_ROWS, tq), F32)],
    )
    return pl.pallas_call(
        _fox_kernel, grid_spec=grid_spec,
        out_shape=jax.ShapeDtypeStruct((bsz, seq, width), F32),
        compiler_params=_params(("parallel", "arbitrary")),
        name="forgetting_attention",
    )(*tables, q, qb, k, kb, v_t)


def _fox_bias_lanes(split):
    bsz, _, _, seq = split.shape
    lo = 2 * MLSTM_HEADS
    pieces = jnp.swapaxes(split[:, :, lo:lo + FOX_HEADS, :], 1, 2).astype(BF16)
    ones = jnp.ones_like(pieces)

    def lanes(per_head):
        per_pair = per_head.reshape(bsz, FOX_PAIRS, 2 * FOX_BIAS_LANES, seq)
        per_pair = jnp.pad(per_pair, ((0, 0), (0, 0), (0, LANES - 2 * FOX_BIAS_LANES), (0, 0)))
        return jnp.swapaxes(per_pair.reshape(bsz, FOX_PAIRS * LANES, seq), 1, 2)

    return (lanes(jnp.concatenate([pieces, ones], axis=2)),
            lanes(jnp.concatenate([ones, -pieces], axis=2)))


FFN_COLS = 256


def _merge_ffn_kernel(x_ref, ys_ref, ym_ref, yf_ref, gate_ref, g1_ref, sh_ref, sc_ref, g2_ref,
                      ws_ref, wm_ref, wf_ref, wo_ref, gain_ref, w1_ref, w3_ref, w2_ref, fg_ref,
                      o_ref, *, final_norm):
    d = x_ref.shape[-1]

    def gate(i):
        return _sigmoid(gate_ref[0, :, i * d:(i + 1) * d].astype(F32))

    merged = (gate(0) * _dot(ys_ref[0].astype(BF16), ws_ref[...])
              + gate(1) * _dot(ym_ref[0].astype(BF16), wm_ref[...])
              + gate(2) * _dot(yf_ref[0].astype(BF16), wf_ref[...]))
    x = x_ref[0] + g1_ref[0] * _dot(merged.astype(BF16), wo_ref[...])

    h = _rms_modulate(x, gain_ref[...], sc_ref[0], sh_ref[0]).astype(BF16)
    starts = list(range(0, w1_ref.shape[1], FFN_COLS))

    def up(c0):
        return _dot(h, w1_ref[:, c0:c0 + FFN_COLS]), _dot(h, w3_ref[:, c0:c0 + FFN_COLS])

    acc = jnp.zeros(x.shape, F32)
    nxt = up(starts[0])
    for i, c0 in enumerate(starts):
        a, b = nxt
        if i + 1 < len(starts):
            nxt = up(starts[i + 1])
        acc = acc + _dot((_silu(a) * b).astype(BF16), w2_ref[c0:c0 + FFN_COLS, :])
    y = x + g2_ref[0] * acc
    if final_norm:
        y = (y * lax.rsqrt(jnp.mean(y * y, axis=-1, keepdims=True) + NORM_EPS)) * fg_ref[...]
    o_ref[0] = y


def _merge_ffn(acts, mod_vectors, weights, final_norm, tm):
    bsz, seq, d = acts[0].shape
    tok = lambda b, i: (b, i, 0)
    vec = pl.BlockSpec((1, 1, d), lambda b, i: (b, 0, 0))
    return pl.pallas_call(
        functools.partial(_merge_ffn_kernel, final_norm=final_norm),
        grid=(bsz, seq // tm),
        in_specs=[pl.BlockSpec((1, tm, a.shape[-1]), tok) for a in acts]
        + [vec] * len(mod_vectors) + [_const_spec(w.shape) for w in weights],
        out_specs=pl.BlockSpec((1, tm, d), tok),
        out_shape=jax.ShapeDtypeStruct(acts[0].shape, F32),
        compiler_params=_params(("parallel", "parallel")),
        name="merge_swiglu",
    )(*acts, *mod_vectors, *weights)


def _pad_heads(w, heads, head_dim):
    lead = w.shape[:-1]
    w = w.reshape(lead + (heads, head_dim))
    w = jnp.pad(w, [(0, 0)] * len(lead) + [(0, 0), (0, LANES - head_dim)])
    return w.reshape(lead + (heads * LANES,))


def _layer_weights(w_in, b_in):
    sizes = (S5_WIDTH, MLSTM_WIDTH, MLSTM_WIDTH, MLSTM_WIDTH, MLSTM_HEADS, MLSTM_HEADS,
             FOX_WIDTH, FOX_WIDTH, FOX_WIDTH, FOX_HEADS)
    offs = [0]
    for n in sizes:
        offs.append(offs[-1] + n)
    both = jnp.concatenate([w_in, b_in[None, :]], axis=0)
    part = [both[:, offs[i]:offs[i + 1]] for i in range(len(sizes))] + [both[:, offs[-1]:]]
    pad_m = lambda w: _pad_heads(w, MLSTM_HEADS, MLSTM_HEAD_DIM)
    cat = jnp.concatenate([part[0], pad_m(part[1]), pad_m(part[3]),
                           part[6], part[7], part[10]], axis=1)
    small = jnp.concatenate([part[4], part[5], part[9]], axis=1)
    small = jnp.pad(small, ((0, 0), (0, GATE_ROWS - small.shape[1])))
    cat_t = jnp.concatenate([small, part[8], pad_m(part[2])], axis=1).T
    return (cat[:-1].astype(BF16), cat[-1:], cat_t[:, :-1].astype(BF16), cat_t[:, -1:])


def kernel(x, c, mod_w, mod_b, norm1_g, norm2_g, w_in, b_in, s5_a_re, s5_a_im, s5_log_dt, s5_b_re, s5_b_im, s5_c_re, s5_c_im, s5_d, s5_w_glu, mlstm_conv_w, mlstm_conv_b, mlstm_wq, mlstm_wk, mlstm_norm_g, w_up_s5, w_up_mlstm, w_up_fox, w_out, ffn_w1, ffn_w3, ffn_w2, final_g):
    bsz, seq, d = x.shape
    depth = mod_w.shape[0]
    assert seq % MLSTM_CHUNK == 0
    tm = _row_tile(seq, 512)
    tq = _row_tile(seq, 1024)
    ts = _row_tile(seq, 512)

    mod = _modulation(c, mod_w, mod_b)
    final_gain = final_g.reshape(1, d)
    for l in range(depth):
        sh1, sc1, g1, sh2, sc2, g2 = [mod[l, :, i * d:(i + 1) * d].reshape(bsz, 1, d)
                                      for i in range(6)]
        w_cat, b_cat, w_t, b_t = _layer_weights(w_in[l], b_in[l])
        (s5_u, m_qk, m_o, f_q, f_k, gates, gate_pre, f_vt, m_vt) = _input_projection(
            x, sh1, sc1, norm1_g[l].reshape(1, d), w_cat, b_cat, w_t, b_t, tm)

        gate_rows, gate_split = _gate_scan(gate_pre)
        gate_cols = jnp.swapaxes(gate_rows, 1, 2)
        f_qb, f_kb = _fox_bias_lanes(gate_split)

        bmat, cmat, astep, apow = _s5_tables(s5_a_re[l], s5_a_im[l], s5_log_dt[l], s5_b_re[l],
                                             s5_b_im[l], s5_c_re[l], s5_c_im[l])
        y_s5 = _s5(s5_u, bmat, cmat, astep, apow, s5_d[l].reshape(1, -1),
                   s5_w_glu[l].astype(BF16), ts)

        pad_hh = ((0, 0), (0, LANES - MLSTM_HEAD_DIM), (0, LANES - MLSTM_HEAD_DIM))
        y_m = _mlstm(m_qk, m_vt, m_o, gate_rows, gate_cols,
                     _pad_heads(mlstm_conv_w[l], MLSTM_HEADS, MLSTM_HEAD_DIM),
                     _pad_heads(mlstm_conv_b[l].reshape(1, -1), MLSTM_HEADS, MLSTM_HEAD_DIM),
                     jnp.pad(mlstm_wq[l], pad_hh).astype(BF16),
                     jnp.pad(mlstm_wk[l], pad_hh).astype(BF16),
                     _pad_heads(mlstm_norm_g[l].reshape(1, -1), MLSTM_HEADS, MLSTM_HEAD_DIM))

        y_f = _fox(f_q, f_qb, f_k, f_kb, f_vt, tq)

        w_up_m = _pad_heads(w_up_mlstm[l].T, MLSTM_HEADS, MLSTM_HEAD_DIM).T
        weights = (w_up_s5[l].astype(BF16), w_up_m.astype(BF16), w_up_fox[l].astype(BF16),
                   w_out[l].astype(BF16), norm2_g[l].reshape(1, d), ffn_w1[l].astype(BF16),
                   ffn_w3[l].astype(BF16), ffn_w2[l].astype(BF16), final_gain)
        x = _merge_ffn((x, y_s5, y_m, y_f, gates), (g1, sh2, sc2, g2), weights,
                       l == depth - 1, tm)
    return x
```

```python
import functools
import math

import jax
import jax.numpy as jnp
from jax import lax
from jax.experimental import pallas as pl
from jax.experimental.pallas import tpu as pltpu

F32 = jnp.float32
BF16 = jnp.bfloat16

LANES = 128
SUBLANES = 8
VMEM_LIMIT_BYTES = 56 * 1024 * 1024

NORM_EPS = 1e-6
S5_GROUPS = 16
S5_GROUP_DIM = 16
S5_STATE = 64
S5_WIDTH = S5_GROUPS * S5_GROUP_DIM
S5_CHANNELS = S5_GROUPS * S5_STATE
MLSTM_HEADS = 4
MLSTM_HEAD_DIM = 96
MLSTM_WIDTH = MLSTM_HEADS * MLSTM_HEAD_DIM
MLSTM_PAD_WIDTH = MLSTM_HEADS * LANES
MLSTM_CHUNK = 128
CONV_TAPS = 4
FOX_HEADS = 6
FOX_HEAD_DIM = 64
FOX_WIDTH = FOX_HEADS * FOX_HEAD_DIM
FOX_PAIRS = FOX_WIDTH // LANES
GATE_ROWS = 16
NEG_BIG = -1e30
LOG2_E = math.log2(math.e)

NT_DIMS = (((1,), (1,)), ((), ()))


def _dot(a, b):
    return jnp.dot(a, b, preferred_element_type=F32)


def _dot_nt(a, b):
    return lax.dot_general(a, b, NT_DIMS, preferred_element_type=F32)


def _sigmoid(x):
    return 1.0 / (1.0 + jnp.exp(-x))


def _silu(x):
    return x * _sigmoid(x)


def _log_sigmoid(x):
    return jnp.minimum(x, 0.0) - jnp.log1p(jnp.exp(-jnp.abs(x)))


def _rms_modulate(x, gain, scale, shift):
    y = x * lax.rsqrt(jnp.mean(x * x, axis=-1, keepdims=True) + NORM_EPS)
    return (y * gain) * (1.0 + scale) + shift


def _params(semantics):
    return pltpu.CompilerParams(dimension_semantics=semantics, vmem_limit_bytes=VMEM_LIMIT_BYTES)


def _const_spec(shape):
    nd = len(shape)
    return pl.BlockSpec(shape, lambda *_: (0,) * nd, pipeline_mode=pl.Buffered(1))


def _row_tile(seq, want):
    t = min(want, seq)
    assert seq % t == 0 and t % SUBLANES == 0
    return t


def _mod_kernel(c_ref, w_ref, b_ref, o_ref):
    cf = _silu(c_ref[...]).astype(BF16)
    o_ref[0] = _dot(cf, w_ref[0].astype(BF16)) + b_ref[0]


def _modulation(c, mod_w, mod_b):
    depth, d, six_d = mod_w.shape
    bsz = c.shape[0]
    n_col = six_d // d
    return pl.pallas_call(
        _mod_kernel,
        grid=(depth, n_col),
        in_specs=[pl.BlockSpec((bsz, d), lambda l, j: (0, 0)),
                  pl.BlockSpec((1, d, d), lambda l, j: (l, 0, j)),
                  pl.BlockSpec((1, 1, d), lambda l, j: (l, 0, j))],
        out_specs=pl.BlockSpec((1, bsz, d), lambda l, j: (l, 0, j)),
        out_shape=jax.ShapeDtypeStruct((depth, bsz, six_d), F32),
        compiler_params=_params(("parallel", "parallel")),
        name="adaln_modulation",
    )(c, mod_w, mod_b.reshape(depth, 1, six_d))


IN_SEGMENTS = ((S5_WIDTH, F32), (MLSTM_PAD_WIDTH, F32), (MLSTM_PAD_WIDTH, F32),
               (FOX_WIDTH, F32), (FOX_WIDTH, BF16))
IN_T_SEGMENTS = ((GATE_ROWS, F32), (FOX_WIDTH, BF16), (MLSTM_PAD_WIDTH, F32))
IN_T_FOX_VALUES = 1
FOX_VT_ROWS = FOX_HEAD_DIM + 16
DOT_COLS = 512


def _inproj_kernel(x_ref, sh_ref, sc_ref, g_ref, w_ref, b_ref, wt_ref, bt_ref, *out_refs):
    h = _rms_modulate(x_ref[0], g_ref[...], sc_ref[0], sh_ref[0]).astype(BF16)
    n_t = len(IN_T_SEGMENTS)
    start = 0
    for o_ref in out_refs[:-n_t]:
        width = o_ref.shape[-1]
        for c0 in range(0, width, DOT_COLS):
            cw = min(DOT_COLS, width - c0)
            z = _dot(h, w_ref[:, start + c0:start + c0 + cw]) + b_ref[:, start + c0:start + c0 + cw]
            o_ref[0, :, c0:c0 + cw] = z.astype(o_ref.dtype)
        start += width
    start = 0
    for (rows, _), o_ref in zip(IN_T_SEGMENTS, out_refs[-n_t:]):
        z = _dot_nt(wt_ref[start:start + rows, :], h) + bt_ref[start:start + rows, :]
        if o_ref.ndim == 3:
            o_ref[0] = z.astype(o_ref.dtype)
        else:
            for hd in range(FOX_HEADS):
                o_ref[0, hd, :FOX_HEAD_DIM, :] = z[hd * FOX_HEAD_DIM:(hd + 1) * FOX_HEAD_DIM].astype(
                    o_ref.dtype)
                o_ref[0, hd, FOX_HEAD_DIM:, :] = jnp.ones(
                    (FOX_VT_ROWS - FOX_HEAD_DIM, z.shape[1]), o_ref.dtype)
        start += rows


def _input_projection(x, shift, scale, gain, w_cat, b_cat, w_t, b_t, tm):
    bsz, seq, d = x.shape
    segs = IN_SEGMENTS + ((w_cat.shape[1] - sum(w for w, _ in IN_SEGMENTS), BF16),)
    tok = lambda b, i: (b, i, 0)
    vec = lambda b, i: (b, 0, 0)
    out_shape = [jax.ShapeDtypeStruct((bsz, seq, w), dt) for w, dt in segs]
    out_specs = [pl.BlockSpec((1, tm, w), tok) for w, _ in segs]
    for idx, (r, dt) in enumerate(IN_T_SEGMENTS):
        if idx == IN_T_FOX_VALUES:
            out_shape.append(jax.ShapeDtypeStruct((bsz, FOX_HEADS, FOX_VT_ROWS, seq), dt))
            out_specs.append(pl.BlockSpec((1, FOX_HEADS, FOX_VT_ROWS, tm),
                                          lambda b, i: (b, 0, 0, i)))
        else:
            out_shape.append(jax.ShapeDtypeStruct((bsz, r, seq), dt))
            out_specs.append(pl.BlockSpec((1, r, tm), lambda b, i: (b, 0, i)))
    return pl.pallas_call(
        _inproj_kernel,
        grid=(bsz, seq // tm),
        in_specs=[pl.BlockSpec((1, tm, d), tok),
                  pl.BlockSpec((1, 1, d), vec), pl.BlockSpec((1, 1, d), vec),
                  _const_spec(gain.shape), _const_spec(w_cat.shape), _const_spec(b_cat.shape),
                  _const_spec(w_t.shape), _const_spec(b_t.shape)],
        out_specs=out_specs,
        out_shape=out_shape,
        compiler_params=_params(("parallel", "parallel")),
        name="norm_input_projection",
    )(x, shift, scale, gain, w_cat, b_cat, w_t, b_t)


def _bf16_round(x):
    return x.astype(BF16).astype(F32)


def _gate_scan_kernel(x_ref, o_ref, split_ref):
    x = x_ref[0]
    seq = x.shape[1]
    seg = MLSTM_CHUNK
    lane = lax.broadcasted_iota(jnp.int32, x.shape, 1) & (seg - 1)
    v = _log_sigmoid(x)
    k = 1
    while k < seg:
        v = v + jnp.where(lane >= k, pltpu.roll(v, k, 1), 0.0)
        k *= 2
    row = lax.broadcasted_iota(jnp.int32, (GATE_ROWS, seg), 0)
    carry = jnp.zeros((GATE_ROWS, 1), F32)
    for j in range(seq // seg):
        sl = slice(j * seg, (j + 1) * seg)
        local = v[:, sl]
        total = local + carry
        o_ref[0, :, sl] = jnp.where(row < MLSTM_HEADS, x[:, sl],
                                    jnp.where(row < 2 * MLSTM_HEADS, local, total))
        scaled = total * LOG2_E
        hi = _bf16_round(scaled)
        mid = _bf16_round(scaled - hi)
        split_ref[0, 0, :, sl] = hi
        split_ref[0, 1, :, sl] = mid
        split_ref[0, 2, :, sl] = _bf16_round((scaled - hi) - mid)
        carry = carry + local[:, seg - 1:seg]


def _gate_scan(pre):
    bsz, rows, seq = pre.shape
    spec = pl.BlockSpec((1, rows, seq), lambda b: (b, 0, 0))
    return pl.pallas_call(
        _gate_scan_kernel, grid=(bsz,), in_specs=[spec],
        out_specs=[spec, pl.BlockSpec((1, 3, rows, seq), lambda b: (b, 0, 0, 0))],
        out_shape=[jax.ShapeDtypeStruct(pre.shape, F32),
                   jax.ShapeDtypeStruct((bsz, 3, rows, seq), F32)],
        compiler_params=_params(("parallel",)),
        name="gate_scan",
    )(pre)


SCAN_SHIFTS = (1, 2, 4)


def _s5_kernel(u_ref, bmat_ref, cmat_ref, astep_ref, apow_ref, d_ref, wglu_ref, o_ref,
               st_ref, carry_ref):
    nch = S5_CHANNELS

    @pl.when(pl.program_id(1) == 0)
    def _():
        carry_ref[...] = jnp.zeros_like(carry_ref)

    u = u_ref[0]
    st_ref[...] = _dot(u.astype(BF16), bmat_ref[...])
    n_blocks = u.shape[0] // SUBLANES

    def body(i, carry):
        cr, ci = carry
        r0 = pl.multiple_of(i * SUBLANES, SUBLANES)
        xr = st_ref[pl.ds(r0, SUBLANES), :nch]
        xi = st_ref[pl.ds(r0, SUBLANES), nch:]
        for k, shift in enumerate(SCAN_SHIFTS):
            ar, ai = astep_ref[k, :, :nch], astep_ref[k, :, nch:]
            sr, si = pltpu.roll(xr, shift, 0), pltpu.roll(xi, shift, 0)
            xr, xi = xr + (ar * sr - ai * si), xi + (ar * si + ai * sr)
        pr, pi = apow_ref[:, :nch], apow_ref[:, nch:]
        xr, xi = xr + (pr * cr - pi * ci), xi + (pr * ci + pi * cr)
        st_ref[pl.ds(r0, SUBLANES), :nch] = xr
        st_ref[pl.ds(r0, SUBLANES), nch:] = xi
        last = SUBLANES - 1
        return (jnp.broadcast_to(xr[last:, :], xr.shape), jnp.broadcast_to(xi[last:, :], xi.shape))

    cr, ci = lax.fori_loop(0, n_blocks, body, (carry_ref[:, :nch], carry_ref[:, nch:]))
    carry_ref[:, :nch] = cr
    carry_ref[:, nch:] = ci

    y = _dot(st_ref[...].astype(BF16), cmat_ref[...]) + d_ref[...] * u
    y = y * (0.5 * (1.0 + jnp.tanh(math.sqrt(2.0 / math.pi) * (y + 0.044715 * (y * y * y)))))
    o_ref[0] = y * _sigmoid(_dot(y.astype(BF16), wglu_ref[...]))


def _s5(u, bmat, cmat, astep, apow, d_skip, w_glu, tb):
    bsz, seq, width = u.shape
    tok = lambda b, i: (b, i, 0)
    return pl.pallas_call(
        _s5_kernel,
        grid=(bsz, seq // tb),
        in_specs=[pl.BlockSpec((1, tb, width), tok)] + [
            _const_spec(a.shape) for a in (bmat, cmat, astep, apow, d_skip, w_glu)],
        out_specs=pl.BlockSpec((1, tb, width), tok),
        out_shape=jax.ShapeDtypeStruct(u.shape, F32),
        scratch_shapes=[pltpu.VMEM((tb, 2 * S5_CHANNELS), F32),
                        pltpu.VMEM((SUBLANES, 2 * S5_CHANNELS), F32)],
        compiler_params=_params(("parallel", "arbitrary")),
        name="s5_scan",
    )(u, bmat, cmat, astep, apow, d_skip, w_glu)


def _s5_tables(a_re, a_im, log_dt, b_re, b_im, c_re, c_im):
    g, n, p = b_re.shape
    dt = jnp.exp(log_dt)[:, None]
    mag = jnp.exp(dt * a_re)
    ang = dt * a_im
    abar_r, abar_i = mag * jnp.cos(ang), mag * jnp.sin(ang)
    den = a_re * a_re + a_im * a_im
    pr, qi = abar_r - 1.0, abar_i
    coef_r = (pr * a_re + qi * a_im) / den
    coef_i = (qi * a_re - pr * a_im) / den
    bbar_r = coef_r[..., None] * b_re - coef_i[..., None] * b_im
    bbar_i = coef_r[..., None] * b_im + coef_i[..., None] * b_re
    eye = jnp.eye(g, dtype=F32)

    def in_block(bb):
        return jnp.einsum('gnp,gh->gphn', bb, eye).reshape(g * p, g * n)

    def out_block(cc):
        return jnp.einsum('gpn,gh->gnhp', cc, eye).reshape(g * n, g * p)

    bmat = jnp.concatenate([in_block(bbar_r), in_block(bbar_i)], axis=1).astype(BF16)
    cmat = jnp.concatenate([out_block(c_re), -out_block(c_im)], axis=0).astype(BF16)

    def power(e):
        e = jnp.asarray(e, F32)[:, None, None]
        m = jnp.exp(e * (dt * a_re)[None])
        th = e * ang[None]
        return (m * jnp.cos(th)).reshape(-1, g * n), (m * jnp.sin(th)).reshape(-1, g * n)

    rows = jnp.arange(SUBLANES)
    sr, si = power(SCAN_SHIFTS)
    keep = (rows[None, :] >= jnp.asarray(SCAN_SHIFTS)[:, None]).astype(F32)[:, :, None]
    astep = jnp.concatenate([keep * sr[:, None, :], keep * si[:, None, :]], axis=-1)
    wr, wi = power(rows + 1)
    apow = jnp.concatenate([wr, wi], axis=-1)
    return bmat, cmat, astep, apow


def _mlstm_kernel(qk_ref, vt_ref, o_ref, grow_ref, gcol_ref, cw_ref, cb_ref, wq_ref, wk_ref,
                  ng_ref, out_ref, tail_ref, c_ref, n_ref, m_ref):
    n_seq, chunk = qk_ref.shape[0], qk_ref.shape[1]
    nh = MLSTM_HEADS

    @pl.when(pl.program_id(1) == 0)
    def _():
        tail_ref[...] = jnp.zeros_like(tail_ref)
        c_ref[...] = jnp.zeros_like(c_ref)
        n_ref[...] = jnp.zeros_like(n_ref)
        m_ref[...] = jnp.zeros_like(m_ref)

    cx = []
    for r in range(n_seq):
        x = qk_ref[r]
        ext = jnp.concatenate([tail_ref[r], x], axis=0)
        conv = cb_ref[...] + cw_ref[CONV_TAPS - 1:CONV_TAPS, :] * x
        for tap in range(CONV_TAPS - 1):
            back = CONV_TAPS - 1 - tap
            conv = conv + cw_ref[tap:tap + 1, :] * ext[SUBLANES - back:SUBLANES - back + chunk, :]
        tail_ref[r] = x[chunk - SUBLANES:, :]
        cx.append(_silu(conv))

    key_id = lax.broadcasted_iota(jnp.int32, (chunk, chunk), 0)
    query_id = lax.broadcasted_iota(jnp.int32, (chunk, chunk), 1)
    causal = key_id <= query_id
    k_scale = MLSTM_HEAD_DIM ** -0.5
    cols = [slice(hd * LANES, (hd + 1) * LANES) for hd in range(nh)]
    chains = [(r, hd) for r in range(n_seq) for hd in range(nh)]
    ids = range(len(chains))
    cxh = [cx[r][:, cols[hd]].astype(BF16) for r, hd in chains]
    q = [_dot(cxh[i], wq_ref[chains[i][1]]).astype(BF16) for i in ids]
    k = [(_dot(cxh[i], wk_ref[chains[i][1]]) * k_scale).astype(BF16) for i in ids]
    v_t = [vt_ref[r, cols[hd], :] for r, hd in chains]
    kq = [_dot_nt(k[i], q[i]) for i in ids]
    carry = [_dot_nt(c_ref[r, hd].astype(BF16), q[i]) for i, (r, hd) in enumerate(chains)]
    n_q = [_dot_nt(n_ref[r].astype(BF16), q[i]) for i, (r, hd) in enumerate(chains)]

    s, inter, m_t, stats = [], [], [], []
    for i, (r, hd) in enumerate(chains):
        li_row = grow_ref[r, hd:hd + 1, :]
        b_row = grow_ref[r, nh + hd:nh + hd + 1, :]
        d_col = gcol_ref[r, :, hd:hd + 1] - gcol_ref[r, :, nh + hd:nh + hd + 1]
        m_st = m_ref[r, hd:hd + 1, 0:1]
        dmat = jnp.where(causal, b_row + d_col, -jnp.inf)
        m_inter = b_row + m_st
        m_t.append(jnp.maximum(m_inter, jnp.max(dmat, axis=0, keepdims=True)))
        s.append(kq[i] * jnp.exp(dmat - m_t[i]))
        inter.append(jnp.exp(m_inter - m_t[i]))

        b_last = b_row[:, chunk - 1:chunk]
        w_row = b_last - b_row + li_row
        m_new = jnp.maximum(b_last + m_st, jnp.max(w_row, axis=1, keepdims=True))
        stats.append((m_new, jnp.exp(b_last + m_st - m_new), jnp.exp(w_row - m_new)))

    sv = [_dot(v_t[i].astype(BF16), s[i].astype(BF16)) for i in ids]
    vk = [_dot((v_t[i] * stats[i][2]).astype(BF16), k[i]) for i in ids]
    row8 = lax.broadcasted_iota(jnp.int32, (SUBLANES, chunk), 0)
    ws_rows = []
    for r in range(n_seq):
        rows = jnp.zeros((SUBLANES, chunk), F32)
        for hd in range(nh):
            rows = jnp.where(row8 == hd, stats[r * nh + hd][2], rows)
        ws_rows.append(rows.astype(BF16))
    ws_k = [_dot(ws_rows[r], k[i]) for i, (r, hd) in enumerate(chains)]

    for i, (r, hd) in enumerate(chains):
        m_new, decay, _ = stats[i]
        num = sv[i] + inter[i] * carry[i]
        den = jnp.sum(s[i], axis=0, keepdims=True) + inter[i] * n_q[i][hd:hd + 1, :]
        h = num * (1.0 / jnp.maximum(jnp.abs(den), jnp.exp(-m_t[i])))
        c_ref[r, hd] = decay * c_ref[r, hd] + vk[i]
        n_ref[r, hd:hd + 1, :] = decay * n_ref[r, hd:hd + 1, :] + ws_k[i][hd:hd + 1, :]
        m_ref[r, hd:hd + 1, :] = jnp.broadcast_to(m_new, (1, LANES))
        hn = h * lax.rsqrt(jnp.sum(h * h, axis=0, keepdims=True) * (1.0 / MLSTM_HEAD_DIM) + NORM_EPS)
        out_ref[r, :, cols[hd]] = _sigmoid(o_ref[r, :, cols[hd]]) * (hn.T * ng_ref[:, cols[hd]])


MLSTM_SEQS_PER_STEP = 2


def _mlstm(qk, v_t, o, gate_rows, gate_cols, conv_w, conv_b, wq, wk, norm_g):
    bsz, seq, width = qk.shape
    chunk = MLSTM_CHUNK
    n_seq = MLSTM_SEQS_PER_STEP if bsz % MLSTM_SEQS_PER_STEP == 0 else 1
    tok = lambda b, i: (b, i, 0)
    tok_spec = pl.BlockSpec((n_seq, chunk, width), tok)
    return pl.pallas_call(
        _mlstm_kernel,
        grid=(bsz // n_seq, seq // chunk),
        in_specs=[tok_spec, pl.BlockSpec((n_seq, width, chunk), lambda b, i: (b, 0, i)), tok_spec,
                  pl.BlockSpec((n_seq, GATE_ROWS, chunk), lambda b, i: (b, 0, i)),
                  pl.BlockSpec((n_seq, chunk, GATE_ROWS), tok)] + [
            _const_spec(a.shape) for a in (conv_w, conv_b, wq, wk, norm_g)],
        out_specs=tok_spec,
        out_shape=jax.ShapeDtypeStruct(qk.shape, F32),
        scratch_shapes=[pltpu.VMEM((n_seq, SUBLANES, width), F32),
                        pltpu.VMEM((n_seq, MLSTM_HEADS, LANES, LANES), F32),
                        pltpu.VMEM((n_seq, SUBLANES, LANES), F32),
                        pltpu.VMEM((n_seq, SUBLANES, LANES), F32)],
        compiler_params=_params(("parallel", "arbitrary")),
        name="mlstm_chunkwise",
    )(qk, v_t, o, gate_rows, gate_cols, conv_w, conv_b, wq, wk, norm_g)


FOX_BIAS_LANES = 6
FOX_FLAG_NEW_QUERY, FOX_FLAG_DIAGONAL, FOX_FLAG_FINISH = 1, 2, 4


def _fox_kernel(qa_ref, ka_ref, kb_ref, qo_ref, flag_ref, q_ref, qbias_ref, k_ref, kbias_ref,
                vt_ref, o_ref, qx_ref, s_ref, smax_ref, m_ref, acc_ref):
    step = pl.program_id(1)
    flags = flag_ref[step]
    tq, tk = q_ref.shape[1], k_ref.shape[1]
    hd = FOX_HEAD_DIM

    def pair_lanes(pair):
        return slice(pair * LANES, (pair + 1) * LANES)

    def reset_accumulators():
        m_ref[...] = jnp.full_like(m_ref, NEG_BIG)
        acc_ref[...] = jnp.zeros_like(acc_ref)

    @pl.when(step == 0)
    def _():
        s_ref[...] = jnp.full(s_ref.shape, NEG_BIG, F32)
        smax_ref[...] = jnp.full(smax_ref.shape, NEG_BIG, F32)
        reset_accumulators()

    @pl.when((flags & FOX_FLAG_NEW_QUERY) != 0)
    def _():
        lane = lax.broadcasted_iota(jnp.int32, (1, LANES), 1)
        for pair in range(FOX_PAIRS):
            q = q_ref[0, :, pair_lanes(pair)] * (hd ** -0.5 * LOG2_E)
            qb = qbias_ref[0, :, pair_lanes(pair)]
            for hh in range(2):
                in_head = (lane >= hh * hd) & (lane < (hh + 1) * hd)
                in_bias = (lane >= hh * FOX_BIAS_LANES) & (lane < (hh + 1) * FOX_BIAS_LANES)
                qx_ref[2 * pair + hh, :, :LANES] = jnp.where(in_head, q, 0.0).astype(BF16)
                qx_ref[2 * pair + hh, :, LANES:] = jnp.where(in_bias, qb, jnp.zeros_like(qb))

    def attend(diagonal):
        kx = [jnp.concatenate([k_ref[0, :, pair_lanes(pair)], kbias_ref[0, :, pair_lanes(pair)]],
                              axis=1) for pair in range(FOX_PAIRS)]
        probs = {}

        def score(hh):
            s = _dot_nt(kx[hh // 2], qx_ref[hh])
            if diagonal:
                visible = (lax.broadcasted_iota(jnp.int32, (tk, tq), 0)
                           <= lax.broadcasted_iota(jnp.int32, (tk, tq), 1))
                s = jnp.where(visible, s, NEG_BIG)
            s_ref[hh] = s
            smax_ref[hh] = jnp.max(s, axis=0, keepdims=True)

        def softmax(hh):
            s = s_ref[hh]
            m_prev = m_ref[hh]
            m_new = jnp.maximum(m_prev, smax_ref[hh])
            m_ref[hh] = m_new
            probs[hh] = (jnp.exp2(m_prev - m_new), jnp.exp2(s - m_new).astype(BF16))

        def value(hh):
            alpha, p = probs.pop(hh)
            acc_ref[hh] = alpha * acc_ref[hh] + _dot(vt_ref[0, hh], p)

        softmax(0)
        for hh in range(FOX_HEADS):
            score(hh)
            if hh + 1 < FOX_HEADS:
                softmax(hh + 1)
            value(hh)

    @pl.when((flags & FOX_FLAG_DIAGONAL) == 0)
    def _():
        attend(False)

    @pl.when((flags & FOX_FLAG_DIAGONAL) != 0)
    def _():
        attend(True)

    @pl.when((flags & FOX_FLAG_FINISH) != 0)
    def _():
        for pair in range(FOX_PAIRS):
            out_t = jnp.concatenate([acc_ref[hh, :hd, :] / acc_ref[hh, hd:hd + 1, :]
                                     for hh in (2 * pair, 2 * pair + 1)], axis=0)
            o_ref[0, :, pair_lanes(pair)] = out_t.T
        reset_accumulators()

    @pl.when(step == 0)
    def _():
        reset_accumulators()


def _fox(q, qb, k, kb, v_t, tq):
    bsz, seq, width = q.shape
    nq = seq // tq
    q_idx = [i for i in range(nq) for _ in range(i + 1)]
    k_idx = [j for i in range(nq) for j in range(i + 1)]
    n_pairs = len(q_idx)
    q_new = q_idx + [q_idx[-1]]
    k_new = k_idx + [k_idx[-1]]
    k_old = [0] + k_idx
    q_old = [0] + q_idx
    flags = []
    for n in range(n_pairs + 1):
        f = 0
        if n < n_pairs and k_idx[n] == 0:
            f |= FOX_FLAG_NEW_QUERY
        if n < n_pairs and k_idx[n] == q_idx[n]:
            f |= FOX_FLAG_DIAGONAL
        if n >= 1 and (n == n_pairs or k_idx[n] == 0):
            f |= FOX_FLAG_FINISH
        flags.append(f)
    tables = [jnp.asarray(t, jnp.int32) for t in (q_new, k_new, k_old, q_old, flags)]
    q_spec = pl.BlockSpec((1, tq, width), lambda b, s, qa, ka, kb_, qo, fl: (b, qa[s], 0))
    k_spec = pl.BlockSpec((1, tq, width), lambda b, s, qa, ka, kb_, qo, fl: (b, ka[s], 0))
    grid_spec = pltpu.PrefetchScalarGridSpec(
        num_scalar_prefetch=len(tables),
        grid=(bsz, n_pairs + 1),
        in_specs=[q_spec, q_spec, k_spec, k_spec,
                  pl.BlockSpec((1, FOX_HEADS, FOX_VT_ROWS, tq),
                               lambda b, s, qa, ka, kb_, qo, fl: (b, 0, 0, kb_[s]))],
        out_specs=pl.BlockSpec((1, tq, width), lambda b, s, qa, ka, kb_, qo, fl: (b, qo[s], 0)),
        scratch_shapes=[pltpu.VMEM((FOX_HEADS, tq, 2 * LANES), BF16),
                        pltpu.VMEM((FOX_HEADS, tq, tq), F32),
                        pltpu.VMEM((FOX_HEADS, 1, tq), F32),
                        pltpu.VMEM((FOX_HEADS, 1, tq), F32),
                        pltpu.VMEM((FOX_HEADS, FOX_VT_ROWS, tq), F32)],
    )
    return pl.pallas_call(
        _fox_kernel, grid_spec=grid_spec,
        out_shape=jax.ShapeDtypeStruct((bsz, seq, width), F32),
        compiler_params=_params(("parallel", "arbitrary")),
        name="forgetting_attention",
    )(*tables, q, qb, k, kb, v_t)


def _fox_bias_lanes(split):
    bsz, _, _, seq = split.shape
    lo = 2 * MLSTM_HEADS
    pieces = jnp.swapaxes(split[:, :, lo:lo + FOX_HEADS, :], 1, 2).astype(BF16)
    ones = jnp.ones_like(pieces)

    def lanes(per_head):
        per_pair = per_head.reshape(bsz, FOX_PAIRS, 2 * FOX_BIAS_LANES, seq)
        per_pair = jnp.pad(per_pair, ((0, 0), (0, 0), (0, LANES - 2 * FOX_BIAS_LANES), (0, 0)))
        return jnp.swapaxes(per_pair.reshape(bsz, FOX_PAIRS * LANES, seq), 1, 2)

    return (lanes(jnp.concatenate([pieces, ones], axis=2)),
            lanes(jnp.concatenate([ones, -pieces], axis=2)))


FFN_COLS = 256


def _merge_ffn_kernel(x_ref, ys_ref, ym_ref, yf_ref, gate_ref, g1_ref, sh_ref, sc_ref, g2_ref,
                      ws_ref, wm_ref, wf_ref, wo_ref, gain_ref, w1_ref, w3_ref, w2_ref, fg_ref,
                      o_ref, *, final_norm):
    d = x_ref.shape[-1]

    def gate(i):
        return _sigmoid(gate_ref[0, :, i * d:(i + 1) * d].astype(F32))

    merged = (gate(0) * _dot(ys_ref[0].astype(BF16), ws_ref[...])
              + gate(1) * _dot(ym_ref[0].astype(BF16), wm_ref[...])
              + gate(2) * _dot(yf_ref[0].astype(BF16), wf_ref[...]))
    x = x_ref[0] + g1_ref[0] * _dot(merged.astype(BF16), wo_ref[...])

    h = _rms_modulate(x, gain_ref[...], sc_ref[0], sh_ref[0]).astype(BF16)
    starts = list(range(0, w1_ref.shape[1], FFN_COLS))

    def up(c0):
        return _dot(h, w1_ref[:, c0:c0 + FFN_COLS]), _dot(h, w3_ref[:, c0:c0 + FFN_COLS])

    acc = jnp.zeros(x.shape, F32)
    nxt = up(starts[0])
    for i, c0 in enumerate(starts):
        a, b = nxt
        if i + 1 < len(starts):
            nxt = up(starts[i + 1])
        acc = acc + _dot((_silu(a) * b).astype(BF16), w2_ref[c0:c0 + FFN_COLS, :])
    y = x + g2_ref[0] * acc
    if final_norm:
        y = (y * lax.rsqrt(jnp.mean(y * y, axis=-1, keepdims=True) + NORM_EPS)) * fg_ref[...]
    o_ref[0] = y


def _merge_ffn(acts, mod_vectors, weights, final_norm, tm):
    bsz, seq, d = acts[0].shape
    tok = lambda b, i: (b, i, 0)
    vec = pl.BlockSpec((1, 1, d), lambda b, i: (b, 0, 0))
    return pl.pallas_call(
        functools.partial(_merge_ffn_kernel, final_norm=final_norm),
        grid=(bsz, seq // tm),
        in_specs=[pl.BlockSpec((1, tm, a.shape[-1]), tok) for a in acts]
        + [vec] * len(mod_vectors) + [_const_spec(w.shape) for w in weights],
        out_specs=pl.BlockSpec((1, tm, d), tok),
        out_shape=jax.ShapeDtypeStruct(acts[0].shape, F32),
        compiler_params=_params(("parallel", "parallel")),
        name="merge_swiglu",
    )(*acts, *mod_vectors, *weights)


def _pad_heads(w, heads, head_dim):
    lead = w.shape[:-1]
    w = w.reshape(lead + (heads, head_dim))
    w = jnp.pad(w, [(0, 0)] * len(lead) + [(0, 0), (0, LANES - head_dim)])
    return w.reshape(lead + (heads * LANES,))


def _layer_weights(w_in, b_in):
    sizes = (S5_WIDTH, MLSTM_WIDTH, MLSTM_WIDTH, MLSTM_WIDTH, MLSTM_HEADS, MLSTM_HEADS,
             FOX_WIDTH, FOX_WIDTH, FOX_WIDTH, FOX_HEADS)
    offs = [0]
    for n in sizes:
        offs.append(offs[-1] + n)
    both = jnp.concatenate([w_in, b_in[None, :]], axis=0)
    part = [both[:, offs[i]:offs[i + 1]] for i in range(len(sizes))] + [both[:, offs[-1]:]]
    pad_m = lambda w: _pad_heads(w, MLSTM_HEADS, MLSTM_HEAD_DIM)
    cat = jnp.concatenate([part[0], pad_m(part[1]), pad_m(part[3]),
                           part[6], part[7], part[10]], axis=1)
    small = jnp.concatenate([part[4], part[5], part[9]], axis=1)
    small = jnp.pad(small, ((0, 0), (0, GATE_ROWS - small.shape[1])))
    cat_t = jnp.concatenate([small, part[8], pad_m(part[2])], axis=1).T
    return (cat[:-1].astype(BF16), cat[-1:], cat_t[:, :-1].astype(BF16), cat_t[:, -1:])


def kernel(x, c, mod_w, mod_b, norm1_g, norm2_g, w_in, b_in, s5_a_re, s5_a_im, s5_log_dt, s5_b_re, s5_b_im, s5_c_re, s5_c_im, s5_d, s5_w_glu, mlstm_conv_w, mlstm_conv_b, mlstm_wq, mlstm_wk, mlstm_norm_g, w_up_s5, w_up_mlstm, w_up_fox, w_out, ffn_w1, ffn_w3, ffn_w2, final_g):
    bsz, seq, d = x.shape
    depth = mod_w.shape[0]
    assert seq % MLSTM_CHUNK == 0
    tm = _row_tile(seq, 512)
    tq = _row_tile(seq, 1024)
    ts = _row_tile(seq, 512)

    mod = _modulation(c, mod_w, mod_b)
    final_gain = final_g.reshape(1, d)
    for l in range(depth):
        sh1, sc1, g1, sh2, sc2, g2 = [mod[l, :, i * d:(i + 1) * d].reshape(bsz, 1, d)
                                      for i in range(6)]
        w_cat, b_cat, w_t, b_t = _layer_weights(w_in[l], b_in[l])
        (s5_u, m_qk, m_o, f_q, f_k, gates, gate_pre, f_vt, m_vt) = _input_projection(
            x, sh1, sc1, norm1_g[l].reshape(1, d), w_cat, b_cat, w_t, b_t, tm)

        gate_rows, gate_split = _gate_scan(gate_pre)
        gate_cols = jnp.swapaxes(gate_rows, 1, 2)
        f_qb, f_kb = _fox_bias_lanes(gate_split)

        bmat, cmat, astep, apow = _s5_tables(s5_a_re[l], s5_a_im[l], s5_log_dt[l], s5_b_re[l],
                                             s5_b_im[l], s5_c_re[l], s5_c_im[l])
        y_s5 = _s5(s5_u, bmat, cmat, astep, apow, s5_d[l].reshape(1, -1),
                   s5_w_glu[l].astype(BF16), ts)

        pad_hh = ((0, 0), (0, LANES - MLSTM_HEAD_DIM), (0, LANES - MLSTM_HEAD_DIM))
        y_m = _mlstm(m_qk, m_vt, m_o, gate_rows, gate_cols,
                     _pad_heads(mlstm_conv_w[l], MLSTM_HEADS, MLSTM_HEAD_DIM),
                     _pad_heads(mlstm_conv_b[l].reshape(1, -1), MLSTM_HEADS, MLSTM_HEAD_DIM),
                     jnp.pad(mlstm_wq[l], pad_hh).astype(BF16),
                     jnp.pad(mlstm_wk[l], pad_hh).astype(BF16),
                     _pad_heads(mlstm_norm_g[l].reshape(1, -1), MLSTM_HEADS, MLSTM_HEAD_DIM))

        y_f = _fox(f_q, f_qb, f_k, f_kb, f_vt, tq)

        w_up_m = _pad_heads(w_up_mlstm[l].T, MLSTM_HEADS, MLSTM_HEAD_DIM).T
        weights = (w_up_s5[l].astype(BF16), w_up_m.astype(BF16), w_up_fox[l].astype(BF16),
                   w_out[l].astype(BF16), norm2_g[l].reshape(1, d), ffn_w1[l].astype(BF16),
                   ffn_w3[l].astype(BF16), ffn_w2[l].astype(BF16), final_gain)
        x = _merge_ffn((x, y_s5, y_m, y_f, gates), (g1, sh2, sc2, g2), weights,
                       l == depth - 1, tm)
    return x
```

```python
import functools
import math

import jax
import jax.numpy as jnp
from jax import lax
from jax.experimental import pallas as pl
from jax.experimental.pallas import tpu as pltpu

F32 = jnp.float32
BF16 = jnp.bfloat16

LANES = 128
SUBLANES = 8
VMEM_LIMIT_BYTES = 56 * 1024 * 1024

NORM_EPS = 1e-6
S5_GROUPS = 16
S5_GROUP_DIM = 16
S5_STATE = 64
S5_WIDTH = S5_GROUPS * S5_GROUP_DIM
S5_CHANNELS = S5_GROUPS * S5_STATE
MLSTM_HEADS = 4
MLSTM_HEAD_DIM = 96
MLSTM_WIDTH = MLSTM_HEADS * MLSTM_HEAD_DIM
MLSTM_PAD_WIDTH = MLSTM_HEADS * LANES
MLSTM_CHUNK = 128
CONV_TAPS = 4
FOX_HEADS = 6
FOX_HEAD_DIM = 64
FOX_WIDTH = FOX_HEADS * FOX_HEAD_DIM
FOX_PAIRS = FOX_WIDTH // LANES
GATE_ROWS = 16
NEG_BIG = -1e30
LOG2_E = math.log2(math.e)

NT_DIMS = (((1,), (1,)), ((), ()))


def _dot(a, b):
    return jnp.dot(a, b, preferred_element_type=F32)


def _dot_nt(a, b):
    return lax.dot_general(a, b, NT_DIMS, preferred_element_type=F32)


def _sigmoid(x):
    return 1.0 / (1.0 + jnp.exp(-x))


def _silu(x):
    return x * _sigmoid(x)


def _log_sigmoid(x):
    return jnp.minimum(x, 0.0) - jnp.log1p(jnp.exp(-jnp.abs(x)))


def _rms_modulate(x, gain, scale, shift):
    y = x * lax.rsqrt(jnp.mean(x * x, axis=-1, keepdims=True) + NORM_EPS)
    return (y * gain) * (1.0 + scale) + shift


def _params(semantics):
    return pltpu.CompilerParams(dimension_semantics=semantics, vmem_limit_bytes=VMEM_LIMIT_BYTES)


def _const_spec(shape):
    nd = len(shape)
    return pl.BlockSpec(shape, lambda *_: (0,) * nd, pipeline_mode=pl.Buffered(1))


def _row_tile(seq, want):
    t = min(want, seq)
    assert seq % t == 0 and t % SUBLANES == 0
    return t


def _mod_kernel(c_ref, w_ref, b_ref, o_ref):
    cf = _silu(c_ref[...]).astype(BF16)
    o_ref[0] = _dot(cf, w_ref[0].astype(BF16)) + b_ref[0]


def _modulation(c, mod_w, mod_b):
    depth, d, six_d = mod_w.shape
    bsz = c.shape[0]
    n_col = six_d // d
    return pl.pallas_call(
        _mod_kernel,
        grid=(depth, n_col),
        in_specs=[pl.BlockSpec((bsz, d), lambda l, j: (0, 0)),
                  pl.BlockSpec((1, d, d), lambda l, j: (l, 0, j)),
                  pl.BlockSpec((1, 1, d), lambda l, j: (l, 0, j))],
        out_specs=pl.BlockSpec((1, bsz, d), lambda l, j: (l, 0, j)),
        out_shape=jax.ShapeDtypeStruct((depth, bsz, six_d), F32),
        compiler_params=_params(("parallel", "parallel")),
        name="adaln_modulation",
    )(c, mod_w, mod_b.reshape(depth, 1, six_d))


IN_SEGMENTS = ((S5_WIDTH, F32), (MLSTM_PAD_WIDTH, F32), (MLSTM_PAD_WIDTH, F32),
               (FOX_WIDTH, F32), (FOX_WIDTH, BF16))
IN_T_SEGMENTS = ((GATE_ROWS, F32), (FOX_WIDTH, BF16), (MLSTM_PAD_WIDTH, F32))
IN_T_FOX_VALUES = 1
FOX_VT_ROWS = FOX_HEAD_DIM + 16
DOT_COLS = 512


def _inproj_kernel(x_ref, sh_ref, sc_ref, g_ref, w_ref, b_ref, wt_ref, bt_ref, *out_refs):
    h = _rms_modulate(x_ref[0], g_ref[...], sc_ref[0], sh_ref[0]).astype(BF16)
    n_t = len(IN_T_SEGMENTS)
    start = 0
    for o_ref in out_refs[:-n_t]:
        width = o_ref.shape[-1]
        for c0 in range(0, width, DOT_COLS):
            cw = min(DOT_COLS, width - c0)
            z = _dot(h, w_ref[:, start + c0:start + c0 + cw]) + b_ref[:, start + c0:start + c0 + cw]
            o_ref[0, :, c0:c0 + cw] = z.astype(o_ref.dtype)
        start += width
    start = 0
    for (rows, _), o_ref in zip(IN_T_SEGMENTS, out_refs[-n_t:]):
        z = _dot_nt(wt_ref[start:start + rows, :], h) + bt_ref[start:start + rows, :]
        if o_ref.ndim == 3:
            o_ref[0] = z.astype(o_ref.dtype)
        else:
            for hd in range(FOX_HEADS):
                o_ref[0, hd, :FOX_HEAD_DIM, :] = z[hd * FOX_HEAD_DIM:(hd + 1) * FOX_HEAD_DIM].astype(
                    o_ref.dtype)
                o_ref[0, hd, FOX_HEAD_DIM:, :] = jnp.ones(
                    (FOX_VT_ROWS - FOX_HEAD_DIM, z.shape[1]), o_ref.dtype)
        start += rows


def _input_projection(x, shift, scale, gain, w_cat, b_cat, w_t, b_t, tm):
    bsz, seq, d = x.shape
    segs = IN_SEGMENTS + ((w_cat.shape[1] - sum(w for w, _ in IN_SEGMENTS), BF16),)
    tok = lambda b, i: (b, i, 0)
    vec = lambda b, i: (b, 0, 0)
    out_shape = [jax.ShapeDtypeStruct((bsz, seq, w), dt) for w, dt in segs]
    out_specs = [pl.BlockSpec((1, tm, w), tok) for w, _ in segs]
    for idx, (r, dt) in enumerate(IN_T_SEGMENTS):
        if idx == IN_T_FOX_VALUES:
            out_shape.append(jax.ShapeDtypeStruct((bsz, FOX_HEADS, FOX_VT_ROWS, seq), dt))
            out_specs.append(pl.BlockSpec((1, FOX_HEADS, FOX_VT_ROWS, tm),
                                          lambda b, i: (b, 0, 0, i)))
        else:
            out_shape.append(jax.ShapeDtypeStruct((bsz, r, seq), dt))
            out_specs.append(pl.BlockSpec((1, r, tm), lambda b, i: (b, 0, i)))
    return pl.pallas_call(
        _inproj_kernel,
        grid=(bsz, seq // tm),
        in_specs=[pl.BlockSpec((1, tm, d), tok),
                  pl.BlockSpec((1, 1, d), vec), pl.BlockSpec((1, 1, d), vec),
                  _const_spec(gain.shape), _const_spec(w_cat.shape), _const_spec(b_cat.shape),
                  _const_spec(w_t.shape), _const_spec(b_t.shape)],
        out_specs=out_specs,
        out_shape=out_shape,
        compiler_params=_params(("parallel", "parallel")),
        name="norm_input_projection",
    )(x, shift, scale, gain, w_cat, b_cat, w_t, b_t)


def _bf16_round(x):
    return x.astype(BF16).astype(F32)


def _gate_scan_kernel(x_ref, o_ref, split_ref):
    x = x_ref[0]
    seq = x.shape[1]
    seg = MLSTM_CHUNK
    lane = lax.broadcasted_iota(jnp.int32, x.shape, 1) & (seg - 1)
    v = _log_sigmoid(x)
    k = 1
    while k < seg:
        v = v + jnp.where(lane >= k, pltpu.roll(v, k, 1), 0.0)
        k *= 2
    row = lax.broadcasted_iota(jnp.int32, (GATE_ROWS, seg), 0)
    carry = jnp.zeros((GATE_ROWS, 1), F32)
    for j in range(seq // seg):
        sl = slice(j * seg, (j + 1) * seg)
        local = v[:, sl]
        total = local + carry
        o_ref[0, :, sl] = jnp.where(row < MLSTM_HEADS, x[:, sl],
                                    jnp.where(row < 2 * MLSTM_HEADS, local, total))
        scaled = total * LOG2_E
        hi = _bf16_round(scaled)
        mid = _bf16_round(scaled - hi)
        split_ref[0, 0, :, sl] = hi
        split_ref[0, 1, :, sl] = mid
        split_ref[0, 2, :, sl] = _bf16_round((scaled - hi) - mid)
        carry = carry + local[:, seg - 1:seg]


def _gate_scan(pre):
    bsz, rows, seq = pre.shape
    spec = pl.BlockSpec((1, rows, seq), lambda b: (b, 0, 0))
    return pl.pallas_call(
        _gate_scan_kernel, grid=(bsz,), in_specs=[spec],
        out_specs=[spec, pl.BlockSpec((1, 3, rows, seq), lambda b: (b, 0, 0, 0))],
        out_shape=[jax.ShapeDtypeStruct(pre.shape, F32),
                   jax.ShapeDtypeStruct((bsz, 3, rows, seq), F32)],
        compiler_params=_params(("parallel",)),
        name="gate_scan",
    )(pre)


SCAN_SHIFTS = (1, 2, 4)


def _s5_kernel(u_ref, bmat_ref, cmat_ref, astep_ref, apow_ref, d_ref, wglu_ref, o_ref,
               st_ref, carry_ref):
    nch = S5_CHANNELS

    @pl.when(pl.program_id(1) == 0)
    def _():
        carry_ref[...] = jnp.zeros_like(carry_ref)

    u = u_ref[0]
    st_ref[...] = _dot(u.astype(BF16), bmat_ref[...])
    n_blocks = u.shape[0] // SUBLANES

    def body(i, carry):
        cr, ci = carry
        r0 = pl.multiple_of(i * SUBLANES, SUBLANES)
        xr = st_ref[pl.ds(r0, SUBLANES), :nch]
        xi = st_ref[pl.ds(r0, SUBLANES), nch:]
        for k, shift in enumerate(SCAN_SHIFTS):
            ar, ai = astep_ref[k, :, :nch], astep_ref[k, :, nch:]
            sr, si = pltpu.roll(xr, shift, 0), pltpu.roll(xi, shift, 0)
            xr, xi = xr + (ar * sr - ai * si), xi + (ar * si + ai * sr)
        pr, pi = apow_ref[:, :nch], apow_ref[:, nch:]
        xr, xi = xr + (pr * cr - pi * ci), xi + (pr * ci + pi * cr)
        st_ref[pl.ds(r0, SUBLANES), :nch] = xr
        st_ref[pl.ds(r0, SUBLANES), nch:] = xi
        last = SUBLANES - 1
        return (jnp.broadcast_to(xr[last:, :], xr.shape), jnp.broadcast_to(xi[last:, :], xi.shape))

    cr, ci = lax.fori_loop(0, n_blocks, body, (carry_ref[:, :nch], carry_ref[:, nch:]))
    carry_ref[:, :nch] = cr
    carry_ref[:, nch:] = ci

    y = _dot(st_ref[...].astype(BF16), cmat_ref[...]) + d_ref[...] * u
    y = y * (0.5 * (1.0 + jnp.tanh(math.sqrt(2.0 / math.pi) * (y + 0.044715 * (y * y * y)))))
    o_ref[0] = y * _sigmoid(_dot(y.astype(BF16), wglu_ref[...]))


def _s5(u, bmat, cmat, astep, apow, d_skip, w_glu, tb):
    bsz, seq, width = u.shape
    tok = lambda b, i: (b, i, 0)
    return pl.pallas_call(
        _s5_kernel,
        grid=(bsz, seq // tb),
        in_specs=[pl.BlockSpec((1, tb, width), tok)] + [
            _const_spec(a.shape) for a in (bmat, cmat, astep, apow, d_skip, w_glu)],
        out_specs=pl.BlockSpec((1, tb, width), tok),
        out_shape=jax.ShapeDtypeStruct(u.shape, F32),
        scratch_shapes=[pltpu.VMEM((tb, 2 * S5_CHANNELS), F32),
                        pltpu.VMEM((SUBLANES, 2 * S5_CHANNELS), F32)],
        compiler_params=_params(("parallel", "arbitrary")),
        name="s5_scan",
    )(u, bmat, cmat, astep, apow, d_skip, w_glu)


def _s5_tables(a_re, a_im, log_dt, b_re, b_im, c_re, c_im):
    g, n, p = b_re.shape
    dt = jnp.exp(log_dt)[:, None]
    mag = jnp.exp(dt * a_re)
    ang = dt * a_im
    abar_r, abar_i = mag * jnp.cos(ang), mag * jnp.sin(ang)
    den = a_re * a_re + a_im * a_im
    pr, qi = abar_r - 1.0, abar_i
    coef_r = (pr * a_re + qi * a_im) / den
    coef_i = (qi * a_re - pr * a_im) / den
    bbar_r = coef_r[..., None] * b_re - coef_i[..., None] * b_im
    bbar_i = coef_r[..., None] * b_im + coef_i[..., None] * b_re
    eye = jnp.eye(g, dtype=F32)

    def in_block(bb):
        return jnp.einsum('gnp,gh->gphn', bb, eye).reshape(g * p, g * n)

    def out_block(cc):
        return jnp.einsum('gpn,gh->gnhp', cc, eye).reshape(g * n, g * p)

    bmat = jnp.concatenate([in_block(bbar_r), in_block(bbar_i)], axis=1).astype(BF16)
    cmat = jnp.concatenate([out_block(c_re), -out_block(c_im)], axis=0).astype(BF16)

    def power(e):
        e = jnp.asarray(e, F32)[:, None, None]
        m = jnp.exp(e * (dt * a_re)[None])
        th = e * ang[None]
        return (m * jnp.cos(th)).reshape(-1, g * n), (m * jnp.sin(th)).reshape(-1, g * n)

    rows = jnp.arange(SUBLANES)
    sr, si = power(SCAN_SHIFTS)
    keep = (rows[None, :] >= jnp.asarray(SCAN_SHIFTS)[:, None]).astype(F32)[:, :, None]
    astep = jnp.concatenate([keep * sr[:, None, :], keep * si[:, None, :]], axis=-1)
    wr, wi = power(rows + 1)
    apow = jnp.concatenate([wr, wi], axis=-1)
    return bmat, cmat, astep, apow


def _mlstm_kernel(qk_ref, vt_ref, o_ref, grow_ref, gcol_ref, cw_ref, cb_ref, wq_ref, wk_ref,
                  ng_ref, out_ref, tail_ref, c_ref, n_ref, m_ref):
    n_seq, chunk = qk_ref.shape[0], qk_ref.shape[1]
    nh = MLSTM_HEADS

    @pl.when(pl.program_id(1) == 0)
    def _():
        tail_ref[...] = jnp.zeros_like(tail_ref)
        c_ref[...] = jnp.zeros_like(c_ref)
        n_ref[...] = jnp.zeros_like(n_ref)
        m_ref[...] = jnp.zeros_like(m_ref)

    cx = []
    for r in range(n_seq):
        x = qk_ref[r]
        ext = jnp.concatenate([tail_ref[r], x], axis=0)
        conv = cb_ref[...] + cw_ref[CONV_TAPS - 1:CONV_TAPS, :] * x
        for tap in range(CONV_TAPS - 1):
            back = CONV_TAPS - 1 - tap
            conv = conv + cw_ref[tap:tap + 1, :] * ext[SUBLANES - back:SUBLANES - back + chunk, :]
        tail_ref[r] = x[chunk - SUBLANES:, :]
        cx.append(_silu(conv))

    key_id = lax.broadcasted_iota(jnp.int32, (chunk, chunk), 0)
    query_id = lax.broadcasted_iota(jnp.int32, (chunk, chunk), 1)
    causal = key_id <= query_id
    k_scale = MLSTM_HEAD_DIM ** -0.5
    cols = [slice(hd * LANES, (hd + 1) * LANES) for hd in range(nh)]
    chains = [(r, hd) for r in range(n_seq) for hd in range(nh)]
    ids = range(len(chains))
    cxh = [cx[r][:, cols[hd]].astype(BF16) for r, hd in chains]
    q = [_dot(cxh[i], wq_ref[chains[i][1]]).astype(BF16) for i in ids]
    k = [(_dot(cxh[i], wk_ref[chains[i][1]]) * k_scale).astype(BF16) for i in ids]
    v_t = [vt_ref[r, cols[hd], :] for r, hd in chains]
    kq = [_dot_nt(k[i], q[i]) for i in ids]
    carry = [_dot_nt(c_ref[r, hd].astype(BF16), q[i]) for i, (r, hd) in enumerate(chains)]
    n_q = [_dot_nt(n_ref[r].astype(BF16), q[i]) for i, (r, hd) in enumerate(chains)]

    s, inter, m_t, stats = [], [], [], []
    for i, (r, hd) in enumerate(chains):
        li_row = grow_ref[r, hd:hd + 1, :]
        b_row = grow_ref[r, nh + hd:nh + hd + 1, :]
        d_col = gcol_ref[r, :, hd:hd + 1] - gcol_ref[r, :, nh + hd:nh + hd + 1]
        m_st = m_ref[r, hd:hd + 1, 0:1]
        dmat = jnp.where(causal, b_row + d_col, -jnp.inf)
        m_inter = b_row + m_st
        m_t.append(jnp.maximum(m_inter, jnp.max(dmat, axis=0, keepdims=True)))
        s.append(kq[i] * jnp.exp(dmat - m_t[i]))
        inter.append(jnp.exp(m_inter - m_t[i]))

        b_last = b_row[:, chunk - 1:chunk]
        w_row = b_last - b_row + li_row
        m_new = jnp.maximum(b_last + m_st, jnp.max(w_row, axis=1, keepdims=True))
        stats.append((m_new, jnp.exp(b_last + m_st - m_new), jnp.exp(w_row - m_new)))

    sv = [_dot(v_t[i].astype(BF16), s[i].astype(BF16)) for i in ids]
    vk = [_dot((v_t[i] * stats[i][2]).astype(BF16), k[i]) for i in ids]
    row8 = lax.broadcasted_iota(jnp.int32, (SUBLANES, chunk), 0)
    ws_rows = []
    for r in range(n_seq):
        rows = jnp.zeros((SUBLANES, chunk), F32)
        for hd in range(nh):
            rows = jnp.where(row8 == hd, stats[r * nh + hd][2], rows)
        ws_rows.append(rows.astype(BF16))
    ws_k = [_dot(ws_rows[r], k[i]) for i, (r, hd) in enumerate(chains)]

    for i, (r, hd) in enumerate(chains):
        m_new, decay, _ = stats[i]
        num = sv[i] + inter[i] * carry[i]
        den = jnp.sum(s[i], axis=0, keepdims=True) + inter[i] * n_q[i][hd:hd + 1, :]
        h = num * (1.0 / jnp.maximum(jnp.abs(den), jnp.exp(-m_t[i])))
        c_ref[r, hd] = decay * c_ref[r, hd] + vk[i]
        n_ref[r, hd:hd + 1, :] = decay * n_ref[r, hd:hd + 1, :] + ws_k[i][hd:hd + 1, :]
        m_ref[r, hd:hd + 1, :] = jnp.broadcast_to(m_new, (1, LANES))
        hn = h * lax.rsqrt(jnp.sum(h * h, axis=0, keepdims=True) * (1.0 / MLSTM_HEAD_DIM) + NORM_EPS)
        out_ref[r, :, cols[hd]] = _sigmoid(o_ref[r, :, cols[hd]]) * (hn.T * ng_ref[:, cols[hd]])


MLSTM_SEQS_PER_STEP = 2


def _mlstm(qk, v_t, o, gate_rows, gate_cols, conv_w, conv_b, wq, wk, norm_g):
    bsz, seq, width = qk.shape
    chunk = MLSTM_CHUNK
    n_seq = MLSTM_SEQS_PER_STEP if bsz % MLSTM_SEQS_PER_STEP == 0 else 1
    tok = lambda b, i: (b, i, 0)
    tok_spec = pl.BlockSpec((n_seq, chunk, width), tok)
    return pl.pallas_call(
        _mlstm_kernel,
        grid=(bsz // n_seq, seq // chunk),
        in_specs=[tok_spec, pl.BlockSpec((n_seq, width, chunk), lambda b, i: (b, 0, i)), tok_spec,
                  pl.BlockSpec((n_seq, GATE_ROWS, chunk), lambda b, i: (b, 0, i)),
                  pl.BlockSpec((n_seq, chunk, GATE_ROWS), tok)] + [
            _const_spec(a.shape) for a in (conv_w, conv_b, wq, wk, norm_g)],
        out_specs=tok_spec,
        out_shape=jax.ShapeDtypeStruct(qk.shape, F32),
        scratch_shapes=[pltpu.VMEM((n_seq, SUBLANES, width), F32),
                        pltpu.VMEM((n_seq, MLSTM_HEADS, LANES, LANES), F32),
                        pltpu.VMEM((n_seq, SUBLANES, LANES), F32),
                        pltpu.VMEM((n_seq, SUBLANES, LANES), F32)],
        compiler_params=_params(("parallel", "arbitrary")),
        name="mlstm_chunkwise",
    )(qk, v_t, o, gate_rows, gate_cols, conv_w, conv_b, wq, wk, norm_g)


FOX_BIAS_LANES = 6
FOX_FLAG_NEW_QUERY, FOX_FLAG_DIAGONAL, FOX_FLAG_FINISH = 1, 2, 4


def _fox_kernel(qa_ref, ka_ref, kb_ref, qo_ref, flag_ref, q_ref, qbias_ref, k_ref, kbias_ref,
                vt_ref, o_ref, qx_ref, s_ref, smax_ref, m_ref, acc_ref):
    step = pl.program_id(1)
    flags = flag_ref[step]
    tq, tk = q_ref.shape[1], k_ref.shape[1]
    hd = FOX_HEAD_DIM

    def pair_lanes(pair):
        return slice(pair * LANES, (pair + 1) * LANES)

    def reset_accumulators():
        m_ref[...] = jnp.full_like(m_ref, NEG_BIG)
        acc_ref[...] = jnp.zeros_like(acc_ref)

    @pl.when(step == 0)
    def _():
        s_ref[...] = jnp.full(s_ref.shape, NEG_BIG, F32)
        smax_ref[...] = jnp.full(smax_ref.shape, NEG_BIG, F32)
        reset_accumulators()

    @pl.when((flags & FOX_FLAG_NEW_QUERY) != 0)
    def _():
        lane = lax.broadcasted_iota(jnp.int32, (1, LANES), 1)
        for pair in range(FOX_PAIRS):
            q = q_ref[0, :, pair_lanes(pair)] * (hd ** -0.5 * LOG2_E)
            qb = qbias_ref[0]
            for hh in range(2):
                in_head = (lane >= hh * hd) & (lane < (hh + 1) * hd)
                first = (2 * pair + hh) * FOX_BIAS_LANES
                in_bias = (lane >= first) & (lane < first + FOX_BIAS_LANES)
                qx_ref[2 * pair + hh, :, :LANES] = jnp.where(in_head, q, 0.0).astype(BF16)
                qx_ref[2 * pair + hh, :, LANES:] = jnp.where(in_bias, qb, jnp.zeros_like(qb))

    def attend(diagonal):
        kx = [jnp.concatenate([k_ref[0, :, pair_lanes(pair)], kbias_ref[0]],
                              axis=1) for pair in range(FOX_PAIRS)]
        probs = {}

        def score(hh):
            s = _dot_nt(kx[hh // 2], qx_ref[hh])
            if diagonal:
                visible = (lax.broadcasted_iota(jnp.int32, (tk, tq), 0)
                           <= lax.broadcasted_iota(jnp.int32, (tk, tq), 1))
                s = jnp.where(visible, s, NEG_BIG)
            s_ref[hh] = s
            smax_ref[hh] = jnp.max(s, axis=0, keepdims=True)

        def softmax(hh):
            s = s_ref[hh]
            m_prev = m_ref[hh]
            m_new = jnp.maximum(m_prev, smax_ref[hh])
            m_ref[hh] = m_new
            probs[hh] = (jnp.exp2(m_prev - m_new), jnp.exp2(s - m_new).astype(BF16))

        def value(hh):
            alpha, p = probs.pop(hh)
            acc_ref[hh] = alpha * acc_ref[hh] + _dot(vt_ref[0, hh], p)

        softmax(0)
        for hh in range(FOX_HEADS):
            score(hh)
            if hh + 1 < FOX_HEADS:
                softmax(hh + 1)
            value(hh)

    @pl.when((flags & FOX_FLAG_DIAGONAL) == 0)
    def _():
        attend(False)

    @pl.when((flags & FOX_FLAG_DIAGONAL) != 0)
    def _():
        attend(True)

    @pl.when((flags & FOX_FLAG_FINISH) != 0)
    def _():
        for pair in range(FOX_PAIRS):
            out_t = jnp.concatenate([acc_ref[hh, :hd, :] / acc_ref[hh, hd:hd + 1, :]
                                     for hh in (2 * pair, 2 * pair + 1)], axis=0)
            o_ref[0, :, pair_lanes(pair)] = out_t.T
        reset_accumulators()

    @pl.when(step == 0)
    def _():
        reset_accumulators()


def _fox(q, qb, k, kb, v_t, tq):
    bsz, seq, width = q.shape
    nq = seq // tq
    q_idx = [i for i in range(nq) for _ in range(i + 1)]
    k_idx = [j for i in range(nq) for j in range(i + 1)]
    n_pairs = len(q_idx)
    q_new = q_idx + [q_idx[-1]]
    k_new = k_idx + [k_idx[-1]]
    k_old = [0] + k_idx
    q_old = [0] + q_idx
    flags = []
    for n in range(n_pairs + 1):
        f = 0
        if n < n_pairs and k_idx[n] == 0:
            f |= FOX_FLAG_NEW_QUERY
        if n < n_pairs and k_idx[n] == q_idx[n]:
            f |= FOX_FLAG_DIAGONAL
        if n >= 1 and (n == n_pairs or k_idx[n] == 0):
            f |= FOX_FLAG_FINISH
        flags.append(f)
    tables = [jnp.asarray(t, jnp.int32) for t in (q_new, k_new, k_old, q_old, flags)]
    q_spec = pl.BlockSpec((1, tq, width), lambda b, s, qa, ka, kb_, qo, fl: (b, qa[s], 0))
    k_spec = pl.BlockSpec((1, tq, width), lambda b, s, qa, ka, kb_, qo, fl: (b, ka[s], 0))
    grid_spec = pltpu.PrefetchScalarGridSpec(
        num_scalar_prefetch=len(tables),
        grid=(bsz, n_pairs + 1),
        in_specs=[q_spec, pl.BlockSpec((1, tq, LANES), lambda b, s, qa, ka, kb_, qo, fl: (b, qa[s], 0)),
                  k_spec, pl.BlockSpec((1, tq, LANES), lambda b, s, qa, ka, kb_, qo, fl: (b, ka[s], 0)),
                  pl.BlockSpec((1, FOX_HEADS, FOX_VT_ROWS, tq),
                               lambda b, s, qa, ka, kb_, qo, fl: (b, 0, 0, kb_[s]))],
        out_specs=pl.BlockSpec((1, tq, width), lambda b, s, qa, ka, kb_, qo, fl: (b, qo[s], 0)),
        scratch_shapes=[pltpu.VMEM((FOX_HEADS, tq, 2 * LANES), BF16),
                        pltpu.VMEM((FOX_HEADS, tq, tq), F32),
                        pltpu.VMEM((FOX_HEADS, 1, tq), F32),
                        pltpu.VMEM((FOX_HEADS, 1, tq), F32),
                        pltpu.VMEM((FOX_HEADS, FOX_VT_ROWS, tq), F32)],
    )
    return pl.pallas_call(
        _fox_kernel, grid_spec=grid_spec,
        out_shape=jax.ShapeDtypeStruct((bsz, seq, width), F32),
        compiler_params=_params(("parallel", "arbitrary")),
        name="forgetting_attention",
    )(*tables, q, qb, k, kb, v_t)


def _fox_bias_lanes(split):
    bsz, _, _, seq = split.shape
    lo = 2 * MLSTM_HEADS
    pieces = jnp.swapaxes(split[:, :, lo:lo + FOX_HEADS, :], 1, 2).astype(BF16)
    ones = jnp.ones_like(pieces)

    def lanes(per_head):
        rows = per_head.reshape(bsz, FOX_HEADS * FOX_BIAS_LANES, seq)
        rows = jnp.pad(rows, ((0, 0), (0, LANES - FOX_HEADS * FOX_BIAS_LANES), (0, 0)))
        return jnp.swapaxes(rows, 1, 2)

    return (lanes(jnp.concatenate([pieces, ones], axis=2)),
            lanes(jnp.concatenate([ones, -pieces], axis=2)))


FFN_COLS = 256


def _merge_ffn_kernel(x_ref, ys_ref, ym_ref, yf_ref, gate_ref, g1_ref, sh_ref, sc_ref, g2_ref,
                      ws_ref, wm_ref, wf_ref, wo_ref, gain_ref, w1_ref, w3_ref, w2_ref, fg_ref,
                      o_ref, *, final_norm):
    d = x_ref.shape[-1]

    def gate(i):
        return _sigmoid(gate_ref[0, :, i * d:(i + 1) * d].astype(F32))

    merged = (gate(0) * _dot(ys_ref[0].astype(BF16), ws_ref[...])
              + gate(1) * _dot(ym_ref[0].astype(BF16), wm_ref[...])
              + gate(2) * _dot(yf_ref[0].astype(BF16), wf_ref[...]))
    x = x_ref[0] + g1_ref[0] * _dot(merged.astype(BF16), wo_ref[...])

    h = _rms_modulate(x, gain_ref[...], sc_ref[0], sh_ref[0]).astype(BF16)
    starts = list(range(0, w1_ref.shape[1], FFN_COLS))

    def up(c0):
        return _dot(h, w1_ref[:, c0:c0 + FFN_COLS]), _dot(h, w3_ref[:, c0:c0 + FFN_COLS])

    acc = jnp.zeros(x.shape, F32)
    nxt = up(starts[0])
    for i, c0 in enumerate(starts):
        a, b = nxt
        if i + 1 < len(starts):
            nxt = up(starts[i + 1])
        acc = acc + _dot((_silu(a) * b).astype(BF16), w2_ref[c0:c0 + FFN_COLS, :])
    y = x + g2_ref[0] * acc
    if final_norm:
        y = (y * lax.rsqrt(jnp.mean(y * y, axis=-1, keepdims=True) + NORM_EPS)) * fg_ref[...]
    o_ref[0] = y


def _merge_ffn(acts, mod_vectors, weights, final_norm, tm):
    bsz, seq, d = acts[0].shape
    tok = lambda b, i: (b, i, 0)
    vec = pl.BlockSpec((1, 1, d), lambda b, i: (b, 0, 0))
    return pl.pallas_call(
        functools.partial(_merge_ffn_kernel, final_norm=final_norm),
        grid=(bsz, seq // tm),
        in_specs=[pl.BlockSpec((1, tm, a.shape[-1]), tok) for a in acts]
        + [vec] * len(mod_vectors) + [_const_spec(w.shape) for w in weights],
        out_specs=pl.BlockSpec((1, tm, d), tok),
        out_shape=jax.ShapeDtypeStruct(acts[0].shape, F32),
        compiler_params=_params(("parallel", "parallel")),
        name="merge_swiglu",
    )(*acts, *mod_vectors, *weights)


def _pad_heads(w, heads, head_dim):
    lead = w.shape[:-1]
    w = w.reshape(lead + (heads, head_dim))
    w = jnp.pad(w, [(0, 0)] * len(lead) + [(0, 0), (0, LANES - head_dim)])
    return w.reshape(lead + (heads * LANES,))


def _layer_weights(w_in, b_in):
    sizes = (S5_WIDTH, MLSTM_WIDTH, MLSTM_WIDTH, MLSTM_WIDTH, MLSTM_HEADS, MLSTM_HEADS,
             FOX_WIDTH, FOX_WIDTH, FOX_WIDTH, FOX_HEADS)
    offs = [0]
    for n in sizes:
        offs.append(offs[-1] + n)
    both = jnp.concatenate([w_in, b_in[None, :]], axis=0)
    part = [both[:, offs[i]:offs[i + 1]] for i in range(len(sizes))] + [both[:, offs[-1]:]]
    pad_m = lambda w: _pad_heads(w, MLSTM_HEADS, MLSTM_HEAD_DIM)
    cat = jnp.concatenate([part[0], pad_m(part[1]), pad_m(part[3]),
                           part[6], part[7], part[10]], axis=1)
    small = jnp.concatenate([part[4], part[5], part[9]], axis=1)
    small = jnp.pad(small, ((0, 0), (0, GATE_ROWS - small.shape[1])))
    cat_t = jnp.concatenate([small, part[8], pad_m(part[2])], axis=1).T
    return (cat[:-1].astype(BF16), cat[-1:], cat_t[:, :-1].astype(BF16), cat_t[:, -1:])


def kernel(x, c, mod_w, mod_b, norm1_g, norm2_g, w_in, b_in, s5_a_re, s5_a_im, s5_log_dt, s5_b_re, s5_b_im, s5_c_re, s5_c_im, s5_d, s5_w_glu, mlstm_conv_w, mlstm_conv_b, mlstm_wq, mlstm_wk, mlstm_norm_g, w_up_s5, w_up_mlstm, w_up_fox, w_out, ffn_w1, ffn_w3, ffn_w2, final_g):
    bsz, seq, d = x.shape
    depth = mod_w.shape[0]
    assert seq % MLSTM_CHUNK == 0
    tm = _row_tile(seq, 512)
    tq = _row_tile(seq, 1024)
    ts = _row_tile(seq, 512)

    mod = _modulation(c, mod_w, mod_b)
    final_gain = final_g.reshape(1, d)
    for l in range(depth):
        sh1, sc1, g1, sh2, sc2, g2 = [mod[l, :, i * d:(i + 1) * d].reshape(bsz, 1, d)
                                      for i in range(6)]
        w_cat, b_cat, w_t, b_t = _layer_weights(w_in[l], b_in[l])
        (s5_u, m_qk, m_o, f_q, f_k, gates, gate_pre, f_vt, m_vt) = _input_projection(
            x, sh1, sc1, norm1_g[l].reshape(1, d), w_cat, b_cat, w_t, b_t, tm)

        gate_rows, gate_split = _gate_scan(gate_pre)
        gate_cols = jnp.swapaxes(gate_rows, 1, 2)
        f_qb, f_kb = _fox_bias_lanes(gate_split)

        bmat, cmat, astep, apow = _s5_tables(s5_a_re[l], s5_a_im[l], s5_log_dt[l], s5_b_re[l],
                                             s5_b_im[l], s5_c_re[l], s5_c_im[l])
        y_s5 = _s5(s5_u, bmat, cmat, astep, apow, s5_d[l].reshape(1, -1),
                   s5_w_glu[l].astype(BF16), ts)

        pad_hh = ((0, 0), (0, LANES - MLSTM_HEAD_DIM), (0, LANES - MLSTM_HEAD_DIM))
        y_m = _mlstm(m_qk, m_vt, m_o, gate_rows, gate_cols,
                     _pad_heads(mlstm_conv_w[l], MLSTM_HEADS, MLSTM_HEAD_DIM),
                     _pad_heads(mlstm_conv_b[l].reshape(1, -1), MLSTM_HEADS, MLSTM_HEAD_DIM),
                     jnp.pad(mlstm_wq[l], pad_hh).astype(BF16),
                     jnp.pad(mlstm_wk[l], pad_hh).astype(BF16),
                     _pad_heads(mlstm_norm_g[l].reshape(1, -1), MLSTM_HEADS, MLSTM_HEAD_DIM))

        y_f = _fox(f_q, f_qb, f_k, f_kb, f_vt, tq)

        w_up_m = _pad_heads(w_up_mlstm[l].T, MLSTM_HEADS, MLSTM_HEAD_DIM).T
        weights = (w_up_s5[l].astype(BF16), w_up_m.astype(BF16), w_up_fox[l].astype(BF16),
                   w_out[l].astype(BF16), norm2_g[l].reshape(1, d), ffn_w1[l].astype(BF16),
                   ffn_w3[l].astype(BF16), ffn_w2[l].astype(BF16), final_gain)
        x = _merge_ffn((x, y_s5, y_m, y_f, gates), (g1, sh2, sc2, g2), weights,
                       l == depth - 1, tm)
    return x
```
